```python
import math
import jax, jax.numpy as jnp
from jax import lax
import numpy as np

D_MODEL = 1024
BATCH = 8
SEQ = 2048
DEPTH = 1
DEC_BATCH = 32
DEC_SEQ = 8
PAST_LEN = 8192
PAGE_SIZE = 128

MIX_WIDTH = D_MODEL
ATT_WIDTH = MIX_WIDTH // 2
HEAD_DIM = 64
N_HEADS = ATT_WIDTH // HEAD_DIM
POOL_WIDTH = MIX_WIDTH - ATT_WIDTH
POOL_WINDOWS = (2, 4, 8, 16)
N_POOL_GROUPS = len(POOL_WINDOWS)
POOL_GROUP_DIM = POOL_WIDTH // N_POOL_GROUPS
MAX_WINDOW = max(POOL_WINDOWS)
IN_WIDTH = 3 * ATT_WIDTH + N_HEADS + POOL_WIDTH
D_FF = ((8 * D_MODEL // 3 + 127) // 128) * 128
CONV_WIDTH = 3
Q_BLOCK = 128
LN_EPS = 1e-5
ATTN_SCALE = HEAD_DIM ** -0.5
ALPHA = (2 * DEPTH) ** 0.25
BETA = (8 * DEPTH) ** -0.25
FORGET_BIAS_MIN = 3.0
FORGET_BIAS_MAX = 10.0
FORGET_W_SCALE = 0.1
NEG_INF = -1e30

kernel_name = "hymba_fox_pool_convffn_step"


def _layer_norm(x, g, b):
    xf = x.astype(jnp.float32)
    mu = jnp.mean(xf, axis=-1, keepdims=True)
    var = jnp.mean(jnp.square(xf - mu), axis=-1, keepdims=True)
    y = (xf - mu) * lax.rsqrt(var + LN_EPS) * g.astype(jnp.float32) + b.astype(jnp.float32)
    return y.astype(x.dtype)


def _mixer_proj(x, w_in, b_f):
    B, T = x.shape[0], x.shape[1]
    p = jnp.einsum('btd,de->bte', x, w_in)
    q = p[..., :ATT_WIDTH].reshape(B, T, N_HEADS, HEAD_DIM)
    k = p[..., ATT_WIDTH:2 * ATT_WIDTH].reshape(B, T, N_HEADS, HEAD_DIM)
    v = p[..., 2 * ATT_WIDTH:3 * ATT_WIDTH].reshape(B, T, N_HEADS, HEAD_DIM)
    f_logit = p[..., 3 * ATT_WIDTH:3 * ATT_WIDTH + N_HEADS]
    u = p[..., 3 * ATT_WIDTH + N_HEADS:]
    logf = jax.nn.log_sigmoid(f_logit.astype(jnp.float32) + b_f.astype(jnp.float32))
    return q, k, v, logf, u


def _attend_block(q, k, v, qc, kc, q_pos, k_pos):
    s = jnp.einsum('bqhd,bkhd->bhqk', q, k).astype(jnp.float32) * ATTN_SCALE
    s = s + jnp.swapaxes(qc, 1, 2)[..., :, None] - jnp.swapaxes(kc, 1, 2)[..., None, :]
    causal = (q_pos[:, None] >= k_pos[None, :])[None, None]
    s = jnp.where(causal, s, NEG_INF)
    p = jax.nn.softmax(s, axis=-1)
    return jnp.einsum('bhqk,bkhd->bqhd', p.astype(v.dtype), v)


def _forget_attention(q, k, v, qc, kc, q_pos, k_pos, n_past):
    T = q.shape[1]
    outs = []
    for start in range(0, T, Q_BLOCK):
        end = min(start + Q_BLOCK, T)
        kend = n_past + end
        outs.append(_attend_block(q[:, start:end], k[:, :kend], v[:, :kend],
                                  qc[:, start:end], kc[:, :kend],
                                  q_pos[start:end], k_pos[:kend]))
    return jnp.concatenate(outs, axis=1)


def _multi_pool(u, buf, pos, w_pool, pool_scale):
    B, T = u.shape[0], u.shape[1]
    full = jnp.concatenate([buf.astype(u.dtype), u], axis=1)
    cs = jnp.cumsum(full.astype(jnp.float32), axis=1)
    cs = jnp.concatenate([jnp.zeros((B, 1, POOL_WIDTH), jnp.float32), cs], axis=1)
    uf = u.astype(jnp.float32)
    groups = []
    for g, w in enumerate(POOL_WINDOWS):
        lo, hi = g * POOL_GROUP_DIM, (g + 1) * POOL_GROUP_DIM
        end = cs[:, MAX_WINDOW:MAX_WINDOW + T, lo:hi]
        start = cs[:, MAX_WINDOW - w:MAX_WINDOW - w + T, lo:hi]
        cnt = jnp.minimum(pos + 1, w).astype(jnp.float32)[None, :, None]
        groups.append((end - start) / cnt - uf[..., lo:hi])
    pooled = jnp.stack(groups, axis=2).astype(u.dtype)
    mixed = jnp.einsum('btgc,gce->btge', pooled, w_pool).reshape(B, T, POOL_WIDTH)
    return mixed * pool_scale, full[:, T:]


def _conv_ffn(x, buf, w_up, b_up, w_conv, b_conv, w_down):
    T = x.shape[1]
    h = jnp.einsum('btd,df->btf', x, w_up) + b_up
    full = jnp.concatenate([buf.astype(h.dtype), h], axis=1)
    hc = b_conv + full[:, 0:T] * w_conv[0]
    for j in range(1, CONV_WIDTH):
        hc = hc + full[:, j:j + T] * w_conv[j]
    gate, val = hc[..., :D_FF], hc[..., D_FF:]
    out = jnp.einsum('btf,fd->btd', jax.nn.gelu(gate) * val, w_down)
    return out, full[:, T:]


def _layer(x, pos, k_past, v_past, kc_past, pool_buf, conv_buf,
           w_in, b_f, w_pool, pool_scale, w_out, ln1_g, ln1_b,
           w_up, b_up, w_conv, b_conv, w_down, ln2_g, ln2_b):
    B, T = x.shape[0], x.shape[1]
    n_past = k_past.shape[1]
    q, k, v, logf, u = _mixer_proj(x, w_in, b_f)
    qc = jnp.cumsum(logf, axis=1)
    kc = jnp.concatenate([kc_past, qc], axis=1)
    k_all = jnp.concatenate([k_past.astype(k.dtype), k], axis=1)
    v_all = jnp.concatenate([v_past.astype(v.dtype), v], axis=1)
    k_pos = jnp.arange(n_past + T, dtype=jnp.int32) + (pos[0] - n_past)
    att = _forget_attention(q, k_all, v_all, qc, kc, pos, k_pos, n_past)
    att = att.reshape(B, T, ATT_WIDTH)
    pool, new_pool = _multi_pool(u, pool_buf, pos, w_pool, pool_scale)
    mix = jnp.einsum('bte,ed->btd', jnp.concatenate([att, pool], axis=-1), w_out)
    x1 = _layer_norm(ALPHA * x + mix, ln1_g, ln1_b)
    ff, new_conv = _conv_ffn(x1, conv_buf, w_up, b_up, w_conv, b_conv, w_down)
    x2 = _layer_norm(ALPHA * x1 + ff, ln2_g, ln2_b)
    return x2, k, v, logf, new_pool, new_conv


def setup_inputs(seed: int = 0) -> dict:
    key = jax.random.key(seed)
    ks = jax.random.split(key, 24)
    f32 = jnp.float32
    n_pages = PAST_LEN // PAGE_SIZE
    n_used = DEC_BATCH * n_pages
    n_pool_pages = (n_used * 5) // 4
    nrm = lambda k, s: jax.random.normal(k, s, f32)
    x_prompt = nrm(ks[0], (BATCH, SEQ, D_MODEL))
    x_sample = nrm(ks[1], (DEC_BATCH, DEC_SEQ, D_MODEL))
    b_f = jnp.linspace(FORGET_BIAS_MIN, FORGET_BIAS_MAX, N_HEADS, dtype=f32)[None, :] \
        + 0.1 * nrm(ks[9], (DEPTH, N_HEADS))
    cache_k = nrm(ks[2], (DEPTH, n_pool_pages, PAGE_SIZE, N_HEADS, HEAD_DIM))
    cache_v = nrm(ks[3], (DEPTH, n_pool_pages, PAGE_SIZE, N_HEADS, HEAD_DIM))
    cache_logf = jax.nn.log_sigmoid(b_f[:, None, None, :]
                                    + 0.3 * nrm(ks[4], (DEPTH, n_pool_pages, PAGE_SIZE, N_HEADS)))
    state_pool = nrm(ks[5], (DEPTH, DEC_BATCH, MAX_WINDOW - 1, POOL_WIDTH))
    state_conv = nrm(ks[6], (DEPTH, DEC_BATCH, CONV_WIDTH - 1, 2 * D_FF))
    page_table = jax.random.permutation(ks[7], n_pool_pages)[:n_used].reshape(DEC_BATCH, n_pages).astype(jnp.int32)
    w_in = nrm(ks[8], (DEPTH, D_MODEL, IN_WIDTH)) * D_MODEL ** -0.5
    w_in = w_in.at[:, :, 3 * ATT_WIDTH:3 * ATT_WIDTH + N_HEADS].multiply(FORGET_W_SCALE)
    w_pool = nrm(ks[10], (DEPTH, N_POOL_GROUPS, POOL_GROUP_DIM, POOL_GROUP_DIM)) * POOL_GROUP_DIM ** -0.5
    pool_scale = 1.0 + 0.1 * nrm(ks[11], (DEPTH, POOL_WIDTH))
    w_out = nrm(ks[12], (DEPTH, MIX_WIDTH, D_MODEL)) * (MIX_WIDTH ** -0.5) * BETA
    ln1_g = 1.0 + 0.05 * nrm(ks[13], (DEPTH, D_MODEL))
    ln1_b = 0.02 * nrm(ks[14], (DEPTH, D_MODEL))
    w_up = nrm(ks[15], (DEPTH, D_MODEL, 2 * D_FF)) * D_MODEL ** -0.5
    b_up = 0.02 * nrm(ks[16], (DEPTH, 2 * D_FF))
    w_conv = nrm(ks[17], (DEPTH, CONV_WIDTH, 2 * D_FF)) * CONV_WIDTH ** -0.5
    b_conv = 0.02 * nrm(ks[18], (DEPTH, 2 * D_FF))
    w_down = nrm(ks[19], (DEPTH, D_FF, D_MODEL)) * (D_FF ** -0.5) * BETA
    ln2_g = 1.0 + 0.05 * nrm(ks[20], (DEPTH, D_MODEL))
    ln2_b = 0.02 * nrm(ks[21], (DEPTH, D_MODEL))
    return {"x_prompt": x_prompt, "x_sample": x_sample,
            "cache_k": cache_k, "cache_v": cache_v, "cache_logf": cache_logf,
            "state_pool": state_pool, "state_conv": state_conv, "page_table": page_table,
            "w_in": w_in, "b_f": b_f, "w_pool": w_pool, "pool_scale": pool_scale,
            "w_out": w_out, "ln1_g": ln1_g, "ln1_b": ln1_b,
            "w_up": w_up, "b_up": b_up, "w_conv": w_conv, "b_conv": b_conv,
            "w_down": w_down, "ln2_g": ln2_g, "ln2_b": ln2_b}


def reference(x_prompt, x_sample, cache_k, cache_v, cache_logf, state_pool, state_conv,
              page_table, w_in, b_f, w_pool, pool_scale, w_out, ln1_g, ln1_b,
              w_up, b_up, w_conv, b_conv, w_down, ln2_g, ln2_b):
    B, T_p = x_prompt.shape[0], x_prompt.shape[1]
    DB, T_s = x_sample.shape[0], x_sample.shape[1]
    n_pages = page_table.shape[1]
    n_past = n_pages * PAGE_SIZE
    pos_p = jnp.arange(T_p, dtype=jnp.int32)
    pos_s = n_past + jnp.arange(T_s, dtype=jnp.int32)
    empty_kv = jnp.zeros((B, 0, N_HEADS, HEAD_DIM), x_prompt.dtype)
    empty_kc = jnp.zeros((B, 0, N_HEADS), jnp.float32)
    zero_pool = jnp.zeros((B, MAX_WINDOW - 1, POOL_WIDTH), x_prompt.dtype)
    zero_conv = jnp.zeros((B, CONV_WIDTH - 1, 2 * D_FF), x_prompt.dtype)
    yp, ys = x_prompt, x_sample
    kp_l, vp_l, fp_l, pp_l, cp_l = [], [], [], [], []
    ks_l, vs_l, fs_l, ps_l, cs_l = [], [], [], [], []
    for l in range(DEPTH):
        weights = (w_in[l], b_f[l], w_pool[l], pool_scale[l], w_out[l], ln1_g[l], ln1_b[l],
                   w_up[l], b_up[l], w_conv[l], b_conv[l], w_down[l], ln2_g[l], ln2_b[l])
        yp, kp, vp, fp, pp, cp = _layer(yp, pos_p, empty_kv, empty_kv, empty_kc,
                                        zero_pool, zero_conv, *weights)
        k_past = cache_k[l][page_table].reshape(DB, n_past, N_HEADS, HEAD_DIM)
        v_past = cache_v[l][page_table].reshape(DB, n_past, N_HEADS, HEAD_DIM)
        lf_past = cache_logf[l][page_table].reshape(DB, n_past, N_HEADS).astype(jnp.float32)
        kc_past = lf_past - lax.cumsum(lf_past, axis=1, reverse=True)
        ys, ksn, vsn, fsn, psn, csn = _layer(ys, pos_s, k_past, v_past, kc_past,
                                             state_pool[l], state_conv[l], *weights)
        kp_l.append(kp); vp_l.append(vp); fp_l.append(fp); pp_l.append(pp); cp_l.append(cp)
        ks_l.append(ksn); vs_l.append(vsn); fs_l.append(fsn); ps_l.append(psn); cs_l.append(csn)
    return (yp, ys,
            jnp.stack(kp_l, 0), jnp.stack(vp_l, 0), jnp.stack(fp_l, 0),
            jnp.stack(pp_l, 0), jnp.stack(cp_l, 0),
            jnp.stack(ks_l, 0), jnp.stack(vs_l, 0), jnp.stack(fs_l, 0),
            jnp.stack(ps_l, 0), jnp.stack(cs_l, 0))
```

```python
import functools

import jax
import jax.numpy as jnp
from jax import lax
from jax.experimental import pallas as pl
from jax.experimental.pallas import tpu as pltpu

BF = jnp.bfloat16
F32 = jnp.float32

HEAD_DIM = 64
PAGE_SIZE = 128
POOL_WINDOWS = (2, 4, 8, 16)
MAX_WINDOW = max(POOL_WINDOWS)
CONV_WIDTH = 3
LN_EPS = 1e-5
NEG_INF = -1e30
LANES = 128
SUBLANES = 8
BF_ROWS = 16
VMEM_LIMIT = 56 * 1024 * 1024


def _cparams(sem):
    return pltpu.CompilerParams(dimension_semantics=sem, vmem_limit_bytes=VMEM_LIMIT)


def _const_spec(shape):
    nd = len(shape)
    return pl.BlockSpec(shape, lambda *_: (0,) * nd, pipeline_mode=pl.Buffered(1))


def _dot(a, b):
    return jnp.dot(a, b, preferred_element_type=F32)


def _dot_nt(a, b):
    return lax.dot_general(a, b, (((1,), (1,)), ((), ())), preferred_element_type=F32)


def _layer_norm(r, g, b):
    mu = jnp.mean(r, axis=-1, keepdims=True)
    d = r - mu
    var = jnp.mean(d * d, axis=-1, keepdims=True)
    return d * lax.rsqrt(var + LN_EPS) * g + b


def _scan_incl(x, axis, reverse=False):
    n = x.shape[axis]
    idx = lax.broadcasted_iota(jnp.int32, x.shape, axis)
    s = 1
    while s < n:
        if reverse:
            x = x + jnp.where(idx + s < n, pltpu.roll(x, n - s, axis), 0.0)
        else:
            x = x + jnp.where(idx >= s, pltpu.roll(x, s, axis), 0.0)
        s *= 2
    return x


def _proj_prompt_kernel(x_ref, wr_ref, wt_ref, bfr_ref, bfc_ref,
                        q_ref, kt_ref, vt_ref, ktb_ref, vtb_ref, u_ref, lf_ref, lft_ref,
                        *, att_w, pool_w, n_heads, scale):
    xb = x_ref[0].astype(BF)
    q_ref[0] = (_dot(xb, wr_ref[:, 0:att_w]) * scale).astype(BF)
    u_ref[0] = _dot(xb, wr_ref[:, att_w:att_w + pool_w])
    z = _dot(xb, wr_ref[:, att_w + pool_w:att_w + pool_w + LANES]) + bfr_ref[...]
    lf_ref[0] = jax.nn.log_sigmoid(z)
    kt = _dot_nt(wt_ref[0:att_w, :], xb)
    kt_ref[0] = kt
    ktb_ref[0] = kt.astype(BF)
    vt = _dot_nt(wt_ref[att_w:2 * att_w, :], xb)
    vt_ref[0] = vt
    vtb_ref[0] = vt.astype(BF)
    zt = _dot_nt(wt_ref[2 * att_w:2 * att_w + BF_ROWS, :], xb)[0:n_heads] + bfc_ref[...]
    lft_ref[0] = jax.nn.log_sigmoid(zt)


def _proj_prompt(x, w_row, w_t, bf_row, bf_col, att_w, pool_w, n_heads, tm):
    bsz, t, d = x.shape
    kern = functools.partial(_proj_prompt_kernel, att_w=att_w, pool_w=pool_w, n_heads=n_heads,
                             scale=HEAD_DIM ** -0.5)
    row = lambda w: pl.BlockSpec((1, tm, w), lambda i, j: (i, j, 0))
    col = lambda h: pl.BlockSpec((1, h, tm), lambda i, j: (i, 0, j))
    return pl.pallas_call(
        kern,
        grid=(bsz, t // tm),
        in_specs=[row(d), _const_spec(w_row.shape), _const_spec(w_t.shape),
                  _const_spec((1, LANES)), _const_spec((n_heads, 1))],
        out_specs=[row(att_w), col(att_w), col(att_w), col(att_w), col(att_w), row(pool_w),
                   row(LANES), col(n_heads)],
        out_shape=[jax.ShapeDtypeStruct((bsz, t, att_w), BF),
                   jax.ShapeDtypeStruct((bsz, att_w, t), F32),
                   jax.ShapeDtypeStruct((bsz, att_w, t), F32),
                   jax.ShapeDtypeStruct((bsz, att_w, t), BF),
                   jax.ShapeDtypeStruct((bsz, att_w, t), BF),
                   jax.ShapeDtypeStruct((bsz, t, pool_w), F32),
                   jax.ShapeDtypeStruct((bsz, t, LANES), F32),
                   jax.ShapeDtypeStruct((bsz, n_heads, t), F32)],
        compiler_params=_cparams(("parallel", "parallel")),
        name="proj_prompt",
    )(x, w_row, w_t, bf_row, bf_col)


def _proj_sample_kernel(x_ref, wr_ref, wt_ref, bfr_ref, q_ref, k_ref, v_ref, kb_ref, vb_ref, u_ref, lf_ref,
                        *, att_w, pool_w, scale):
    xb = x_ref[...].astype(BF)
    q_ref[...] = (_dot(xb, wr_ref[:, 0:att_w]) * scale).astype(BF)
    u_ref[...] = _dot(xb, wr_ref[:, att_w:att_w + pool_w])
    z = _dot(xb, wr_ref[:, att_w + pool_w:att_w + pool_w + LANES]) + bfr_ref[...]
    lf_ref[...] = jax.nn.log_sigmoid(z)
    k = _dot_nt(xb, wt_ref[0:att_w, :])
    k_ref[...] = k
    kb_ref[...] = k.astype(BF)
    v = _dot_nt(xb, wt_ref[att_w:2 * att_w, :])
    v_ref[...] = v
    vb_ref[...] = v.astype(BF)


def _proj_sample(x2, w_row, w_t, bf_row, att_w, pool_w):
    n, d = x2.shape
    kern = functools.partial(_proj_sample_kernel, att_w=att_w, pool_w=pool_w, scale=HEAD_DIM ** -0.5)
    return pl.pallas_call(
        kern,
        out_shape=[jax.ShapeDtypeStruct((n, att_w), BF),
                   jax.ShapeDtypeStruct((n, att_w), F32),
                   jax.ShapeDtypeStruct((n, att_w), F32),
                   jax.ShapeDtypeStruct((n, att_w), BF),
                   jax.ShapeDtypeStruct((n, att_w), BF),
                   jax.ShapeDtypeStruct((n, pool_w), F32),
                   jax.ShapeDtypeStruct((n, LANES), F32)],
        compiler_params=pltpu.CompilerParams(vmem_limit_bytes=VMEM_LIMIT),
        name="proj_sample",
    )(x2, w_row, w_t, bf_row)


def _attn_kernel(q_ref, kt_ref, vt_ref, lf_ref, lft_ref, o_ref, c_ref, ct_ref, *, tq, n_heads):
    qi = pl.program_id(1)

    @pl.when(qi == 0)
    def _():
        c_ref[...] = _scan_incl(lf_ref[0], 0)
        ct_ref[...] = _scan_incl(lft_ref[0], 1)

    row0 = pl.multiple_of(qi * tq, tq)
    ci_all = c_ref[pl.ds(row0, tq), :]
    lane = lax.broadcasted_iota(jnp.int32, (1, LANES), 1)
    low = lane < HEAD_DIM
    r_idx = lax.broadcasted_iota(jnp.int32, (tq, tq), 0)
    c_idx = lax.broadcasted_iota(jnp.int32, (tq, tq), 1)
    causal = r_idx >= c_idx

    for pair in range(n_heads // 2):
        rows = slice(pair * LANES, (pair + 1) * LANES)
        qp = q_ref[0, :, rows]
        zero = jnp.zeros_like(qp)
        qm = (jnp.where(low, qp, zero), jnp.where(low, zero, qp))
        ci = tuple(ci_all[:, 2 * pair + hf:2 * pair + hf + 1] for hf in range(2))

        def step(j, carry, masked, pair=pair, rows=rows, qm=qm, ci=ci):
            col0 = pl.multiple_of(j * tq, tq)
            kp = kt_ref[0, rows, pl.ds(col0, tq)]
            vp = vt_ref[0, rows, pl.ds(col0, tq)]
            acc = carry[4]
            new = []
            upd = []
            for hf in range(2):
                m, l = carry[2 * hf], carry[2 * hf + 1]
                h = 2 * pair + hf
                s = _dot(qm[hf], kp) + ci[hf] - ct_ref[h:h + 1, pl.ds(col0, tq)]
                if masked:
                    s = jnp.where(causal, s, NEG_INF)
                m_new = jnp.maximum(m, jnp.max(s, axis=-1, keepdims=True))
                alpha = jnp.exp(m - m_new)
                p = jnp.exp(s - m_new)
                l_new = alpha * l + jnp.sum(p, axis=-1, keepdims=True)
                pv = _dot_nt(p.astype(BF), vp)
                new += [m_new, l_new]
                upd.append(alpha * acc + pv)
            return (new[0], new[1], new[2], new[3], jnp.where(low, upd[0], upd[1]))

        init = (jnp.full((tq, 1), NEG_INF, F32), jnp.zeros((tq, 1), F32),
                jnp.full((tq, 1), NEG_INF, F32), jnp.zeros((tq, 1), F32),
                jnp.zeros((tq, LANES), F32))
        carry = lax.fori_loop(0, qi, functools.partial(step, masked=False), init)
        m0, l0, m1, l1, acc = step(qi, carry, True)
        o_ref[0, :, rows] = (acc / jnp.where(low, l0, l1)).astype(o_ref.dtype)


def _attn(q, ktb, vtb, lf, lft, tq):
    b, t, w = q.shape
    n_heads = w // HEAD_DIM
    kern = functools.partial(_attn_kernel, tq=tq, n_heads=n_heads)
    per_b = lambda s1, s2: pl.BlockSpec((1, s1, s2), lambda i, j: (i, 0, 0))
    return pl.pallas_call(
        kern,
        grid=(b, t // tq),
        in_specs=[pl.BlockSpec((1, tq, w), lambda i, j: (i, j, 0)),
                  per_b(w, t), per_b(w, t), per_b(t, LANES), per_b(n_heads, t)],
        out_specs=pl.BlockSpec((1, tq, w), lambda i, j: (i, j, 0)),
        out_shape=jax.ShapeDtypeStruct((b, t, w), BF),
        scratch_shapes=[pltpu.VMEM((t, LANES), F32), pltpu.VMEM((n_heads, t), F32)],
        compiler_params=_cparams(("parallel", "arbitrary")),
        name="attn_prompt",
    )(q, ktb, vtb, lf, lft)


def _paged_kernel(pt_ref, q_ref, kn_ref, vn_ref, lf_ref, lft_ref, *rest, n_heads, pages, tq):
    k_refs = rest[0:pages]
    v_refs = rest[pages:2 * pages]
    lp_refs = rest[2 * pages:3 * pages]
    o_ref = rest[3 * pages]
    qbd_ref, ci_ref, m_ref, l_ref, acc_ref, carry_ref = rest[3 * pages + 1:]
    g = pl.program_id(1)
    rows = n_heads * tq
    w = q_ref.shape[2]

    def expand(x):
        return jnp.concatenate(
            [jnp.broadcast_to(x[h:h + 1, :], (tq, x.shape[1])) for h in range(n_heads)], axis=0)

    def update(s, pv_fn):
        m = m_ref[...]
        m_new = jnp.maximum(m, jnp.max(s, axis=-1, keepdims=True))
        alpha = jnp.exp(m - m_new)
        p = jnp.exp(s - m_new)
        l_ref[...] = alpha * l_ref[...] + jnp.sum(p, axis=-1, keepdims=True)
        acc_ref[...] = alpha * acc_ref[...] + pv_fn(p.astype(BF))
        m_ref[...] = m_new

    @pl.when(g == 0)
    def _():
        q = q_ref[0]
        lane = lax.broadcasted_iota(jnp.int32, (1, w), 1)
        zero = jnp.zeros_like(q)
        qbd_ref[...] = jnp.concatenate(
            [jnp.where((lane >= h * HEAD_DIM) & (lane < (h + 1) * HEAD_DIM), q, zero)
             for h in range(n_heads)], axis=0)
        c = _scan_incl(lf_ref[0], 0)
        ci_ref[...] = jnp.concatenate([c[:, h:h + 1] for h in range(n_heads)], axis=0)
        m_ref[...] = jnp.full((rows, 1), NEG_INF, F32)
        l_ref[...] = jnp.zeros((rows, 1), F32)
        acc_ref[...] = jnp.zeros((rows, w), F32)
        carry_ref[...] = jnp.zeros((n_heads, 1), F32)
        cj = _scan_incl(lft_ref[0], 1)
        s = _dot_nt(qbd_ref[...], kn_ref[0]) + ci_ref[...] - expand(cj)
        r_idx = lax.broadcasted_iota(jnp.int32, (rows, LANES), 0)
        c_idx = lax.broadcasted_iota(jnp.int32, (rows, LANES), 1)
        s = jnp.where(c_idx <= jnp.bitwise_and(r_idx, tq - 1), s, NEG_INF)
        update(s, lambda p: _dot(p, vn_ref[0]))

    ktc = jnp.concatenate([r[0].reshape(w, PAGE_SIZE) for r in k_refs], axis=1).astype(BF)
    vtc = jnp.concatenate([r[0].reshape(w, PAGE_SIZE) for r in v_refs], axis=1).astype(BF)
    lfc = jnp.concatenate([r[0] for r in lp_refs], axis=1)
    incl = _scan_incl(lfc, 1, reverse=True)
    carry = carry_ref[...]
    cj = -(carry + (incl - lfc))
    carry_ref[...] = carry + incl[:, 0:1]
    s = _dot(qbd_ref[...], ktc) + ci_ref[...] - expand(cj)
    update(s, lambda p: _dot_nt(p, vtc))

    @pl.when(g == pl.num_programs(1) - 1)
    def _():
        lane = lax.broadcasted_iota(jnp.int32, (1, w), 1)
        out = jnp.zeros((tq, w), F32)
        for h in range(n_heads):
            sl = slice(h * tq, (h + 1) * tq)
            val = acc_ref[sl, :] / l_ref[sl, :]
            out = jnp.where((lane >= h * HEAD_DIM) & (lane < (h + 1) * HEAD_DIM), val, out)
        o_ref[0] = out.astype(o_ref.dtype)


def _paged_attn(page_table, q, kn, vn, lf, lft, cache_kt, cache_vt, cache_lft, pages):
    db, tq, w = q.shape
    n_heads = w // HEAD_DIM
    n_pages = page_table.shape[1]
    ng = n_pages // pages
    rows = n_heads * tq
    assert tq & (tq - 1) == 0

    def page_spec(i, shape):
        nd = len(shape)
        return pl.BlockSpec(shape, lambda b, g, pt: (pt[b, (ng - 1 - g) * pages + i],) + (0,) * (nd - 1))

    per_b = lambda s1, s2: pl.BlockSpec((1, s1, s2), lambda b, g, pt: (b, 0, 0))
    in_specs = [per_b(tq, w), per_b(PAGE_SIZE, w), per_b(PAGE_SIZE, w), per_b(tq, LANES), per_b(n_heads, LANES)]
    in_specs += [page_spec(i, (1, n_heads, HEAD_DIM, PAGE_SIZE)) for i in range(pages)]
    in_specs += [page_spec(i, (1, n_heads, HEAD_DIM, PAGE_SIZE)) for i in range(pages)]
    in_specs += [page_spec(i, (1, n_heads, PAGE_SIZE)) for i in range(pages)]
    kern = functools.partial(_paged_kernel, n_heads=n_heads, pages=pages, tq=tq)
    grid_spec = pltpu.PrefetchScalarGridSpec(
        num_scalar_prefetch=1,
        grid=(db, ng),
        in_specs=in_specs,
        out_specs=per_b(tq, w),
        scratch_shapes=[pltpu.VMEM((rows, w), BF), pltpu.VMEM((rows, 1), F32),
                        pltpu.VMEM((rows, 1), F32), pltpu.VMEM((rows, 1), F32),
                        pltpu.VMEM((rows, w), F32), pltpu.VMEM((n_heads, 1), F32)])
    return pl.pallas_call(
        kern,
        grid_spec=grid_spec,
        out_shape=jax.ShapeDtypeStruct((db, tq, w), BF),
        compiler_params=_cparams(("parallel", "arbitrary")),
        name="attn_paged",
    )(page_table, q, kn, vn, lf, lft, *([cache_kt] * pages), *([cache_vt] * pages), *([cache_lft] * pages))


def _mix_tail(pooled, att, x, wp_ref, ps_ref, wo_ref, g_ref, b_ref, alpha):
    gd = pooled[0].shape[1]
    att_w = att.shape[1]
    mixed = [_dot(pooled[g].astype(BF), wp_ref[g]) * ps_ref[:, g * gd:(g + 1) * gd]
             for g in range(len(pooled))]
    pool = jnp.concatenate(mixed, axis=1).astype(BF)
    mix = _dot(att, wo_ref[0:att_w, :]) + _dot(pool, wo_ref[att_w:, :])
    return _layer_norm(alpha * x + mix, g_ref[...], b_ref[...])


def _mix_prompt_kernel(x_ref, att_ref, u_ref, halo_ref, wp_ref, ps_ref, wo_ref, g_ref, b_ref,
                       o_ref, ubuf_ref, *, tm, alpha):
    t = pl.program_id(1)
    hl = MAX_WINDOW
    halo = halo_ref[0]
    ubuf_ref[0:hl, :] = jnp.where(t == 0, jnp.zeros_like(halo), halo)
    ubuf_ref[hl:hl + tm, :] = u_ref[0]
    pos = t * tm + lax.broadcasted_iota(jnp.int32, (tm, 1), 0)
    gd = u_ref.shape[2] // len(POOL_WINDOWS)
    pooled = []
    for g, w in enumerate(POOL_WINDOWS):
        lanes = slice(g * gd, (g + 1) * gd)
        tok = ubuf_ref[hl:hl + tm, lanes]
        acc = tok
        for k in range(1, w):
            acc = acc + ubuf_ref[hl - k:hl - k + tm, lanes]
        cnt = jnp.minimum(pos + 1, w).astype(F32)
        pooled.append(acc / cnt - tok)
    o_ref[0] = _mix_tail(pooled, att_ref[0], x_ref[0], wp_ref, ps_ref, wo_ref, g_ref, b_ref, alpha)


def _mix_prompt(x, att, u, w_pool, pool_scale, w_out, g, b, alpha, tm):
    bsz, t, d = x.shape
    att_w, pool_w = att.shape[2], u.shape[2]
    hl = MAX_WINDOW
    ng, gd = w_pool.shape[0], w_pool.shape[1]
    kern = functools.partial(_mix_prompt_kernel, tm=tm, alpha=alpha)
    row = lambda w: pl.BlockSpec((1, tm, w), lambda i, j: (i, j, 0))
    halo = pl.BlockSpec((1, hl, pool_w), lambda i, j: (i, jnp.maximum(j * (tm // hl) - 1, 0), 0))
    return pl.pallas_call(
        kern,
        grid=(bsz, t // tm),
        in_specs=[row(d), row(att_w), row(pool_w), halo,
                  _const_spec((ng, gd, gd)), _const_spec((1, pool_w)), _const_spec((att_w + pool_w, d)),
                  _const_spec((1, d)), _const_spec((1, d))],
        out_specs=row(d),
        out_shape=jax.ShapeDtypeStruct((bsz, t, d), F32),
        scratch_shapes=[pltpu.VMEM((hl + tm, pool_w), F32)],
        compiler_params=_cparams(("parallel", "parallel")),
        name="mix_prompt",
    )(x, att, u, u, w_pool, pool_scale, w_out, g, b)


def _mix_sample_kernel(x_ref, att_ref, full_ref, wp_ref, ps_ref, wo_ref, g_ref, b_ref, o_ref,
                       *, ts, n_past, alpha):
    nseq = full_ref.shape[0]
    hl = MAX_WINDOW
    gd = full_ref.shape[2] // len(POOL_WINDOWS)
    pos = n_past + lax.broadcasted_iota(jnp.int32, (1, ts, 1), 1)
    pooled = []
    for g, w in enumerate(POOL_WINDOWS):
        lanes = slice(g * gd, (g + 1) * gd)
        tok = full_ref[:, hl:hl + ts, lanes]
        acc = tok
        for k in range(1, w):
            acc = acc + full_ref[:, hl - k:hl - k + ts, lanes]
        cnt = jnp.minimum(pos + 1, w).astype(F32)
        pooled.append((acc / cnt - tok).reshape(nseq * ts, gd))
    o_ref[...] = _mix_tail(pooled, att_ref[...], x_ref[...], wp_ref, ps_ref, wo_ref, g_ref, b_ref, alpha)


def _mix_sample(x2, att2, full, w_pool, pool_scale, w_out, g, b, alpha, ts, n_past):
    n, d = x2.shape
    kern = functools.partial(_mix_sample_kernel, ts=ts, n_past=n_past, alpha=alpha)
    return pl.pallas_call(
        kern,
        out_shape=jax.ShapeDtypeStruct((n, d), F32),
        compiler_params=pltpu.CompilerParams(vmem_limit_bytes=VMEM_LIMIT),
        name="mix_sample",
    )(x2, att2, full, w_pool, pool_scale, w_out, g, b)


def _ffn_chunk(hg3, hv3, wc_ref, bc_ref, gcols, vcols):
    def conv(h3, cols):
        hc = bc_ref[:, cols] + h3[0] * wc_ref[0:1, cols]
        for j in range(1, CONV_WIDTH):
            hc = hc + h3[j] * wc_ref[j:j + 1, cols]
        return hc
    return (jax.nn.gelu(conv(hg3, gcols)) * conv(hv3, vcols)).astype(BF)


def _ffn_prompt_kernel(x_ref, halo_ref, wu_ref, bu_ref, wc_ref, bc_ref, wd_ref, g_ref, b_ref,
                       o_ref, tail_ref, *, tm, d_ff, chunk, alpha):
    t = pl.program_id(1)
    hl = SUBLANES
    x = x_ref[0]
    xin = jnp.concatenate([halo_ref[0], x], axis=0).astype(BF)
    row = lax.broadcasted_iota(jnp.int32, (hl + tm, 1), 0)
    keep = (row >= hl) | (t > 0)
    acc = jnp.zeros((tm, x.shape[1]), F32)
    for c in range(d_ff // chunk):
        gcols = slice(c * chunk, (c + 1) * chunk)
        vcols = slice(d_ff + c * chunk, d_ff + (c + 1) * chunk)
        h3 = []
        for cols in (gcols, vcols):
            h = jnp.where(keep, _dot(xin, wu_ref[:, cols]) + bu_ref[:, cols], 0.0)
            tail_ref[0, :, cols] = h[tm:tm + hl]
            h3.append([h[hl - 2 + j:hl - 2 + j + tm] for j in range(CONV_WIDTH)])
        act = _ffn_chunk(h3[0], h3[1], wc_ref, bc_ref, gcols, vcols)
        acc = acc + _dot(act, wd_ref[gcols, :])
    o_ref[0] = _layer_norm(alpha * x + acc, g_ref[...], b_ref[...])


def _ffn_prompt(x1, w_up, b_up, w_conv, b_conv, w_down, g, b, alpha, tm, chunk):
    bsz, t, d = x1.shape
    d_ff = w_down.shape[0]
    hl = SUBLANES
    kern = functools.partial(_ffn_prompt_kernel, tm=tm, d_ff=d_ff, chunk=chunk, alpha=alpha)
    row = pl.BlockSpec((1, tm, d), lambda i, j: (i, j, 0))
    halo = pl.BlockSpec((1, hl, d), lambda i, j: (i, jnp.maximum(j * (tm // hl) - 1, 0), 0))
    return pl.pallas_call(
        kern,
        grid=(bsz, t // tm),
        in_specs=[row, halo, _const_spec((d, 2 * d_ff)), _const_spec((1, 2 * d_ff)),
                  _const_spec((CONV_WIDTH, 2 * d_ff)), _const_spec((1, 2 * d_ff)),
                  _const_spec((d_ff, d)), _const_spec((1, d)), _const_spec((1, d))],
        out_specs=[row, pl.BlockSpec((1, hl, 2 * d_ff), lambda i, j: (i, 0, 0))],
        out_shape=[jax.ShapeDtypeStruct((bsz, t, d), F32),
                   jax.ShapeDtypeStruct((bsz, hl, 2 * d_ff), F32)],
        compiler_params=_cparams(("parallel", "arbitrary")),
        name="ffn_prompt",
    )(x1, x1, w_up, b_up, w_conv, b_conv, w_down, g, b)


def _ffn_sample_kernel(x_ref, st_ref, wu_ref, bu_ref, wc_ref, bc_ref, wd_ref, g_ref, b_ref,
                       o_ref, h_ref, buf_ref, *, ts, d_ff, chunk, alpha):
    nseq = st_ref.shape[0]
    n = nseq * ts
    hl = SUBLANES
    nst = CONV_WIDTH - 1
    x = x_ref[...]
    xb = x.astype(BF)
    acc = jnp.zeros((n, x.shape[1]), F32)
    for c in range(d_ff // chunk):
        gcols = slice(c * chunk, (c + 1) * chunk)
        vcols = slice(d_ff + c * chunk, d_ff + (c + 1) * chunk)
        h3 = []
        for cols in (gcols, vcols):
            h = _dot(xb, wu_ref[:, cols]) + bu_ref[:, cols]
            h_ref[:, cols] = h
            buf_ref[:, hl - nst:hl, :] = st_ref[:, :, cols]
            buf_ref[:, hl:hl + ts, :] = h.reshape(nseq, ts, chunk)
            h3.append([buf_ref[:, hl - 2 + j:hl - 2 + j + ts, :].reshape(n, chunk)
                       for j in range(CONV_WIDTH)])
        act = _ffn_chunk(h3[0], h3[1], wc_ref, bc_ref, gcols, vcols)
        acc = acc + _dot(act, wd_ref[gcols, :])
    o_ref[...] = _layer_norm(alpha * x + acc, g_ref[...], b_ref[...])


def _ffn_sample(x2, state, w_up, b_up, w_conv, b_conv, w_down, g, b, alpha, ts, chunk):
    n, d = x2.shape
    d_ff = w_down.shape[0]
    nseq = n // ts
    kern = functools.partial(_ffn_sample_kernel, ts=ts, d_ff=d_ff, chunk=chunk, alpha=alpha)
    return pl.pallas_call(
        kern,
        out_shape=[jax.ShapeDtypeStruct((n, d), F32), jax.ShapeDtypeStruct((n, 2 * d_ff), F32)],
        scratch_shapes=[pltpu.VMEM((nseq, SUBLANES + ts, chunk), F32)],
        compiler_params=pltpu.CompilerParams(vmem_limit_bytes=VMEM_LIMIT),
        name="ffn_sample",
    )(x2, state, w_up, b_up, w_conv, b_conv, w_down, g, b)


def _ffn_chunk_size(d_ff):
    for c in (512, 256, 128):
        if d_ff % c == 0:
            return c
    raise ValueError("d_ff must be a multiple of 128")


def kernel(x_prompt, x_sample, cache_k, cache_v, cache_logf, state_pool, state_conv, page_table,
           w_in, b_f, w_pool, pool_scale, w_out, ln1_g, ln1_b, w_up, b_up, w_conv, b_conv, w_down,
           ln2_g, ln2_b):
    bsz, t_p, d = x_prompt.shape
    db, t_s, _ = x_sample.shape
    depth = w_in.shape[0]
    n_heads = cache_k.shape[3]
    att_w = n_heads * HEAD_DIM
    pool_w = state_pool.shape[3]
    d_ff = w_down.shape[1]
    n_past = page_table.shape[1] * PAGE_SIZE
    alpha = (2 * depth) ** 0.25
    assert HEAD_DIM * 2 == LANES and n_heads % 2 == 0 and n_heads <= SUBLANES
    assert t_s == SUBLANES and cache_k.shape[2] == PAGE_SIZE and t_p >= MAX_WINDOW
    assert w_in.shape[2] == 3 * att_w + n_heads + pool_w

    tm_proj = min(512, t_p)
    tq = min(256, t_p)
    tm_mix = min(256, t_p)
    tm_ffn = min(256, t_p)
    pages = 8 if page_table.shape[1] % 8 == 0 else 1
    chunk = _ffn_chunk_size(d_ff)

    yp, ys = x_prompt, x_sample
    outs = [[] for _ in range(10)]
    for l in range(depth):
        wi = w_in[l]
        fcols = wi[:, 3 * att_w:3 * att_w + n_heads]
        w_row = jnp.concatenate(
            [wi[:, :att_w], wi[:, 3 * att_w + n_heads:], jnp.pad(fcols, ((0, 0), (0, LANES - n_heads)))],
            axis=1).astype(BF)
        w_t = jnp.concatenate(
            [wi[:, att_w:3 * att_w].T, jnp.pad(fcols.T, ((0, BF_ROWS - n_heads), (0, 0)))],
            axis=0).astype(BF)
        bf_row = jnp.pad(b_f[l], (0, LANES - n_heads)).reshape(1, LANES)
        bf_col = b_f[l].reshape(n_heads, 1)
        wp_b = w_pool[l].astype(BF)
        ps = pool_scale[l].reshape(1, pool_w)
        wo_b = w_out[l].astype(BF)
        g1, b1 = ln1_g[l].reshape(1, d), ln1_b[l].reshape(1, d)
        wu_b = w_up[l].astype(BF)
        bu = b_up[l].reshape(1, 2 * d_ff)
        wc = w_conv[l]
        bc = b_conv[l].reshape(1, 2 * d_ff)
        wd_b = w_down[l].astype(BF)
        g2, b2 = ln2_g[l].reshape(1, d), ln2_b[l].reshape(1, d)

        q, kt, vt, ktb, vtb, u, lf, lft = _proj_prompt(yp, w_row, w_t, bf_row, bf_col, att_w, pool_w,
                                                       n_heads, tm_proj)
        att = _attn(q, ktb, vtb, lf, lft, tq)
        x1 = _mix_prompt(yp, att, u, wp_b, ps, wo_b, g1, b1, alpha, tm_mix)
        yp, tail = _ffn_prompt(x1, wu_b, bu, wc, bc, wd_b, g2, b2, alpha, tm_ffn, chunk)
        to_bthd = lambda a: jnp.transpose(a.reshape(bsz, n_heads, HEAD_DIM, t_p), (0, 3, 1, 2))
        outs[0].append(to_bthd(kt))
        outs[1].append(to_bthd(vt))
        outs[2].append(jnp.swapaxes(lft, 1, 2))
        outs[3].append(u[:, t_p - (MAX_WINDOW - 1):, :])
        outs[4].append(tail[:, SUBLANES - (CONV_WIDTH - 1):, :])

        n_s = db * t_s
        qs, ks, vs, kbs, vbs, us, lfs = _proj_sample(ys.reshape(n_s, d), w_row, w_t, bf_row, att_w, pool_w)
        lfs3 = lfs.reshape(db, t_s, LANES)
        lfts = jnp.pad(jnp.swapaxes(lfs3[:, :, :n_heads], 1, 2), ((0, 0), (0, 0), (0, LANES - t_s)))
        padk = lambda a: jnp.pad(a.reshape(db, t_s, att_w), ((0, 0), (0, PAGE_SIZE - t_s), (0, 0)))
        ckt = jnp.transpose(cache_k[l], (0, 2, 3, 1))
        cvt = jnp.transpose(cache_v[l], (0, 2, 3, 1))
        clft = jnp.swapaxes(cache_logf[l], 1, 2)
        att_s = _paged_attn(page_table, qs.reshape(db, t_s, att_w), padk(kbs), padk(vbs), lfs3, lfts,
                            ckt, cvt, clft, pages)
        us3 = us.reshape(db, t_s, pool_w)
        full = jnp.concatenate([jnp.zeros((db, 1, pool_w), F32), state_pool[l], us3], axis=1)
        x1s = _mix_sample(ys.reshape(n_s, d), att_s.reshape(n_s, att_w), full, wp_b, ps, wo_b, g1, b1,
                          alpha, t_s, n_past)
        ys2, hs = _ffn_sample(x1s, state_conv[l], wu_b, bu, wc, bc, wd_b, g2, b2, alpha, t_s, chunk)
        ys = ys2.reshape(db, t_s, d)
        outs[5].append(ks.reshape(db, t_s, n_heads, HEAD_DIM))
        outs[6].append(vs.reshape(db, t_s, n_heads, HEAD_DIM))
        outs[7].append(lfs3[:, :, :n_heads])
        outs[8].append(full[:, t_s + 1:, :])
        outs[9].append(hs.reshape(db, t_s, 2 * d_ff)[:, t_s - (CONV_WIDTH - 1):, :])

    st = [jnp.stack(o, 0) for o in outs]
    return (yp, ys, st[0], st[1], st[2], st[3], st[4], st[5], st[6], st[7], st[8], st[9])
```

```python
import functools

import jax
import jax.numpy as jnp
from jax import lax
from jax.experimental import pallas as pl
from jax.experimental.pallas import tpu as pltpu

BF = jnp.bfloat16
F32 = jnp.float32

HEAD_DIM = 64
PAGE_SIZE = 128
POOL_WINDOWS = (2, 4, 8, 16)
MAX_WINDOW = max(POOL_WINDOWS)
CONV_WIDTH = 3
LN_EPS = 1e-5
NEG_INF = -1e30
LANES = 128
SUBLANES = 8
BF_ROWS = 16
VMEM_LIMIT = 56 * 1024 * 1024


def _cparams(sem):
    return pltpu.CompilerParams(dimension_semantics=sem, vmem_limit_bytes=VMEM_LIMIT)


def _const_spec(shape):
    nd = len(shape)
    return pl.BlockSpec(shape, lambda *_: (0,) * nd, pipeline_mode=pl.Buffered(1))


def _dot(a, b):
    return jnp.dot(a, b, preferred_element_type=F32)


def _dot_nt(a, b):
    return lax.dot_general(a, b, (((1,), (1,)), ((), ())), preferred_element_type=F32)


def _layer_norm(r, g, b):
    mu = jnp.mean(r, axis=-1, keepdims=True)
    d = r - mu
    var = jnp.mean(d * d, axis=-1, keepdims=True)
    return d * lax.rsqrt(var + LN_EPS) * g + b


def _scan_incl(x, axis, reverse=False):
    n = x.shape[axis]
    idx = lax.broadcasted_iota(jnp.int32, x.shape, axis)
    s = 1
    while s < n:
        if reverse:
            x = x + jnp.where(idx + s < n, pltpu.roll(x, n - s, axis), 0.0)
        else:
            x = x + jnp.where(idx >= s, pltpu.roll(x, s, axis), 0.0)
        s *= 2
    return x


def _proj_prompt_kernel(x_ref, wr_ref, wt_ref, bfr_ref, bfc_ref,
                        qt_ref, kb_ref, kt_ref, vt_ref, vtb_ref, u_ref, lf_ref, lft_ref,
                        *, att_w, pool_w, n_heads, scale):
    xb = x_ref[0].astype(BF)
    kb_ref[0] = _dot(xb, wr_ref[:, att_w:2 * att_w]).astype(BF)
    u_ref[0] = _dot(xb, wr_ref[:, 2 * att_w:2 * att_w + pool_w])
    z = _dot(xb, wr_ref[:, 2 * att_w + pool_w:2 * att_w + pool_w + LANES]) + bfr_ref[...]
    lf_ref[0] = jax.nn.log_sigmoid(z)
    qt_ref[0] = (_dot_nt(wt_ref[0:att_w, :], xb) * scale).astype(BF)
    kt_ref[0] = _dot_nt(wt_ref[att_w:2 * att_w, :], xb)
    vt = _dot_nt(wt_ref[2 * att_w:3 * att_w, :], xb)
    vt_ref[0] = vt
    vtb_ref[0] = vt.astype(BF)
    zt = _dot_nt(wt_ref[3 * att_w:3 * att_w + BF_ROWS, :], xb)[0:n_heads] + bfc_ref[...]
    lft_ref[0] = jax.nn.log_sigmoid(zt)


def _proj_prompt(x, w_row, w_t, bf_row, bf_col, att_w, pool_w, n_heads, tm):
    bsz, t, d = x.shape
    kern = functools.partial(_proj_prompt_kernel, att_w=att_w, pool_w=pool_w, n_heads=n_heads,
                             scale=HEAD_DIM ** -0.5)
    row = lambda w: pl.BlockSpec((1, tm, w), lambda i, j: (i, j, 0))
    col = lambda h: pl.BlockSpec((1, h, tm), lambda i, j: (i, 0, j))
    return pl.pallas_call(
        kern,
        grid=(bsz, t // tm),
        in_specs=[row(d), _const_spec(w_row.shape), _const_spec(w_t.shape),
                  _const_spec((1, LANES)), _const_spec((n_heads, 1))],
        out_specs=[col(att_w), row(att_w), col(att_w), col(att_w), col(att_w), row(pool_w),
                   row(LANES), col(n_heads)],
        out_shape=[jax.ShapeDtypeStruct((bsz, att_w, t), BF),
                   jax.ShapeDtypeStruct((bsz, t, att_w), BF),
                   jax.ShapeDtypeStruct((bsz, att_w, t), F32),
                   jax.ShapeDtypeStruct((bsz, att_w, t), F32),
                   jax.ShapeDtypeStruct((bsz, att_w, t), BF),
                   jax.ShapeDtypeStruct((bsz, t, pool_w), F32),
                   jax.ShapeDtypeStruct((bsz, t, LANES), F32),
                   jax.ShapeDtypeStruct((bsz, n_heads, t), F32)],
        compiler_params=_cparams(("parallel", "parallel")),
        name="proj_prompt",
    )(x, w_row, w_t, bf_row, bf_col)


def _proj_sample_kernel(x_ref, wr_ref, wt_ref, bfr_ref, q_ref, k_ref, v_ref, kb_ref, vb_ref, u_ref, lf_ref,
                        *, att_w, pool_w, scale):
    xb = x_ref[...].astype(BF)
    q_ref[...] = (_dot(xb, wr_ref[:, 0:att_w]) * scale).astype(BF)
    u_ref[...] = _dot(xb, wr_ref[:, 2 * att_w:2 * att_w + pool_w])
    z = _dot(xb, wr_ref[:, 2 * att_w + pool_w:2 * att_w + pool_w + LANES]) + bfr_ref[...]
    lf_ref[...] = jax.nn.log_sigmoid(z)
    k = _dot(xb, wr_ref[:, att_w:2 * att_w])
    k_ref[...] = k
    kb_ref[...] = k.astype(BF)
    v = _dot_nt(xb, wt_ref[2 * att_w:3 * att_w, :])
    v_ref[...] = v
    vb_ref[...] = v.astype(BF)


def _proj_sample(x2, w_row, w_t, bf_row, att_w, pool_w):
    n, d = x2.shape
    kern = functools.partial(_proj_sample_kernel, att_w=att_w, pool_w=pool_w, scale=HEAD_DIM ** -0.5)
    return pl.pallas_call(
        kern,
        out_shape=[jax.ShapeDtypeStruct((n, att_w), BF),
                   jax.ShapeDtypeStruct((n, att_w), F32),
                   jax.ShapeDtypeStruct((n, att_w), F32),
                   jax.ShapeDtypeStruct((n, att_w), BF),
                   jax.ShapeDtypeStruct((n, att_w), BF),
                   jax.ShapeDtypeStruct((n, pool_w), F32),
                   jax.ShapeDtypeStruct((n, LANES), F32)],
        compiler_params=pltpu.CompilerParams(vmem_limit_bytes=VMEM_LIMIT),
        name="proj_sample",
    )(x2, w_row, w_t, bf_row)


KEY_CHUNK = 32


def _group_reduce(fn, x):
    parts = [x[i * SUBLANES:(i + 1) * SUBLANES] for i in range(x.shape[0] // SUBLANES)]
    return functools.reduce(fn, parts)


def _attn_kernel(qt_ref, k_ref, vt_ref, lf_ref, lft_ref, o_ref, c_ref, ct_ref, qh_ref, s_ref, p_ref,
                 *, tq, n_heads):
    qi = pl.program_id(1)
    tk = tq
    rc = KEY_CHUNK

    @pl.when(qi == 0)
    def _():
        c_ref[...] = _scan_incl(lf_ref[0], 0)
        ct_ref[...] = _scan_incl(lft_ref[0], 1)

    q0 = pl.multiple_of(qi * tq, tq)
    first = lax.broadcasted_iota(jnp.int32, (LANES, 1), 0) < HEAD_DIM
    key_i = lax.broadcasted_iota(jnp.int32, (rc, tq), 0)
    qry_i = lax.broadcasted_iota(jnp.int32, (rc, tq), 1)

    heads = range(n_heads)
    for pair in range(n_heads // 2):
        qtp = qt_ref[0, pair * LANES:(pair + 1) * LANES, :]
        zero = jnp.zeros_like(qtp)
        qh_ref[2 * pair] = jnp.where(first, qtp, zero)
        qh_ref[2 * pair + 1] = jnp.where(first, zero, qtp)
    ci = [ct_ref[h:h + 1, pl.ds(q0, tq)] for h in heads]

    def step(j, carry, masked):
        k0 = pl.multiple_of(j * tk, tk)
        for h in heads:
            kb = k_ref[0, pl.ds(k0, tk), (h // 2) * LANES:(h // 2 + 1) * LANES]
            s_ref[h] = _dot(kb, qh_ref[h])
        m_new, alpha = [], []
        for h in heads:
            mb = jnp.full((SUBLANES, tq), NEG_INF, F32)
            for r in range(tk // rc):
                rs = slice(r * rc, (r + 1) * rc)
                cj = c_ref[pl.ds(pl.multiple_of(k0 + r * rc, rc), rc), h:h + 1]
                s = s_ref[h, rs, :] + ci[h] - cj
                if masked:
                    s = jnp.where(key_i + r * rc <= qry_i, s, NEG_INF)
                s_ref[h, rs, :] = s
                mb = jnp.maximum(mb, _group_reduce(jnp.maximum, s))
            m_new.append(jnp.maximum(carry[3 * h], jnp.max(mb, axis=0, keepdims=True)))
            alpha.append(jnp.exp(carry[3 * h] - m_new[h]))
        l_new = []
        for h in heads:
            lb = jnp.zeros((SUBLANES, tq), F32)
            for r in range(tk // rc):
                rs = slice(r * rc, (r + 1) * rc)
                p = jnp.exp(s_ref[h, rs, :] - m_new[h])
                lb = lb + _group_reduce(jnp.add, p)
                p_ref[h, rs, :] = p.astype(BF)
            l_new.append(alpha[h] * carry[3 * h + 1] + jnp.sum(lb, axis=0, keepdims=True))
        out = []
        for h in heads:
            vh = vt_ref[0, h * HEAD_DIM:(h + 1) * HEAD_DIM, pl.ds(k0, tk)]
            out += [m_new[h], l_new[h], alpha[h] * carry[3 * h + 2] + _dot(vh, p_ref[h])]
        return tuple(out)

    init = (jnp.full((1, tq), NEG_INF, F32), jnp.zeros((1, tq), F32),
            jnp.zeros((HEAD_DIM, tq), F32)) * n_heads
    carry = lax.fori_loop(0, qi, functools.partial(step, masked=False), init)
    fin = step(qi, carry, True)
    for pair in range(n_heads // 2):
        h0, h1 = 2 * pair, 2 * pair + 1
        out_t = jnp.concatenate([fin[3 * h0 + 2] / fin[3 * h0 + 1],
                                 fin[3 * h1 + 2] / fin[3 * h1 + 1]], axis=0)
        o_ref[0, :, pair * LANES:(pair + 1) * LANES] = out_t.T.astype(o_ref.dtype)


def _attn(qt, kb, vtb, lf, lft, tq):
    b, w, t = qt.shape
    n_heads = w // HEAD_DIM
    kern = functools.partial(_attn_kernel, tq=tq, n_heads=n_heads)
    per_b = lambda s1, s2: pl.BlockSpec((1, s1, s2), lambda i, j: (i, 0, 0))
    return pl.pallas_call(
        kern,
        grid=(b, t // tq),
        in_specs=[pl.BlockSpec((1, w, tq), lambda i, j: (i, 0, j)),
                  per_b(t, w), per_b(w, t), per_b(t, LANES), per_b(n_heads, t)],
        out_specs=pl.BlockSpec((1, tq, w), lambda i, j: (i, j, 0)),
        out_shape=jax.ShapeDtypeStruct((b, t, w), BF),
        scratch_shapes=[pltpu.VMEM((t, LANES), F32), pltpu.VMEM((n_heads, t), F32),
                        pltpu.VMEM((n_heads, LANES, tq), BF),
                        pltpu.VMEM((n_heads, tq, tq), F32), pltpu.VMEM((n_heads, tq, tq), BF)],
        compiler_params=_cparams(("parallel", "arbitrary")),
        name="attn_prompt",
    )(qt, kb, vtb, lf, lft)


def _paged_kernel(pt_ref, q_ref, kn_ref, vn_ref, lf_ref, lft_ref, *rest, n_heads, pages, tq):
    k_refs = rest[0:pages]
    v_refs = rest[pages:2 * pages]
    lp_refs = rest[2 * pages:3 * pages]
    o_ref = rest[3 * pages]
    qbd_ref, ci_ref, m_ref, l_ref, acc_ref, carry_ref = rest[3 * pages + 1:]
    g = pl.program_id(1)
    rows = n_heads * tq
    w = q_ref.shape[2]

    def expand(x):
        return jnp.concatenate(
            [jnp.broadcast_to(x[h:h + 1, :], (tq, x.shape[1])) for h in range(n_heads)], axis=0)

    def update(s, pv_fn):
        m = m_ref[...]
        m_new = jnp.maximum(m, jnp.max(s, axis=-1, keepdims=True))
        alpha = jnp.exp(m - m_new)
        p = jnp.exp(s - m_new)
        l_ref[...] = alpha * l_ref[...] + jnp.sum(p, axis=-1, keepdims=True)
        acc_ref[...] = alpha * acc_ref[...] + pv_fn(p.astype(BF))
        m_ref[...] = m_new

    @pl.when(g == 0)
    def _():
        q = q_ref[0]
        lane = lax.broadcasted_iota(jnp.int32, (1, w), 1)
        zero = jnp.zeros_like(q)
        qbd_ref[...] = jnp.concatenate(
            [jnp.where((lane >= h * HEAD_DIM) & (lane < (h + 1) * HEAD_DIM), q, zero)
             for h in range(n_heads)], axis=0)
        c = _scan_incl(lf_ref[0], 0)
        ci_ref[...] = jnp.concatenate([c[:, h:h + 1] for h in range(n_heads)], axis=0)
        m_ref[...] = jnp.full((rows, 1), NEG_INF, F32)
        l_ref[...] = jnp.zeros((rows, 1), F32)
        acc_ref[...] = jnp.zeros((rows, w), F32)
        carry_ref[...] = jnp.zeros((n_heads, 1), F32)
        cj = _scan_incl(lft_ref[0], 1)
        s = _dot_nt(qbd_ref[...], kn_ref[0]) + ci_ref[...] - expand(cj)
        r_idx = lax.broadcasted_iota(jnp.int32, (rows, LANES), 0)
        c_idx = lax.broadcasted_iota(jnp.int32, (rows, LANES), 1)
        s = jnp.where(c_idx <= jnp.bitwise_and(r_idx, tq - 1), s, NEG_INF)
        update(s, lambda p: _dot(p, vn_ref[0]))

    ktc = jnp.concatenate([r[0].reshape(w, PAGE_SIZE) for r in k_refs], axis=1).astype(BF)
    vtc = jnp.concatenate([r[0].reshape(w, PAGE_SIZE) for r in v_refs], axis=1).astype(BF)
    lfc = jnp.concatenate([r[0] for r in lp_refs], axis=1)
    incl = _scan_incl(lfc, 1, reverse=True)
    carry = carry_ref[...]
    cj = -(carry + (incl - lfc))
    carry_ref[...] = carry + incl[:, 0:1]
    s = _dot(qbd_ref[...], ktc) + ci_ref[...] - expand(cj)
    update(s, lambda p: _dot_nt(p, vtc))

    @pl.when(g == pl.num_programs(1) - 1)
    def _():
        lane = lax.broadcasted_iota(jnp.int32, (1, w), 1)
        out = jnp.zeros((tq, w), F32)
        for h in range(n_heads):
            sl = slice(h * tq, (h + 1) * tq)
            val = acc_ref[sl, :] / l_ref[sl, :]
            out = jnp.where((lane >= h * HEAD_DIM) & (lane < (h + 1) * HEAD_DIM), val, out)
        o_ref[0] = out.astype(o_ref.dtype)


def _paged_attn(page_table, q, kn, vn, lf, lft, cache_kt, cache_vt, cache_lft, pages):
    db, tq, w = q.shape
    n_heads = w // HEAD_DIM
    n_pages = page_table.shape[1]
    ng = n_pages // pages
    rows = n_heads * tq
    assert tq & (tq - 1) == 0

    def page_spec(i, shape):
        nd = len(shape)
        return pl.BlockSpec(shape, lambda b, g, pt: (pt[b, (ng - 1 - g) * pages + i],) + (0,) * (nd - 1))

    per_b = lambda s1, s2: pl.BlockSpec((1, s1, s2), lambda b, g, pt: (b, 0, 0))
    in_specs = [per_b(tq, w), per_b(PAGE_SIZE, w), per_b(PAGE_SIZE, w), per_b(tq, LANES), per_b(n_heads, LANES)]
    in_specs += [page_spec(i, (1, n_heads, HEAD_DIM, PAGE_SIZE)) for i in range(pages)]
    in_specs += [page_spec(i, (1, n_heads, HEAD_DIM, PAGE_SIZE)) for i in range(pages)]
    in_specs += [page_spec(i, (1, n_heads, PAGE_SIZE)) for i in range(pages)]
    kern = functools.partial(_paged_kernel, n_heads=n_heads, pages=pages, tq=tq)
    grid_spec = pltpu.PrefetchScalarGridSpec(
        num_scalar_prefetch=1,
        grid=(db, ng),
        in_specs=in_specs,
        out_specs=per_b(tq, w),
        scratch_shapes=[pltpu.VMEM((rows, w), BF), pltpu.VMEM((rows, 1), F32),
                        pltpu.VMEM((rows, 1), F32), pltpu.VMEM((rows, 1), F32),
                        pltpu.VMEM((rows, w), F32), pltpu.VMEM((n_heads, 1), F32)])
    return pl.pallas_call(
        kern,
        grid_spec=grid_spec,
        out_shape=jax.ShapeDtypeStruct((db, tq, w), BF),
        compiler_params=_cparams(("parallel", "arbitrary")),
        name="attn_paged",
    )(page_table, q, kn, vn, lf, lft, *([cache_kt] * pages), *([cache_vt] * pages), *([cache_lft] * pages))


def _mix_tail(pooled, att, x, wp_ref, ps_ref, wo_ref, g_ref, b_ref, alpha):
    gd = pooled[0].shape[1]
    att_w = att.shape[1]
    mixed = [_dot(pooled[g].astype(BF), wp_ref[g]) * ps_ref[:, g * gd:(g + 1) * gd]
             for g in range(len(pooled))]
    pool = jnp.concatenate(mixed, axis=1).astype(BF)
    mix = _dot(att, wo_ref[0:att_w, :]) + _dot(pool, wo_ref[att_w:, :])
    return _layer_norm(alpha * x + mix, g_ref[...], b_ref[...])


def _mix_prompt_kernel(x_ref, att_ref, u_ref, halo_ref, wp_ref, ps_ref, wo_ref, g_ref, b_ref,
                       o_ref, ubuf_ref, *, tm, alpha):
    t = pl.program_id(1)
    hl = MAX_WINDOW
    halo = halo_ref[0]
    ubuf_ref[0:hl, :] = jnp.where(t == 0, jnp.zeros_like(halo), halo)
    ubuf_ref[hl:hl + tm, :] = u_ref[0]
    pos = t * tm + lax.broadcasted_iota(jnp.int32, (tm, 1), 0)
    gd = u_ref.shape[2] // len(POOL_WINDOWS)
    pooled = []
    for g, w in enumerate(POOL_WINDOWS):
        lanes = slice(g * gd, (g + 1) * gd)
        tok = ubuf_ref[hl:hl + tm, lanes]
        acc = tok
        for k in range(1, w):
            acc = acc + ubuf_ref[hl - k:hl - k + tm, lanes]
        cnt = jnp.minimum(pos + 1, w).astype(F32)
        pooled.append(acc / cnt - tok)
    o_ref[0] = _mix_tail(pooled, att_ref[0], x_ref[0], wp_ref, ps_ref, wo_ref, g_ref, b_ref, alpha)


def _mix_prompt(x, att, u, w_pool, pool_scale, w_out, g, b, alpha, tm):
    bsz, t, d = x.shape
    att_w, pool_w = att.shape[2], u.shape[2]
    hl = MAX_WINDOW
    ng, gd = w_pool.shape[0], w_pool.shape[1]
    kern = functools.partial(_mix_prompt_kernel, tm=tm, alpha=alpha)
    row = lambda w: pl.BlockSpec((1, tm, w), lambda i, j: (i, j, 0))
    halo = pl.BlockSpec((1, hl, pool_w), lambda i, j: (i, jnp.maximum(j * (tm // hl) - 1, 0), 0))
    return pl.pallas_call(
        kern,
        grid=(bsz, t // tm),
        in_specs=[row(d), row(att_w), row(pool_w), halo,
                  _const_spec((ng, gd, gd)), _const_spec((1, pool_w)), _const_spec((att_w + pool_w, d)),
                  _const_spec((1, d)), _const_spec((1, d))],
        out_specs=row(d),
        out_shape=jax.ShapeDtypeStruct((bsz, t, d), F32),
        scratch_shapes=[pltpu.VMEM((hl + tm, pool_w), F32)],
        compiler_params=_cparams(("parallel", "parallel")),
        name="mix_prompt",
    )(x, att, u, u, w_pool, pool_scale, w_out, g, b)


def _mix_sample_kernel(x_ref, att_ref, full_ref, wp_ref, ps_ref, wo_ref, g_ref, b_ref, o_ref,
                       *, ts, n_past, alpha):
    nseq = full_ref.shape[0]
    hl = MAX_WINDOW
    gd = full_ref.shape[2] // len(POOL_WINDOWS)
    pos = n_past + lax.broadcasted_iota(jnp.int32, (1, ts, 1), 1)
    pooled = []
    for g, w in enumerate(POOL_WINDOWS):
        lanes = slice(g * gd, (g + 1) * gd)
        tok = full_ref[:, hl:hl + ts, lanes]
        acc = tok
        for k in range(1, w):
            acc = acc + full_ref[:, hl - k:hl - k + ts, lanes]
        cnt = jnp.minimum(pos + 1, w).astype(F32)
        pooled.append((acc / cnt - tok).reshape(nseq * ts, gd))
    o_ref[...] = _mix_tail(pooled, att_ref[...], x_ref[...], wp_ref, ps_ref, wo_ref, g_ref, b_ref, alpha)


def _mix_sample(x2, att2, full, w_pool, pool_scale, w_out, g, b, alpha, ts, n_past):
    n, d = x2.shape
    kern = functools.partial(_mix_sample_kernel, ts=ts, n_past=n_past, alpha=alpha)
    return pl.pallas_call(
        kern,
        out_shape=jax.ShapeDtypeStruct((n, d), F32),
        compiler_params=pltpu.CompilerParams(vmem_limit_bytes=VMEM_LIMIT),
        name="mix_sample",
    )(x2, att2, full, w_pool, pool_scale, w_out, g, b)


def _ffn_chunk(hg3, hv3, wc_ref, bc_ref, gcols, vcols):
    def conv(h3, cols):
        hc = bc_ref[:, cols] + h3[0] * wc_ref[0:1, cols]
        for j in range(1, CONV_WIDTH):
            hc = hc + h3[j] * wc_ref[j:j + 1, cols]
        return hc
    return (jax.nn.gelu(conv(hg3, gcols)) * conv(hv3, vcols)).astype(BF)


def _ffn_prompt_kernel(x_ref, halo_ref, wu_ref, bu_ref, wc_ref, bc_ref, wd_ref, g_ref, b_ref,
                       o_ref, tail_ref, *, tm, d_ff, chunk, alpha):
    t = pl.program_id(1)
    hl = SUBLANES
    x = x_ref[0]
    xin = jnp.concatenate([halo_ref[0], x], axis=0).astype(BF)
    row = lax.broadcasted_iota(jnp.int32, (hl + tm, 1), 0)
    keep = (row >= hl) | (t > 0)
    acc = jnp.zeros((tm, x.shape[1]), F32)
    for c in range(d_ff // chunk):
        gcols = slice(c * chunk, (c + 1) * chunk)
        vcols = slice(d_ff + c * chunk, d_ff + (c + 1) * chunk)
        h3 = []
        for cols in (gcols, vcols):
            h = jnp.where(keep, _dot(xin, wu_ref[:, cols]) + bu_ref[:, cols], 0.0)
            tail_ref[0, :, cols] = h[tm:tm + hl]
            h3.append([h[hl - 2 + j:hl - 2 + j + tm] for j in range(CONV_WIDTH)])
        act = _ffn_chunk(h3[0], h3[1], wc_ref, bc_ref, gcols, vcols)
        acc = acc + _dot(act, wd_ref[gcols, :])
    o_ref[0] = _layer_norm(alpha * x + acc, g_ref[...], b_ref[...])


def _ffn_prompt(x1, w_up, b_up, w_conv, b_conv, w_down, g, b, alpha, tm, chunk):
    bsz, t, d = x1.shape
    d_ff = w_down.shape[0]
    hl = SUBLANES
    kern = functools.partial(_ffn_prompt_kernel, tm=tm, d_ff=d_ff, chunk=chunk, alpha=alpha)
    row = pl.BlockSpec((1, tm, d), lambda i, j: (i, j, 0))
    halo = pl.BlockSpec((1, hl, d), lambda i, j: (i, jnp.maximum(j * (tm // hl) - 1, 0), 0))
    return pl.pallas_call(
        kern,
        grid=(bsz, t // tm),
        in_specs=[row, halo, _const_spec((d, 2 * d_ff)), _const_spec((1, 2 * d_ff)),
                  _const_spec((CONV_WIDTH, 2 * d_ff)), _const_spec((1, 2 * d_ff)),
                  _const_spec((d_ff, d)), _const_spec((1, d)), _const_spec((1, d))],
        out_specs=[row, pl.BlockSpec((1, hl, 2 * d_ff), lambda i, j: (i, 0, 0))],
        out_shape=[jax.ShapeDtypeStruct((bsz, t, d), F32),
                   jax.ShapeDtypeStruct((bsz, hl, 2 * d_ff), F32)],
        compiler_params=_cparams(("parallel", "arbitrary")),
        name="ffn_prompt",
    )(x1, x1, w_up, b_up, w_conv, b_conv, w_down, g, b)


def _ffn_sample_kernel(x_ref, st_ref, wu_ref, bu_ref, wc_ref, bc_ref, wd_ref, g_ref, b_ref,
                       o_ref, h_ref, buf_ref, *, ts, d_ff, chunk, alpha):
    nseq = st_ref.shape[0]
    n = nseq * ts
    hl = SUBLANES
    nst = CONV_WIDTH - 1
    x = x_ref[...]
    xb = x.astype(BF)
    acc = jnp.zeros((n, x.shape[1]), F32)
    for c in range(d_ff // chunk):
        gcols = slice(c * chunk, (c + 1) * chunk)
        vcols = slice(d_ff + c * chunk, d_ff + (c + 1) * chunk)
        h3 = []
        for cols in (gcols, vcols):
            h = _dot(xb, wu_ref[:, cols]) + bu_ref[:, cols]
            h_ref[:, cols] = h
            buf_ref[:, hl - nst:hl, :] = st_ref[:, :, cols]
            buf_ref[:, hl:hl + ts, :] = h.reshape(nseq, ts, chunk)
            h3.append([buf_ref[:, hl - 2 + j:hl - 2 + j + ts, :].reshape(n, chunk)
                       for j in range(CONV_WIDTH)])
        act = _ffn_chunk(h3[0], h3[1], wc_ref, bc_ref, gcols, vcols)
        acc = acc + _dot(act, wd_ref[gcols, :])
    o_ref[...] = _layer_norm(alpha * x + acc, g_ref[...], b_ref[...])


def _ffn_sample(x2, state, w_up, b_up, w_conv, b_conv, w_down, g, b, alpha, ts, chunk):
    n, d = x2.shape
    d_ff = w_down.shape[0]
    nseq = n // ts
    kern = functools.partial(_ffn_sample_kernel, ts=ts, d_ff=d_ff, chunk=chunk, alpha=alpha)
    return pl.pallas_call(
        kern,
        out_shape=[jax.ShapeDtypeStruct((n, d), F32), jax.ShapeDtypeStruct((n, 2 * d_ff), F32)],
        scratch_shapes=[pltpu.VMEM((nseq, SUBLANES + ts, chunk), F32)],
        compiler_params=pltpu.CompilerParams(vmem_limit_bytes=VMEM_LIMIT),
        name="ffn_sample",
    )(x2, state, w_up, b_up, w_conv, b_conv, w_down, g, b)


def _ffn_chunk_size(d_ff):
    for c in (512, 256, 128):
        if d_ff % c == 0:
            return c
    raise ValueError("d_ff must be a multiple of 128")


def kernel(x_prompt, x_sample, cache_k, cache_v, cache_logf, state_pool, state_conv, page_table,
           w_in, b_f, w_pool, pool_scale, w_out, ln1_g, ln1_b, w_up, b_up, w_conv, b_conv, w_down,
           ln2_g, ln2_b):
    bsz, t_p, d = x_prompt.shape
    db, t_s, _ = x_sample.shape
    depth = w_in.shape[0]
    n_heads = cache_k.shape[3]
    att_w = n_heads * HEAD_DIM
    pool_w = state_pool.shape[3]
    d_ff = w_down.shape[1]
    n_past = page_table.shape[1] * PAGE_SIZE
    alpha = (2 * depth) ** 0.25
    assert HEAD_DIM * 2 == LANES and n_heads % 2 == 0 and n_heads <= SUBLANES
    assert t_s == SUBLANES and cache_k.shape[2] == PAGE_SIZE and t_p >= MAX_WINDOW
    assert w_in.shape[2] == 3 * att_w + n_heads + pool_w

    tm_proj = min(512, t_p)
    tq = min(256, t_p)
    tm_mix = min(256, t_p)
    tm_ffn = min(256, t_p)
    pages = 8 if page_table.shape[1] % 8 == 0 else 1
    chunk = _ffn_chunk_size(d_ff)

    yp, ys = x_prompt, x_sample
    outs = [[] for _ in range(10)]
    for l in range(depth):
        wi = w_in[l]
        fcols = wi[:, 3 * att_w:3 * att_w + n_heads]
        w_row = jnp.concatenate(
            [wi[:, :2 * att_w], wi[:, 3 * att_w + n_heads:], jnp.pad(fcols, ((0, 0), (0, LANES - n_heads)))],
            axis=1).astype(BF)
        w_t = jnp.concatenate(
            [wi[:, :3 * att_w].T, jnp.pad(fcols.T, ((0, BF_ROWS - n_heads), (0, 0)))],
            axis=0).astype(BF)
        bf_row = jnp.pad(b_f[l], (0, LANES - n_heads)).reshape(1, LANES)
        bf_col = b_f[l].reshape(n_heads, 1)
        wp_b = w_pool[l].astype(BF)
        ps = pool_scale[l].reshape(1, pool_w)
        wo_b = w_out[l].astype(BF)
        g1, b1 = ln1_g[l].reshape(1, d), ln1_b[l].reshape(1, d)
        wu_b = w_up[l].astype(BF)
        bu = b_up[l].reshape(1, 2 * d_ff)
        wc = w_conv[l]
        bc = b_conv[l].reshape(1, 2 * d_ff)
        wd_b = w_down[l].astype(BF)
        g2, b2 = ln2_g[l].reshape(1, d), ln2_b[l].reshape(1, d)

        qt, kb, kt, vt, vtb, u, lf, lft = _proj_prompt(yp, w_row, w_t, bf_row, bf_col, att_w, pool_w,
                                                       n_heads, tm_proj)
        att = _attn(qt, kb, vtb, lf, lft, tq)
        x1 = _mix_prompt(yp, att, u, wp_b, ps, wo_b, g1, b1, alpha, tm_mix)
        yp, tail = _ffn_prompt(x1, wu_b, bu, wc, bc, wd_b, g2, b2, alpha, tm_ffn, chunk)
        to_bthd = lambda a: jnp.transpose(a.reshape(bsz, n_heads, HEAD_DIM, t_p), (0, 3, 1, 2))
        outs[0].append(to_bthd(kt))
        outs[1].append(to_bthd(vt))
        outs[2].append(jnp.swapaxes(lft, 1, 2))
        outs[3].append(u[:, t_p - (MAX_WINDOW - 1):, :])
        outs[4].append(tail[:, SUBLANES - (CONV_WIDTH - 1):, :])

        n_s = db * t_s
        qs, ks, vs, kbs, vbs, us, lfs = _proj_sample(ys.reshape(n_s, d), w_row, w_t, bf_row, att_w, pool_w)
        lfs3 = lfs.reshape(db, t_s, LANES)
        lfts = jnp.pad(jnp.swapaxes(lfs3[:, :, :n_heads], 1, 2), ((0, 0), (0, 0), (0, LANES - t_s)))
        padk = lambda a: jnp.pad(a.reshape(db, t_s, att_w), ((0, 0), (0, PAGE_SIZE - t_s), (0, 0)))
        ckt = jnp.transpose(cache_k[l], (0, 2, 3, 1))
        cvt = jnp.transpose(cache_v[l], (0, 2, 3, 1))
        clft = jnp.swapaxes(cache_logf[l], 1, 2)
        att_s = _paged_attn(page_table, qs.reshape(db, t_s, att_w), padk(kbs), padk(vbs), lfs3, lfts,
                            ckt, cvt, clft, pages)
        us3 = us.reshape(db, t_s, pool_w)
        full = jnp.concatenate([jnp.zeros((db, 1, pool_w), F32), state_pool[l], us3], axis=1)
        x1s = _mix_sample(ys.reshape(n_s, d), att_s.reshape(n_s, att_w), full, wp_b, ps, wo_b, g1, b1,
                          alpha, t_s, n_past)
        ys2, hs = _ffn_sample(x1s, state_conv[l], wu_b, bu, wc, bc, wd_b, g2, b2, alpha, t_s, chunk)
        ys = ys2.reshape(db, t_s, d)
        outs[5].append(ks.reshape(db, t_s, n_heads, HEAD_DIM))
        outs[6].append(vs.reshape(db, t_s, n_heads, HEAD_DIM))
        outs[7].append(lfs3[:, :, :n_heads])
        outs[8].append(full[:, t_s + 1:, :])
        outs[9].append(hs.reshape(db, t_s, 2 * d_ff)[:, t_s - (CONV_WIDTH - 1):, :])

    st = [jnp.stack(o, 0) for o in outs]
    return (yp, ys, st[0], st[1], st[2], st[3], st[4], st[5], st[6], st[7], st[8], st[9])
```

```python
import functools

import jax
import jax.numpy as jnp
import numpy as np
from jax import lax
from jax.experimental import pallas as pl
from jax.experimental.pallas import tpu as pltpu

BF = jnp.bfloat16
F32 = jnp.float32

HEAD_DIM = 64
PAGE_SIZE = 128
POOL_WINDOWS = (2, 4, 8, 16)
MAX_WINDOW = max(POOL_WINDOWS)
CONV_WIDTH = 3
LN_EPS = 1e-5
NEG_INF = -1e30
GELU_C1 = float(np.float32(np.sqrt(2.0 / np.pi)))
GELU_C0 = float(np.float32(0.044715))
LANES = 128
SUBLANES = 8
BF_ROWS = 16
VMEM_LIMIT = 56 * 1024 * 1024


def _cparams(sem):
    return pltpu.CompilerParams(dimension_semantics=sem, vmem_limit_bytes=VMEM_LIMIT)


def _const_spec(shape):
    nd = len(shape)
    return pl.BlockSpec(shape, lambda *_: (0,) * nd, pipeline_mode=pl.Buffered(1))


def _dot(a, b):
    return jnp.dot(a, b, preferred_element_type=F32)


def _dot_nt(a, b):
    return lax.dot_general(a, b, (((1,), (1,)), ((), ())), preferred_element_type=F32)


def _layer_norm(r, g, b):
    mu = jnp.mean(r, axis=-1, keepdims=True)
    d = r - mu
    var = jnp.mean(d * d, axis=-1, keepdims=True)
    return d * lax.rsqrt(var + LN_EPS) * g + b


def _scan_incl(x, axis, reverse=False):
    n = x.shape[axis]
    idx = lax.broadcasted_iota(jnp.int32, x.shape, axis)
    s = 1
    while s < n:
        if reverse:
            x = x + jnp.where(idx + s < n, pltpu.roll(x, n - s, axis), 0.0)
        else:
            x = x + jnp.where(idx >= s, pltpu.roll(x, s, axis), 0.0)
        s *= 2
    return x


def _proj_prompt_kernel(x_ref, wr_ref, wt_ref, bfr_ref, bfc_ref,
                        qt_ref, kb_ref, kt_ref, vt_ref, vtb_ref, u_ref, lf_ref, lft_ref,
                        *, att_w, pool_w, n_heads, scale):
    xb = x_ref[0].astype(BF)
    kb_ref[0] = _dot(xb, wr_ref[:, att_w:2 * att_w]).astype(BF)
    u_ref[0] = _dot(xb, wr_ref[:, 2 * att_w:2 * att_w + pool_w])
    z = _dot(xb, wr_ref[:, 2 * att_w + pool_w:2 * att_w + pool_w + LANES]) + bfr_ref[...]
    lf_ref[0] = jax.nn.log_sigmoid(z)
    qt_ref[0] = (_dot_nt(wt_ref[0:att_w, :], xb) * scale).astype(BF)
    kt_ref[0] = _dot_nt(wt_ref[att_w:2 * att_w, :], xb)
    vt = _dot_nt(wt_ref[2 * att_w:3 * att_w, :], xb)
    vt_ref[0] = vt
    vtb_ref[0] = vt.astype(BF)
    zt = _dot_nt(wt_ref[3 * att_w:3 * att_w + BF_ROWS, :], xb)[0:n_heads] + bfc_ref[...]
    lft_ref[0] = jax.nn.log_sigmoid(zt)


def _proj_prompt(x, w_row, w_t, bf_row, bf_col, att_w, pool_w, n_heads, tm):
    bsz, t, d = x.shape
    kern = functools.partial(_proj_prompt_kernel, att_w=att_w, pool_w=pool_w, n_heads=n_heads,
                             scale=HEAD_DIM ** -0.5)
    row = lambda w: pl.BlockSpec((1, tm, w), lambda i, j: (i, j, 0))
    col = lambda h: pl.BlockSpec((1, h, tm), lambda i, j: (i, 0, j))
    return pl.pallas_call(
        kern,
        grid=(bsz, t // tm),
        in_specs=[row(d), _const_spec(w_row.shape), _const_spec(w_t.shape),
                  _const_spec((1, LANES)), _const_spec((n_heads, 1))],
        out_specs=[col(att_w), row(att_w), col(att_w), col(att_w), col(att_w), row(pool_w),
                   row(LANES), col(n_heads)],
        out_shape=[jax.ShapeDtypeStruct((bsz, att_w, t), BF),
                   jax.ShapeDtypeStruct((bsz, t, att_w), BF),
                   jax.ShapeDtypeStruct((bsz, att_w, t), F32),
                   jax.ShapeDtypeStruct((bsz, att_w, t), F32),
                   jax.ShapeDtypeStruct((bsz, att_w, t), BF),
                   jax.ShapeDtypeStruct((bsz, t, pool_w), F32),
                   jax.ShapeDtypeStruct((bsz, t, LANES), F32),
                   jax.ShapeDtypeStruct((bsz, n_heads, t), F32)],
        compiler_params=_cparams(("parallel", "parallel")),
        name="proj_prompt",
    )(x, w_row, w_t, bf_row, bf_col)


def _proj_sample_kernel(x_ref, wr_ref, wt_ref, bfr_ref, q_ref, k_ref, v_ref, kb_ref, vb_ref, u_ref, lf_ref,
                        *, att_w, pool_w, scale):
    xb = x_ref[...].astype(BF)
    q_ref[...] = (_dot(xb, wr_ref[:, 0:att_w]) * scale).astype(BF)
    u_ref[...] = _dot(xb, wr_ref[:, 2 * att_w:2 * att_w + pool_w])
    z = _dot(xb, wr_ref[:, 2 * att_w + pool_w:2 * att_w + pool_w + LANES]) + bfr_ref[...]
    lf_ref[...] = jax.nn.log_sigmoid(z)
    k = _dot(xb, wr_ref[:, att_w:2 * att_w])
    k_ref[...] = k
    kb_ref[...] = k.astype(BF)
    v = _dot_nt(xb, wt_ref[2 * att_w:3 * att_w, :])
    v_ref[...] = v
    vb_ref[...] = v.astype(BF)


def _proj_sample(x2, w_row, w_t, bf_row, att_w, pool_w):
    n, d = x2.shape
    kern = functools.partial(_proj_sample_kernel, att_w=att_w, pool_w=pool_w, scale=HEAD_DIM ** -0.5)
    return pl.pallas_call(
        kern,
        out_shape=[jax.ShapeDtypeStruct((n, att_w), BF),
                   jax.ShapeDtypeStruct((n, att_w), F32),
                   jax.ShapeDtypeStruct((n, att_w), F32),
                   jax.ShapeDtypeStruct((n, att_w), BF),
                   jax.ShapeDtypeStruct((n, att_w), BF),
                   jax.ShapeDtypeStruct((n, pool_w), F32),
                   jax.ShapeDtypeStruct((n, LANES), F32)],
        compiler_params=pltpu.CompilerParams(vmem_limit_bytes=VMEM_LIMIT),
        name="proj_sample",
    )(x2, w_row, w_t, bf_row)


KEY_CHUNK = 32


def _group_reduce(fn, x):
    parts = [x[i * SUBLANES:(i + 1) * SUBLANES] for i in range(x.shape[0] // SUBLANES)]
    return functools.reduce(fn, parts)


def _attn_kernel(qt_ref, k_ref, vt_ref, lf_ref, lft_ref, o_ref, c_ref, ct_ref, qh_ref, s_ref, p_ref,
                 *, tq, n_heads):
    qi = pl.program_id(1)
    tk = tq
    rc = KEY_CHUNK

    @pl.when(qi == 0)
    def _():
        c_ref[...] = _scan_incl(lf_ref[0], 0)
        ct_ref[...] = _scan_incl(lft_ref[0], 1)

    q0 = pl.multiple_of(qi * tq, tq)
    first = lax.broadcasted_iota(jnp.int32, (LANES, 1), 0) < HEAD_DIM
    key_i = lax.broadcasted_iota(jnp.int32, (rc, tq), 0)
    qry_i = lax.broadcasted_iota(jnp.int32, (rc, tq), 1)

    heads = range(n_heads)
    for pair in range(n_heads // 2):
        qtp = qt_ref[0, pair * LANES:(pair + 1) * LANES, :]
        zero = jnp.zeros_like(qtp)
        qh_ref[2 * pair] = jnp.where(first, qtp, zero)
        qh_ref[2 * pair + 1] = jnp.where(first, zero, qtp)
    ci = [ct_ref[h:h + 1, pl.ds(q0, tq)] for h in heads]

    def step(j, carry, masked):
        k0 = pl.multiple_of(j * tk, tk)
        for h in heads:
            kb = k_ref[0, pl.ds(k0, tk), (h // 2) * LANES:(h // 2 + 1) * LANES]
            s_ref[h] = _dot(kb, qh_ref[h])
        m_new, alpha = [], []
        for h in heads:
            mb = jnp.full((SUBLANES, tq), NEG_INF, F32)
            for r in range(tk // rc):
                rs = slice(r * rc, (r + 1) * rc)
                cj = c_ref[pl.ds(pl.multiple_of(k0 + r * rc, rc), rc), h:h + 1]
                s = s_ref[h, rs, :] + ci[h] - cj
                if masked:
                    s = jnp.where(key_i + r * rc <= qry_i, s, NEG_INF)
                s_ref[h, rs, :] = s
                mb = jnp.maximum(mb, _group_reduce(jnp.maximum, s))
            m_new.append(jnp.maximum(carry[3 * h], jnp.max(mb, axis=0, keepdims=True)))
            alpha.append(jnp.exp(carry[3 * h] - m_new[h]))
        l_new = []
        for h in heads:
            lb = jnp.zeros((SUBLANES, tq), F32)
            for r in range(tk // rc):
                rs = slice(r * rc, (r + 1) * rc)
                p = jnp.exp(s_ref[h, rs, :] - m_new[h])
                lb = lb + _group_reduce(jnp.add, p)
                p_ref[h, rs, :] = p.astype(BF)
            l_new.append(alpha[h] * carry[3 * h + 1] + jnp.sum(lb, axis=0, keepdims=True))
        out = []
        for h in heads:
            vh = vt_ref[0, h * HEAD_DIM:(h + 1) * HEAD_DIM, pl.ds(k0, tk)]
            out += [m_new[h], l_new[h], alpha[h] * carry[3 * h + 2] + _dot(vh, p_ref[h])]
        return tuple(out)

    init = (jnp.full((1, tq), NEG_INF, F32), jnp.zeros((1, tq), F32),
            jnp.zeros((HEAD_DIM, tq), F32)) * n_heads
    carry = lax.fori_loop(0, qi, functools.partial(step, masked=False), init)
    fin = step(qi, carry, True)
    for pair in range(n_heads // 2):
        h0, h1 = 2 * pair, 2 * pair + 1
        out_t = jnp.concatenate([fin[3 * h0 + 2] / fin[3 * h0 + 1],
                                 fin[3 * h1 + 2] / fin[3 * h1 + 1]], axis=0)
        o_ref[0, :, pair * LANES:(pair + 1) * LANES] = out_t.T.astype(o_ref.dtype)


def _attn(qt, kb, vtb, lf, lft, tq):
    b, w, t = qt.shape
    n_heads = w // HEAD_DIM
    kern = functools.partial(_attn_kernel, tq=tq, n_heads=n_heads)
    per_b = lambda s1, s2: pl.BlockSpec((1, s1, s2), lambda i, j: (i, 0, 0))
    return pl.pallas_call(
        kern,
        grid=(b, t // tq),
        in_specs=[pl.BlockSpec((1, w, tq), lambda i, j: (i, 0, j)),
                  per_b(t, w), per_b(w, t), per_b(t, LANES), per_b(n_heads, t)],
        out_specs=pl.BlockSpec((1, tq, w), lambda i, j: (i, j, 0)),
        out_shape=jax.ShapeDtypeStruct((b, t, w), BF),
        scratch_shapes=[pltpu.VMEM((t, LANES), F32), pltpu.VMEM((n_heads, t), F32),
                        pltpu.VMEM((n_heads, LANES, tq), BF),
                        pltpu.VMEM((n_heads, tq, tq), F32), pltpu.VMEM((n_heads, tq, tq), BF)],
        compiler_params=_cparams(("parallel", "arbitrary")),
        name="attn_prompt",
    )(qt, kb, vtb, lf, lft)


def _paged_kernel(pt_ref, q_ref, kn_ref, vn_ref, lf_ref, lft_ref, tri_ref, *rest, n_heads, pages, tq):
    k_refs = rest[0:pages]
    v_refs = rest[pages:2 * pages]
    lp_refs = rest[2 * pages:3 * pages]
    o_ref = rest[3 * pages]
    qbd_ref, ci_ref, m_ref, l_ref, acc_ref, carry_ref = rest[3 * pages + 1:]
    g = pl.program_id(1)
    rows = n_heads * tq
    w = q_ref.shape[2]

    def expand(x):
        return jnp.concatenate(
            [jnp.broadcast_to(x[h:h + 1, :], (tq, x.shape[1])) for h in range(n_heads)], axis=0)

    def update(s, pv_fn):
        m = m_ref[...]
        m_new = jnp.maximum(m, jnp.max(s, axis=-1, keepdims=True))
        alpha = jnp.exp(m - m_new)
        p = jnp.exp(s - m_new)
        l_ref[...] = alpha * l_ref[...] + jnp.sum(p, axis=-1, keepdims=True)
        acc_ref[...] = alpha * acc_ref[...] + pv_fn(p.astype(BF))
        m_ref[...] = m_new

    @pl.when(g == 0)
    def _():
        q = q_ref[0]
        lane = lax.broadcasted_iota(jnp.int32, (1, w), 1)
        zero = jnp.zeros_like(q)
        qbd_ref[...] = jnp.concatenate(
            [jnp.where((lane >= h * HEAD_DIM) & (lane < (h + 1) * HEAD_DIM), q, zero)
             for h in range(n_heads)], axis=0)
        c = _scan_incl(lf_ref[0], 0)
        ci_ref[...] = jnp.concatenate([c[:, h:h + 1] for h in range(n_heads)], axis=0)
        m_ref[...] = jnp.full((rows, 1), NEG_INF, F32)
        l_ref[...] = jnp.zeros((rows, 1), F32)
        acc_ref[...] = jnp.zeros((rows, w), F32)
        carry_ref[...] = jnp.zeros((n_heads, PAGE_SIZE), F32)
        cj = _scan_incl(lft_ref[0], 1)
        s = _dot_nt(qbd_ref[...], kn_ref[0]) + ci_ref[...] - expand(cj)
        r_idx = lax.broadcasted_iota(jnp.int32, (rows, LANES), 0)
        c_idx = lax.broadcasted_iota(jnp.int32, (rows, LANES), 1)
        s = jnp.where(c_idx <= jnp.bitwise_and(r_idx, tq - 1), s, NEG_INF)
        update(s, lambda p: _dot(p, vn_ref[0]))

    ktc = jnp.concatenate([r[0].reshape(w, PAGE_SIZE) for r in k_refs], axis=1).astype(BF)
    vtc = jnp.concatenate([r[0].reshape(w, PAGE_SIZE) for r in v_refs], axis=1).astype(BF)
    lf = jnp.concatenate([r[0] for r in lp_refs], axis=0)
    hi = lf.astype(BF)
    rem = lf - hi.astype(F32)
    mid = rem.astype(BF)
    lo = (rem - mid.astype(F32)).astype(BF)
    y = _dot(jnp.concatenate([hi, mid, lo], axis=0), tri_ref[...])
    ph = pages * n_heads
    y = y[0:ph] + y[ph:2 * ph] + y[2 * ph:3 * ph]
    run = carry_ref[...]
    cjs = [None] * pages
    for pg in reversed(range(pages)):
        blk = slice(pg * n_heads, (pg + 1) * n_heads)
        cjs[pg] = -(run + (y[blk, 0:PAGE_SIZE] - lf[blk]))
        run = run + y[blk, PAGE_SIZE:2 * PAGE_SIZE]
    carry_ref[...] = run
    cj = jnp.concatenate(cjs, axis=1)
    s = _dot(qbd_ref[...], ktc) + ci_ref[...] - expand(cj)
    update(s, lambda p: _dot_nt(p, vtc))

    @pl.when(g == pl.num_programs(1) - 1)
    def _():
        lane = lax.broadcasted_iota(jnp.int32, (1, w), 1)
        out = jnp.zeros((tq, w), F32)
        for h in range(n_heads):
            sl = slice(h * tq, (h + 1) * tq)
            val = acc_ref[sl, :] / l_ref[sl, :]
            out = jnp.where((lane >= h * HEAD_DIM) & (lane < (h + 1) * HEAD_DIM), val, out)
        o_ref[0] = out.astype(o_ref.dtype)


def _paged_attn(page_table, q, kn, vn, lf, lft, cache_kt, cache_vt, cache_lft, pages):
    db, tq, w = q.shape
    n_heads = w // HEAD_DIM
    n_pages = page_table.shape[1]
    ng = n_pages // pages
    rows = n_heads * tq
    assert tq & (tq - 1) == 0

    def page_spec(i, shape):
        nd = len(shape)
        return pl.BlockSpec(shape, lambda b, g, pt: (pt[b, (ng - 1 - g) * pages + i],) + (0,) * (nd - 1))

    per_b = lambda s1, s2: pl.BlockSpec((1, s1, s2), lambda b, g, pt: (b, 0, 0))
    in_specs = [per_b(tq, w), per_b(PAGE_SIZE, w), per_b(PAGE_SIZE, w), per_b(tq, LANES), per_b(n_heads, LANES),
                pl.BlockSpec((PAGE_SIZE, 2 * PAGE_SIZE), lambda b, g, pt: (0, 0))]
    in_specs += [page_spec(i, (1, n_heads, HEAD_DIM, PAGE_SIZE)) for i in range(pages)]
    in_specs += [page_spec(i, (1, n_heads, HEAD_DIM, PAGE_SIZE)) for i in range(pages)]
    in_specs += [page_spec(i, (1, n_heads, PAGE_SIZE)) for i in range(pages)]
    kern = functools.partial(_paged_kernel, n_heads=n_heads, pages=pages, tq=tq)
    grid_spec = pltpu.PrefetchScalarGridSpec(
        num_scalar_prefetch=1,
        grid=(db, ng),
        in_specs=in_specs,
        out_specs=per_b(tq, w),
        scratch_shapes=[pltpu.VMEM((rows, w), BF), pltpu.VMEM((rows, 1), F32),
                        pltpu.VMEM((rows, 1), F32), pltpu.VMEM((rows, 1), F32),
                        pltpu.VMEM((rows, w), F32), pltpu.VMEM((n_heads, PAGE_SIZE), F32)])
    pos = lax.broadcasted_iota(jnp.int32, (PAGE_SIZE, PAGE_SIZE), 0)
    col = lax.broadcasted_iota(jnp.int32, (PAGE_SIZE, PAGE_SIZE), 1)
    tri = jnp.concatenate([(pos >= col).astype(BF), jnp.ones((PAGE_SIZE, PAGE_SIZE), BF)], axis=1)
    return pl.pallas_call(
        kern,
        grid_spec=grid_spec,
        out_shape=jax.ShapeDtypeStruct((db, tq, w), BF),
        compiler_params=_cparams(("parallel", "arbitrary")),
        name="attn_paged",
    )(page_table, q, kn, vn, lf, lft, tri, *([cache_kt] * pages), *([cache_vt] * pages), *([cache_lft] * pages))


def _mix_tail(pooled, att, x, wp_ref, ps_ref, wo_ref, g_ref, b_ref, alpha):
    gd = pooled[0].shape[1]
    att_w = att.shape[1]
    mixed = [_dot(pooled[g].astype(BF), wp_ref[g]) * ps_ref[:, g * gd:(g + 1) * gd]
             for g in range(len(pooled))]
    pool = jnp.concatenate(mixed, axis=1).astype(BF)
    mix = _dot(att, wo_ref[0:att_w, :]) + _dot(pool, wo_ref[att_w:, :])
    return _layer_norm(alpha * x + mix, g_ref[...], b_ref[...])


def _mix_prompt_kernel(x_ref, att_ref, u_ref, halo_ref, wp_ref, ps_ref, wo_ref, g_ref, b_ref,
                       o_ref, ubuf_ref, *, tm, alpha):
    t = pl.program_id(1)
    hl = MAX_WINDOW
    halo = halo_ref[0]
    ubuf_ref[0:hl, :] = jnp.where(t == 0, jnp.zeros_like(halo), halo)
    ubuf_ref[hl:hl + tm, :] = u_ref[0]
    pos = t * tm + lax.broadcasted_iota(jnp.int32, (tm, 1), 0)
    gd = u_ref.shape[2] // len(POOL_WINDOWS)
    pooled = []
    for g, w in enumerate(POOL_WINDOWS):
        lanes = slice(g * gd, (g + 1) * gd)
        tok = ubuf_ref[hl:hl + tm, lanes]
        acc = tok
        for k in range(1, w):
            acc = acc + ubuf_ref[hl - k:hl - k + tm, lanes]
        cnt = jnp.minimum(pos + 1, w).astype(F32)
        pooled.append(acc / cnt - tok)
    o_ref[0] = _mix_tail(pooled, att_ref[0], x_ref[0], wp_ref, ps_ref, wo_ref, g_ref, b_ref, alpha)


def _mix_prompt(x, att, u, w_pool, pool_scale, w_out, g, b, alpha, tm):
    bsz, t, d = x.shape
    att_w, pool_w = att.shape[2], u.shape[2]
    hl = MAX_WINDOW
    ng, gd = w_pool.shape[0], w_pool.shape[1]
    kern = functools.partial(_mix_prompt_kernel, tm=tm, alpha=alpha)
    row = lambda w: pl.BlockSpec((1, tm, w), lambda i, j: (i, j, 0))
    halo = pl.BlockSpec((1, hl, pool_w), lambda i, j: (i, jnp.maximum(j * (tm // hl) - 1, 0), 0))
    return pl.pallas_call(
        kern,
        grid=(bsz, t // tm),
        in_specs=[row(d), row(att_w), row(pool_w), halo,
                  _const_spec((ng, gd, gd)), _const_spec((1, pool_w)), _const_spec((att_w + pool_w, d)),
                  _const_spec((1, d)), _const_spec((1, d))],
        out_specs=row(d),
        out_shape=jax.ShapeDtypeStruct((bsz, t, d), F32),
        scratch_shapes=[pltpu.VMEM((hl + tm, pool_w), F32)],
        compiler_params=_cparams(("parallel", "parallel")),
        name="mix_prompt",
    )(x, att, u, u, w_pool, pool_scale, w_out, g, b)


def _mix_sample_kernel(x_ref, att_ref, full_ref, wp_ref, ps_ref, wo_ref, g_ref, b_ref, o_ref,
                       *, ts, n_past, alpha):
    nseq = full_ref.shape[0]
    hl = MAX_WINDOW
    gd = full_ref.shape[2] // len(POOL_WINDOWS)
    pos = n_past + lax.broadcasted_iota(jnp.int32, (1, ts, 1), 1)
    pooled = []
    for g, w in enumerate(POOL_WINDOWS):
        lanes = slice(g * gd, (g + 1) * gd)
        tok = full_ref[:, hl:hl + ts, lanes]
        acc = tok
        for k in range(1, w):
            acc = acc + full_ref[:, hl - k:hl - k + ts, lanes]
        cnt = jnp.minimum(pos + 1, w).astype(F32)
        pooled.append((acc / cnt - tok).reshape(nseq * ts, gd))
    o_ref[...] = _mix_tail(pooled, att_ref[...], x_ref[...], wp_ref, ps_ref, wo_ref, g_ref, b_ref, alpha)


def _mix_sample(x2, att2, full, w_pool, pool_scale, w_out, g, b, alpha, ts, n_past):
    n, d = x2.shape
    kern = functools.partial(_mix_sample_kernel, ts=ts, n_past=n_past, alpha=alpha)
    return pl.pallas_call(
        kern,
        out_shape=jax.ShapeDtypeStruct((n, d), F32),
        compiler_params=pltpu.CompilerParams(vmem_limit_bytes=VMEM_LIMIT),
        name="mix_sample",
    )(x2, att2, full, w_pool, pool_scale, w_out, g, b)


def _shift_rows(h, s):
    n8, c = h.shape[0] // SUBLANES, h.shape[1]
    r = pltpu.roll(h.reshape(n8, SUBLANES, c), s, 1)
    sub = lax.broadcasted_iota(jnp.int32, (1, SUBLANES, c), 1)
    return jnp.where(sub < s, r[:-1], r[1:]).reshape((n8 - 1) * SUBLANES, c)


def _ffn_chunk(hg3, hv3, wc_ref, bc_ref, gcols, vcols):
    def conv(h3, cols, scale):
        hc = bc_ref[:, cols] * scale + h3[0] * (wc_ref[0:1, cols] * scale)
        for j in range(1, CONV_WIDTH):
            hc = hc + h3[j] * (wc_ref[j:j + 1, cols] * scale)
        return hc
    g = conv(hg3, gcols, 1.0)
    vh = conv(hv3, vcols, 0.5)
    t = jnp.tanh(g * (GELU_C1 + (GELU_C1 * GELU_C0) * (g * g)))
    return ((g * (1.0 + t)) * vh).astype(BF)


def _ffn_prompt_kernel(x_ref, halo_ref, wu_ref, bu_ref, wc_ref, bc_ref, wd_ref, g_ref, b_ref,
                       o_ref, tail_ref, *, tm, d_ff, chunk, alpha):
    t = pl.program_id(1)
    hl = SUBLANES
    x = x_ref[0]
    xin = jnp.concatenate([halo_ref[0], x], axis=0).astype(BF)
    started = (t > 0).astype(F32)
    n_chunks = d_ff // chunk
    cols_of = lambda c: (slice(c * chunk, (c + 1) * chunk), slice(d_ff + c * chunk, d_ff + (c + 1) * chunk))

    def up(c):
        return [_dot(xin, wu_ref[:, cols]) for cols in cols_of(c)]

    def shifted(hd, cols):
        body = hd[hl:] + bu_ref[:, cols]
        tail_ref[0, :, cols] = body[tm - hl:tm]
        h = jnp.concatenate([(hd[0:hl] + bu_ref[:, cols]) * started, body], axis=0)
        return [_shift_rows(h, CONV_WIDTH - 1 - j) for j in range(CONV_WIDTH - 1)] + [body]

    acc = jnp.zeros((tm, x.shape[1]), F32)
    nxt = up(0)
    for c in range(n_chunks):
        cur = nxt
        if c + 1 < n_chunks:
            nxt = up(c + 1)
        gcols, vcols = cols_of(c)
        act = _ffn_chunk(shifted(cur[0], gcols), shifted(cur[1], vcols), wc_ref, bc_ref, gcols, vcols)
        acc = acc + _dot(act, wd_ref[gcols, :])
    o_ref[0] = _layer_norm(alpha * x + acc, g_ref[...], b_ref[...])


def _ffn_prompt(x1, w_up, b_up, w_conv, b_conv, w_down, g, b, alpha, tm, chunk):
    bsz, t, d = x1.shape
    d_ff = w_down.shape[0]
    hl = SUBLANES
    kern = functools.partial(_ffn_prompt_kernel, tm=tm, d_ff=d_ff, chunk=chunk, alpha=alpha)
    row = pl.BlockSpec((1, tm, d), lambda i, j: (i, j, 0))
    halo = pl.BlockSpec((1, hl, d), lambda i, j: (i, jnp.maximum(j * (tm // hl) - 1, 0), 0))
    return pl.pallas_call(
        kern,
        grid=(bsz, t // tm),
        in_specs=[row, halo, _const_spec((d, 2 * d_ff)), _const_spec((1, 2 * d_ff)),
                  _const_spec((CONV_WIDTH, 2 * d_ff)), _const_spec((1, 2 * d_ff)),
                  _const_spec((d_ff, d)), _const_spec((1, d)), _const_spec((1, d))],
        out_specs=[row, pl.BlockSpec((1, hl, 2 * d_ff), lambda i, j: (i, 0, 0))],
        out_shape=[jax.ShapeDtypeStruct((bsz, t, d), F32),
                   jax.ShapeDtypeStruct((bsz, hl, 2 * d_ff), F32)],
        compiler_params=_cparams(("parallel", "arbitrary")),
        name="ffn_prompt",
    )(x1, x1, w_up, b_up, w_conv, b_conv, w_down, g, b)


def _ffn_sample_kernel(x_ref, st_ref, wu_ref, bu_ref, wc_ref, bc_ref, wd_ref, g_ref, b_ref,
                       o_ref, h_ref, buf_ref, *, ts, d_ff, chunk, alpha):
    nseq = st_ref.shape[0]
    n = nseq * ts
    hl = SUBLANES
    nst = CONV_WIDTH - 1
    x = x_ref[...]
    xb = x.astype(BF)
    acc = jnp.zeros((n, x.shape[1]), F32)
    for c in range(d_ff // chunk):
        gcols = slice(c * chunk, (c + 1) * chunk)
        vcols = slice(d_ff + c * chunk, d_ff + (c + 1) * chunk)
        h3 = []
        for cols in (gcols, vcols):
            h = _dot(xb, wu_ref[:, cols]) + bu_ref[:, cols]
            h_ref[:, cols] = h
            buf_ref[:, hl - nst:hl, :] = st_ref[:, :, cols]
            buf_ref[:, hl:hl + ts, :] = h.reshape(nseq, ts, chunk)
            h3.append([buf_ref[:, hl - 2 + j:hl - 2 + j + ts, :].reshape(n, chunk)
                       for j in range(CONV_WIDTH)])
        act = _ffn_chunk(h3[0], h3[1], wc_ref, bc_ref, gcols, vcols)
        acc = acc + _dot(act, wd_ref[gcols, :])
    o_ref[...] = _layer_norm(alpha * x + acc, g_ref[...], b_ref[...])


def _ffn_sample(x2, state, w_up, b_up, w_conv, b_conv, w_down, g, b, alpha, ts, chunk):
    n, d = x2.shape
    d_ff = w_down.shape[0]
    nseq = n // ts
    kern = functools.partial(_ffn_sample_kernel, ts=ts, d_ff=d_ff, chunk=chunk, alpha=alpha)
    return pl.pallas_call(
        kern,
        out_shape=[jax.ShapeDtypeStruct((n, d), F32), jax.ShapeDtypeStruct((n, 2 * d_ff), F32)],
        scratch_shapes=[pltpu.VMEM((nseq, SUBLANES + ts, chunk), F32)],
        compiler_params=pltpu.CompilerParams(vmem_limit_bytes=VMEM_LIMIT),
        name="ffn_sample",
    )(x2, state, w_up, b_up, w_conv, b_conv, w_down, g, b)


def _ffn_chunk_size(d_ff):
    for c in (512, 256, 128):
        if d_ff % c == 0:
            return c
    raise ValueError("d_ff must be a multiple of 128")


def kernel(x_prompt, x_sample, cache_k, cache_v, cache_logf, state_pool, state_conv, page_table,
           w_in, b_f, w_pool, pool_scale, w_out, ln1_g, ln1_b, w_up, b_up, w_conv, b_conv, w_down,
           ln2_g, ln2_b):
    bsz, t_p, d = x_prompt.shape
    db, t_s, _ = x_sample.shape
    depth = w_in.shape[0]
    n_heads = cache_k.shape[3]
    att_w = n_heads * HEAD_DIM
    pool_w = state_pool.shape[3]
    d_ff = w_down.shape[1]
    n_past = page_table.shape[1] * PAGE_SIZE
    alpha = (2 * depth) ** 0.25
    assert HEAD_DIM * 2 == LANES and n_heads % 2 == 0 and n_heads <= SUBLANES
    assert t_s == SUBLANES and cache_k.shape[2] == PAGE_SIZE and t_p >= MAX_WINDOW
    assert w_in.shape[2] == 3 * att_w + n_heads + pool_w

    tm_proj = min(512, t_p)
    tq = min(256, t_p)
    tm_mix = min(256, t_p)
    tm_ffn = min(256, t_p)
    pages = 8 if page_table.shape[1] % 8 == 0 else 1
    chunk = _ffn_chunk_size(d_ff)

    yp, ys = x_prompt, x_sample
    outs = [[] for _ in range(10)]
    for l in range(depth):
        wi = w_in[l]
        fcols = wi[:, 3 * att_w:3 * att_w + n_heads]
        w_row = jnp.concatenate(
            [wi[:, :2 * att_w], wi[:, 3 * att_w + n_heads:], jnp.pad(fcols, ((0, 0), (0, LANES - n_heads)))],
            axis=1).astype(BF)
        w_t = jnp.concatenate(
            [wi[:, :3 * att_w].T, jnp.pad(fcols.T, ((0, BF_ROWS - n_heads), (0, 0)))],
            axis=0).astype(BF)
        bf_row = jnp.pad(b_f[l], (0, LANES - n_heads)).reshape(1, LANES)
        bf_col = b_f[l].reshape(n_heads, 1)
        wp_b = w_pool[l].astype(BF)
        ps = pool_scale[l].reshape(1, pool_w)
        wo_b = w_out[l].astype(BF)
        g1, b1 = ln1_g[l].reshape(1, d), ln1_b[l].reshape(1, d)
        wu_b = w_up[l].astype(BF)
        bu = b_up[l].reshape(1, 2 * d_ff)
        wc = w_conv[l]
        bc = b_conv[l].reshape(1, 2 * d_ff)
        wd_b = w_down[l].astype(BF)
        g2, b2 = ln2_g[l].reshape(1, d), ln2_b[l].reshape(1, d)

        qt, kb, kt, vt, vtb, u, lf, lft = _proj_prompt(yp, w_row, w_t, bf_row, bf_col, att_w, pool_w,
                                                       n_heads, tm_proj)
        att = _attn(qt, kb, vtb, lf, lft, tq)
        x1 = _mix_prompt(yp, att, u, wp_b, ps, wo_b, g1, b1, alpha, tm_mix)
        yp, tail = _ffn_prompt(x1, wu_b, bu, wc, bc, wd_b, g2, b2, alpha, tm_ffn, chunk)
        to_bthd = lambda a: jnp.transpose(a.reshape(bsz, n_heads, HEAD_DIM, t_p), (0, 3, 1, 2))
        outs[0].append(to_bthd(kt))
        outs[1].append(to_bthd(vt))
        outs[2].append(jnp.swapaxes(lft, 1, 2))
        outs[3].append(u[:, t_p - (MAX_WINDOW - 1):, :])
        outs[4].append(tail[:, SUBLANES - (CONV_WIDTH - 1):, :])

        n_s = db * t_s
        qs, ks, vs, kbs, vbs, us, lfs = _proj_sample(ys.reshape(n_s, d), w_row, w_t, bf_row, att_w, pool_w)
        lfs3 = lfs.reshape(db, t_s, LANES)
        lfts = jnp.pad(jnp.swapaxes(lfs3[:, :, :n_heads], 1, 2), ((0, 0), (0, 0), (0, LANES - t_s)))
        padk = lambda a: jnp.pad(a.reshape(db, t_s, att_w), ((0, 0), (0, PAGE_SIZE - t_s), (0, 0)))
        ckt = jnp.transpose(cache_k[l], (0, 2, 3, 1))
        cvt = jnp.transpose(cache_v[l], (0, 2, 3, 1))
        clft = jnp.swapaxes(cache_logf[l], 1, 2)
        att_s = _paged_attn(page_table, qs.reshape(db, t_s, att_w), padk(kbs), padk(vbs), lfs3, lfts,
                            ckt, cvt, clft, pages)
        us3 = us.reshape(db, t_s, pool_w)
        full = jnp.concatenate([jnp.zeros((db, 1, pool_w), F32), state_pool[l], us3], axis=1)
        x1s = _mix_sample(ys.reshape(n_s, d), att_s.reshape(n_s, att_w), full, wp_b, ps, wo_b, g1, b1,
                          alpha, t_s, n_past)
        ys2, hs = _ffn_sample(x1s, state_conv[l], wu_b, bu, wc, bc, wd_b, g2, b2, alpha, t_s, chunk)
        ys = ys2.reshape(db, t_s, d)
        outs[5].append(ks.reshape(db, t_s, n_heads, HEAD_DIM))
        outs[6].append(vs.reshape(db, t_s, n_heads, HEAD_DIM))
        outs[7].append(lfs3[:, :, :n_heads])
        outs[8].append(full[:, t_s + 1:, :])
        outs[9].append(hs.reshape(db, t_s, 2 * d_ff)[:, t_s - (CONV_WIDTH - 1):, :])

    st = [jnp.stack(o, 0) for o in outs]
    return (yp, ys, st[0], st[1], st[2], st[3], st[4], st[5], st[6], st[7], st[8], st[9])
```

```python
import functools

import jax
import jax.numpy as jnp
import numpy as np
from jax import lax
from jax.experimental import pallas as pl
from jax.experimental.pallas import tpu as pltpu

BF = jnp.bfloat16
F32 = jnp.float32

HEAD_DIM = 64
PAGE_SIZE = 128
POOL_WINDOWS = (2, 4, 8, 16)
MAX_WINDOW = max(POOL_WINDOWS)
CONV_WIDTH = 3
LN_EPS = 1e-5
NEG_INF = -1e30
GELU_C1 = float(np.float32(np.sqrt(2.0 / np.pi)))
GELU_C0 = float(np.float32(0.044715))
LANES = 128
SUBLANES = 8
BF_ROWS = 16
VMEM_LIMIT = 56 * 1024 * 1024


def _cparams(sem):
    return pltpu.CompilerParams(dimension_semantics=sem, vmem_limit_bytes=VMEM_LIMIT)


def _const_spec(shape):
    nd = len(shape)
    return pl.BlockSpec(shape, lambda *_: (0,) * nd, pipeline_mode=pl.Buffered(1))


def _dot(a, b):
    return jnp.dot(a, b, preferred_element_type=F32)


def _dot_nt(a, b):
    return lax.dot_general(a, b, (((1,), (1,)), ((), ())), preferred_element_type=F32)


def _layer_norm(r, g, b):
    mu = jnp.mean(r, axis=-1, keepdims=True)
    d = r - mu
    var = jnp.mean(d * d, axis=-1, keepdims=True)
    return d * lax.rsqrt(var + LN_EPS) * g + b


def _scan_incl(x, axis, reverse=False):
    n = x.shape[axis]
    idx = lax.broadcasted_iota(jnp.int32, x.shape, axis)
    s = 1
    while s < n:
        if reverse:
            x = x + jnp.where(idx + s < n, pltpu.roll(x, n - s, axis), 0.0)
        else:
            x = x + jnp.where(idx >= s, pltpu.roll(x, s, axis), 0.0)
        s *= 2
    return x


def _proj_prompt_kernel(x_ref, wr_ref, wt_ref, bfr_ref, bfc_ref,
                        qt_ref, kb_ref, kt_ref, vt_ref, vtb_ref, u_ref, lf_ref, lft_ref,
                        *, att_w, pool_w, n_heads, scale):
    xb = x_ref[0].astype(BF)
    kb_ref[0] = _dot(xb, wr_ref[:, att_w:2 * att_w]).astype(BF)
    u_ref[0] = _dot(xb, wr_ref[:, 2 * att_w:2 * att_w + pool_w])
    z = _dot(xb, wr_ref[:, 2 * att_w + pool_w:2 * att_w + pool_w + LANES]) + bfr_ref[...]
    lf_ref[0] = jax.nn.log_sigmoid(z)
    qt_ref[0] = (_dot_nt(wt_ref[0:att_w, :], xb) * scale).astype(BF)
    kt_ref[0] = _dot_nt(wt_ref[att_w:2 * att_w, :], xb)
    vt = _dot_nt(wt_ref[2 * att_w:3 * att_w, :], xb)
    vt_ref[0] = vt
    vtb_ref[0] = vt.astype(BF)
    zt = _dot_nt(wt_ref[3 * att_w:3 * att_w + BF_ROWS, :], xb)[0:n_heads] + bfc_ref[...]
    lft_ref[0] = jax.nn.log_sigmoid(zt)


def _proj_prompt(x, w_row, w_t, bf_row, bf_col, att_w, pool_w, n_heads, tm):
    bsz, t, d = x.shape
    kern = functools.partial(_proj_prompt_kernel, att_w=att_w, pool_w=pool_w, n_heads=n_heads,
                             scale=HEAD_DIM ** -0.5)
    row = lambda w: pl.BlockSpec((1, tm, w), lambda i, j: (i, j, 0))
    col = lambda h: pl.BlockSpec((1, h, tm), lambda i, j: (i, 0, j))
    return pl.pallas_call(
        kern,
        grid=(bsz, t // tm),
        in_specs=[row(d), _const_spec(w_row.shape), _const_spec(w_t.shape),
                  _const_spec((1, LANES)), _const_spec((n_heads, 1))],
        out_specs=[col(att_w), row(att_w), col(att_w), col(att_w), col(att_w), row(pool_w),
                   row(LANES), col(n_heads)],
        out_shape=[jax.ShapeDtypeStruct((bsz, att_w, t), BF),
                   jax.ShapeDtypeStruct((bsz, t, att_w), BF),
                   jax.ShapeDtypeStruct((bsz, att_w, t), F32),
                   jax.ShapeDtypeStruct((bsz, att_w, t), F32),
                   jax.ShapeDtypeStruct((bsz, att_w, t), BF),
                   jax.ShapeDtypeStruct((bsz, t, pool_w), F32),
                   jax.ShapeDtypeStruct((bsz, t, LANES), F32),
                   jax.ShapeDtypeStruct((bsz, n_heads, t), F32)],
        compiler_params=_cparams(("parallel", "parallel")),
        name="proj_prompt",
    )(x, w_row, w_t, bf_row, bf_col)


def _proj_sample_kernel(x_ref, wr_ref, wt_ref, bfr_ref, q_ref, k_ref, v_ref, kb_ref, vb_ref, u_ref, lf_ref,
                        *, att_w, pool_w, scale):
    xb = x_ref[...].astype(BF)
    q_ref[...] = (_dot(xb, wr_ref[:, 0:att_w]) * scale).astype(BF)
    u_ref[...] = _dot(xb, wr_ref[:, 2 * att_w:2 * att_w + pool_w])
    z = _dot(xb, wr_ref[:, 2 * att_w + pool_w:2 * att_w + pool_w + LANES]) + bfr_ref[...]
    lf_ref[...] = jax.nn.log_sigmoid(z)
    k = _dot(xb, wr_ref[:, att_w:2 * att_w])
    k_ref[...] = k
    kb_ref[...] = k.astype(BF)
    v = _dot_nt(xb, wt_ref[2 * att_w:3 * att_w, :])
    v_ref[...] = v
    vb_ref[...] = v.astype(BF)


def _proj_sample(x2, w_row, w_t, bf_row, att_w, pool_w):
    n, d = x2.shape
    kern = functools.partial(_proj_sample_kernel, att_w=att_w, pool_w=pool_w, scale=HEAD_DIM ** -0.5)
    return pl.pallas_call(
        kern,
        out_shape=[jax.ShapeDtypeStruct((n, att_w), BF),
                   jax.ShapeDtypeStruct((n, att_w), F32),
                   jax.ShapeDtypeStruct((n, att_w), F32),
                   jax.ShapeDtypeStruct((n, att_w), BF),
                   jax.ShapeDtypeStruct((n, att_w), BF),
                   jax.ShapeDtypeStruct((n, pool_w), F32),
                   jax.ShapeDtypeStruct((n, LANES), F32)],
        compiler_params=pltpu.CompilerParams(vmem_limit_bytes=VMEM_LIMIT),
        name="proj_sample",
    )(x2, w_row, w_t, bf_row)


KEY_CHUNK = 32


def _group_reduce(fn, x):
    parts = [x[i * SUBLANES:(i + 1) * SUBLANES] for i in range(x.shape[0] // SUBLANES)]
    return functools.reduce(fn, parts)


def _attn_kernel(qt_ref, k_ref, vt_ref, lf_ref, lft_ref, o_ref, c_ref, ct_ref, qh_ref, s_ref, p_ref,
                 *, tq, n_heads):
    qi = pl.program_id(1)
    tk = tq
    rc = KEY_CHUNK

    @pl.when(qi == 0)
    def _():
        c_ref[...] = _scan_incl(lf_ref[0], 0)
        ct_ref[...] = _scan_incl(lft_ref[0], 1)

    q0 = pl.multiple_of(qi * tq, tq)
    first = lax.broadcasted_iota(jnp.int32, (LANES, 1), 0) < HEAD_DIM
    key_i = lax.broadcasted_iota(jnp.int32, (rc, tq), 0)
    qry_i = lax.broadcasted_iota(jnp.int32, (rc, tq), 1)

    heads = range(n_heads)
    for pair in range(n_heads // 2):
        qtp = qt_ref[0, pair * LANES:(pair + 1) * LANES, :]
        zero = jnp.zeros_like(qtp)
        qh_ref[2 * pair] = jnp.where(first, qtp, zero)
        qh_ref[2 * pair + 1] = jnp.where(first, zero, qtp)
    ci = [ct_ref[h:h + 1, pl.ds(q0, tq)] for h in heads]

    def step(j, carry, masked):
        k0 = pl.multiple_of(j * tk, tk)
        for h in heads:
            kb = k_ref[0, pl.ds(k0, tk), (h // 2) * LANES:(h // 2 + 1) * LANES]
            s_ref[h] = _dot(kb, qh_ref[h])
        m_new, alpha = [], []
        for h in heads:
            mb = jnp.full((SUBLANES, tq), NEG_INF, F32)
            for r in range(tk // rc):
                rs = slice(r * rc, (r + 1) * rc)
                cj = c_ref[pl.ds(pl.multiple_of(k0 + r * rc, rc), rc), h:h + 1]
                s = s_ref[h, rs, :] + ci[h] - cj
                if masked:
                    s = jnp.where(key_i + r * rc <= qry_i, s, NEG_INF)
                s_ref[h, rs, :] = s
                mb = jnp.maximum(mb, _group_reduce(jnp.maximum, s))
            m_new.append(jnp.maximum(carry[3 * h], jnp.max(mb, axis=0, keepdims=True)))
            alpha.append(jnp.exp(carry[3 * h] - m_new[h]))
        l_new = []
        for h in heads:
            lb = jnp.zeros((SUBLANES, tq), F32)
            for r in range(tk // rc):
                rs = slice(r * rc, (r + 1) * rc)
                p = jnp.exp(s_ref[h, rs, :] - m_new[h])
                lb = lb + _group_reduce(jnp.add, p)
                p_ref[h, rs, :] = p.astype(BF)
            l_new.append(alpha[h] * carry[3 * h + 1] + jnp.sum(lb, axis=0, keepdims=True))
        out = []
        for h in heads:
            vh = vt_ref[0, h * HEAD_DIM:(h + 1) * HEAD_DIM, pl.ds(k0, tk)]
            out += [m_new[h], l_new[h], alpha[h] * carry[3 * h + 2] + _dot(vh, p_ref[h])]
        return tuple(out)

    init = (jnp.full((1, tq), NEG_INF, F32), jnp.zeros((1, tq), F32),
            jnp.zeros((HEAD_DIM, tq), F32)) * n_heads
    carry = lax.fori_loop(0, qi, functools.partial(step, masked=False), init)
    fin = step(qi, carry, True)
    for pair in range(n_heads // 2):
        h0, h1 = 2 * pair, 2 * pair + 1
        out_t = jnp.concatenate([fin[3 * h0 + 2] / fin[3 * h0 + 1],
                                 fin[3 * h1 + 2] / fin[3 * h1 + 1]], axis=0)
        o_ref[0, :, pair * LANES:(pair + 1) * LANES] = out_t.T.astype(o_ref.dtype)


def _attn(qt, kb, vtb, lf, lft, tq):
    b, w, t = qt.shape
    n_heads = w // HEAD_DIM
    kern = functools.partial(_attn_kernel, tq=tq, n_heads=n_heads)
    per_b = lambda s1, s2: pl.BlockSpec((1, s1, s2), lambda i, j: (i, 0, 0))
    return pl.pallas_call(
        kern,
        grid=(b, t // tq),
        in_specs=[pl.BlockSpec((1, w, tq), lambda i, j: (i, 0, j)),
                  per_b(t, w), per_b(w, t), per_b(t, LANES), per_b(n_heads, t)],
        out_specs=pl.BlockSpec((1, tq, w), lambda i, j: (i, j, 0)),
        out_shape=jax.ShapeDtypeStruct((b, t, w), BF),
        scratch_shapes=[pltpu.VMEM((t, LANES), F32), pltpu.VMEM((n_heads, t), F32),
                        pltpu.VMEM((n_heads, LANES, tq), BF),
                        pltpu.VMEM((n_heads, tq, tq), F32), pltpu.VMEM((n_heads, tq, tq), BF)],
        compiler_params=_cparams(("parallel", "arbitrary")),
        name="attn_prompt",
    )(qt, kb, vtb, lf, lft)


def _paged_kernel(pt_ref, q_ref, kn_ref, vn_ref, lf_ref, lft_ref, tri_ref, clf_ref, *rest,
                  n_heads, pages, tq):
    k_refs = rest[0:pages]
    v_refs = rest[pages:2 * pages]
    o_ref = rest[2 * pages]
    qbd_ref, ci_ref, m_ref, l_ref, acc_ref, carry_ref = rest[2 * pages + 1:]
    b = pl.program_id(0)
    g = pl.program_id(1)
    first_page = (pl.num_programs(1) - 1 - g) * pages
    rows = n_heads * tq
    w = q_ref.shape[2]

    def expand(x):
        return jnp.concatenate(
            [jnp.broadcast_to(x[h:h + 1, :], (tq, x.shape[1])) for h in range(n_heads)], axis=0)

    def update(s, pv_fn):
        m = m_ref[...]
        m_new = jnp.maximum(m, jnp.max(s, axis=-1, keepdims=True))
        alpha = jnp.exp(m - m_new)
        p = jnp.exp(s - m_new)
        l_ref[...] = alpha * l_ref[...] + jnp.sum(p, axis=-1, keepdims=True)
        acc_ref[...] = alpha * acc_ref[...] + pv_fn(p.astype(BF))
        m_ref[...] = m_new

    @pl.when(g == 0)
    def _():
        q = q_ref[0]
        lane = lax.broadcasted_iota(jnp.int32, (1, w), 1)
        zero = jnp.zeros_like(q)
        qbd_ref[...] = jnp.concatenate(
            [jnp.where((lane >= h * HEAD_DIM) & (lane < (h + 1) * HEAD_DIM), q, zero)
             for h in range(n_heads)], axis=0)
        c = _scan_incl(lf_ref[0], 0)
        ci_ref[...] = jnp.concatenate([c[:, h:h + 1] for h in range(n_heads)], axis=0)
        m_ref[...] = jnp.full((rows, 1), NEG_INF, F32)
        l_ref[...] = jnp.zeros((rows, 1), F32)
        acc_ref[...] = jnp.zeros((rows, w), F32)
        carry_ref[...] = jnp.zeros((n_heads, PAGE_SIZE), F32)
        cj = _scan_incl(lft_ref[0], 1)
        s = _dot_nt(qbd_ref[...], kn_ref[0]) + ci_ref[...] - expand(cj)
        r_idx = lax.broadcasted_iota(jnp.int32, (rows, LANES), 0)
        c_idx = lax.broadcasted_iota(jnp.int32, (rows, LANES), 1)
        s = jnp.where(c_idx <= jnp.bitwise_and(r_idx, tq - 1), s, NEG_INF)
        update(s, lambda p: _dot(p, vn_ref[0]))

    ktc = jnp.concatenate([r[0].reshape(w, PAGE_SIZE) for r in k_refs], axis=1).astype(BF)
    vtc = jnp.concatenate([r[0].reshape(w, PAGE_SIZE) for r in v_refs], axis=1).astype(BF)
    lf = jnp.concatenate([clf_ref[pt_ref[b, first_page + i]] for i in range(pages)], axis=0)
    hi = lf.astype(BF)
    rem = lf - hi.astype(F32)
    mid = rem.astype(BF)
    lo = (rem - mid.astype(F32)).astype(BF)
    y = _dot(jnp.concatenate([hi, mid, lo], axis=0), tri_ref[...])
    ph = pages * n_heads
    y = y[0:ph] + y[ph:2 * ph] + y[2 * ph:3 * ph]
    run = carry_ref[...]
    cjs = [None] * pages
    for pg in reversed(range(pages)):
        blk = slice(pg * n_heads, (pg + 1) * n_heads)
        cjs[pg] = -(run + (y[blk, 0:PAGE_SIZE] - lf[blk]))
        run = run + y[blk, PAGE_SIZE:2 * PAGE_SIZE]
    carry_ref[...] = run
    cj = jnp.concatenate(cjs, axis=1)
    s = _dot(qbd_ref[...], ktc) + ci_ref[...] - expand(cj)
    update(s, lambda p: _dot_nt(p, vtc))

    @pl.when(g == pl.num_programs(1) - 1)
    def _():
        lane = lax.broadcasted_iota(jnp.int32, (1, w), 1)
        out = jnp.zeros((tq, w), F32)
        for h in range(n_heads):
            sl = slice(h * tq, (h + 1) * tq)
            val = acc_ref[sl, :] / l_ref[sl, :]
            out = jnp.where((lane >= h * HEAD_DIM) & (lane < (h + 1) * HEAD_DIM), val, out)
        o_ref[0] = out.astype(o_ref.dtype)


def _paged_attn(page_table, q, kn, vn, lf, lft, cache_kt, cache_vt, cache_lft, pages):
    db, tq, w = q.shape
    n_heads = w // HEAD_DIM
    n_pages = page_table.shape[1]
    ng = n_pages // pages
    rows = n_heads * tq
    assert tq & (tq - 1) == 0

    def page_spec(i, shape):
        nd = len(shape)
        return pl.BlockSpec(shape, lambda b, g, pt: (pt[b, (ng - 1 - g) * pages + i],) + (0,) * (nd - 1))

    per_b = lambda s1, s2: pl.BlockSpec((1, s1, s2), lambda b, g, pt: (b, 0, 0))
    in_specs = [per_b(tq, w), per_b(PAGE_SIZE, w), per_b(PAGE_SIZE, w), per_b(tq, LANES), per_b(n_heads, LANES),
                _const_spec((PAGE_SIZE, 2 * PAGE_SIZE)), _const_spec(cache_lft.shape)]
    in_specs += [page_spec(i, (1, n_heads, HEAD_DIM, PAGE_SIZE)) for i in range(pages)]
    in_specs += [page_spec(i, (1, n_heads, HEAD_DIM, PAGE_SIZE)) for i in range(pages)]
    kern = functools.partial(_paged_kernel, n_heads=n_heads, pages=pages, tq=tq)
    grid_spec = pltpu.PrefetchScalarGridSpec(
        num_scalar_prefetch=1,
        grid=(db, ng),
        in_specs=in_specs,
        out_specs=per_b(tq, w),
        scratch_shapes=[pltpu.VMEM((rows, w), BF), pltpu.VMEM((rows, 1), F32),
                        pltpu.VMEM((rows, 1), F32), pltpu.VMEM((rows, 1), F32),
                        pltpu.VMEM((rows, w), F32), pltpu.VMEM((n_heads, PAGE_SIZE), F32)])
    pos = lax.broadcasted_iota(jnp.int32, (PAGE_SIZE, PAGE_SIZE), 0)
    col = lax.broadcasted_iota(jnp.int32, (PAGE_SIZE, PAGE_SIZE), 1)
    tri = jnp.concatenate([(pos >= col).astype(BF), jnp.ones((PAGE_SIZE, PAGE_SIZE), BF)], axis=1)
    return pl.pallas_call(
        kern,
        grid_spec=grid_spec,
        out_shape=jax.ShapeDtypeStruct((db, tq, w), BF),
        compiler_params=_cparams(("parallel", "arbitrary")),
        name="attn_paged",
    )(page_table, q, kn, vn, lf, lft, tri, cache_lft, *([cache_kt] * pages), *([cache_vt] * pages))


def _mix_tail(pooled, att, x, wp_ref, ps_ref, wo_ref, g_ref, b_ref, alpha):
    gd = pooled[0].shape[1]
    att_w = att.shape[1]
    mixed = [_dot(pooled[g].astype(BF), wp_ref[g]) * ps_ref[:, g * gd:(g + 1) * gd]
             for g in range(len(pooled))]
    pool = jnp.concatenate(mixed, axis=1).astype(BF)
    mix = _dot(att, wo_ref[0:att_w, :]) + _dot(pool, wo_ref[att_w:, :])
    return _layer_norm(alpha * x + mix, g_ref[...], b_ref[...])


def _shift_down(x, s):
    n8, c = x.shape[0] // SUBLANES, x.shape[1]
    if s == SUBLANES:
        return jnp.concatenate([x[0:SUBLANES], x[:-SUBLANES]], axis=0)
    r = pltpu.roll(x.reshape(n8, SUBLANES, c), s, 1)
    prev = jnp.concatenate([r[0:1], r[:-1]], axis=0)
    sub = lax.broadcasted_iota(jnp.int32, (1, SUBLANES, c), 1)
    return jnp.where(sub < s, prev, r).reshape(n8 * SUBLANES, c)


def _mix_prompt_kernel(x_ref, att_ref, u_ref, halo_ref, wp_ref, ps_ref, wo_ref, g_ref, b_ref,
                       o_ref, *, tm, alpha):
    t = pl.program_id(1)
    hl = MAX_WINDOW
    started = (t > 0).astype(F32)
    pos = t * tm + lax.broadcasted_iota(jnp.int32, (tm, 1), 0)
    gd = u_ref.shape[2] // len(POOL_WINDOWS)
    pooled = []
    for g, w in enumerate(POOL_WINDOWS):
        lanes = slice(g * gd, (g + 1) * gd)
        tok = u_ref[0, :, lanes]
        win = jnp.concatenate([halo_ref[0, :, lanes] * started, tok], axis=0)
        k = 1
        while k < w:
            win = win + _shift_down(win, k)
            k *= 2
        cnt = jnp.minimum(pos + 1, w).astype(F32)
        pooled.append(win[hl:] / cnt - tok)
    o_ref[0] = _mix_tail(pooled, att_ref[0], x_ref[0], wp_ref, ps_ref, wo_ref, g_ref, b_ref, alpha)


def _mix_prompt(x, att, u, w_pool, pool_scale, w_out, g, b, alpha, tm):
    bsz, t, d = x.shape
    att_w, pool_w = att.shape[2], u.shape[2]
    hl = MAX_WINDOW
    ng, gd = w_pool.shape[0], w_pool.shape[1]
    kern = functools.partial(_mix_prompt_kernel, tm=tm, alpha=alpha)
    row = lambda w: pl.BlockSpec((1, tm, w), lambda i, j: (i, j, 0))
    halo = pl.BlockSpec((1, hl, pool_w), lambda i, j: (i, jnp.maximum(j * (tm // hl) - 1, 0), 0))
    return pl.pallas_call(
        kern,
        grid=(bsz, t // tm),
        in_specs=[row(d), row(att_w), row(pool_w), halo,
                  _const_spec((ng, gd, gd)), _const_spec((1, pool_w)), _const_spec((att_w + pool_w, d)),
                  _const_spec((1, d)), _const_spec((1, d))],
        out_specs=row(d),
        out_shape=jax.ShapeDtypeStruct((bsz, t, d), F32),
        compiler_params=_cparams(("parallel", "parallel")),
        name="mix_prompt",
    )(x, att, u, u, w_pool, pool_scale, w_out, g, b)


def _mix_sample_kernel(x_ref, att_ref, full_ref, wp_ref, ps_ref, wo_ref, g_ref, b_ref, o_ref,
                       *, ts, n_past, alpha):
    nseq = full_ref.shape[0]
    hl = MAX_WINDOW
    gd = full_ref.shape[2] // len(POOL_WINDOWS)
    pos = n_past + lax.broadcasted_iota(jnp.int32, (1, ts, 1), 1)
    pooled = []
    for g, w in enumerate(POOL_WINDOWS):
        lanes = slice(g * gd, (g + 1) * gd)
        tok = full_ref[:, hl:hl + ts, lanes]
        acc = tok
        for k in range(1, w):
            acc = acc + full_ref[:, hl - k:hl - k + ts, lanes]
        cnt = jnp.minimum(pos + 1, w).astype(F32)
        pooled.append((acc / cnt - tok).reshape(nseq * ts, gd))
    o_ref[...] = _mix_tail(pooled, att_ref[...], x_ref[...], wp_ref, ps_ref, wo_ref, g_ref, b_ref, alpha)


def _mix_sample(x2, att2, full, w_pool, pool_scale, w_out, g, b, alpha, ts, n_past):
    n, d = x2.shape
    kern = functools.partial(_mix_sample_kernel, ts=ts, n_past=n_past, alpha=alpha)
    return pl.pallas_call(
        kern,
        out_shape=jax.ShapeDtypeStruct((n, d), F32),
        compiler_params=pltpu.CompilerParams(vmem_limit_bytes=VMEM_LIMIT),
        name="mix_sample",
    )(x2, att2, full, w_pool, pool_scale, w_out, g, b)


def _shift_rows(h, s):
    n8, c = h.shape[0] // SUBLANES, h.shape[1]
    r = pltpu.roll(h.reshape(n8, SUBLANES, c), s, 1)
    sub = lax.broadcasted_iota(jnp.int32, (1, SUBLANES, c), 1)
    return jnp.where(sub < s, r[:-1], r[1:]).reshape((n8 - 1) * SUBLANES, c)


def _ffn_chunk(hg3, hv3, wc_ref, bc_ref, gcols, vcols):
    def conv(h3, cols, scale):
        hc = bc_ref[:, cols] * scale + h3[0] * (wc_ref[0:1, cols] * scale)
        for j in range(1, CONV_WIDTH):
            hc = hc + h3[j] * (wc_ref[j:j + 1, cols] * scale)
        return hc
    g = conv(hg3, gcols, 1.0)
    vh = conv(hv3, vcols, 0.5)
    t = jnp.tanh(g * (GELU_C1 + (GELU_C1 * GELU_C0) * (g * g)))
    return ((g * (1.0 + t)) * vh).astype(BF)


def _ffn_prompt_kernel(x_ref, halo_ref, wu_ref, bu_ref, wc_ref, bc_ref, wd_ref, g_ref, b_ref,
                       o_ref, tail_ref, *, tm, d_ff, chunk, alpha):
    t = pl.program_id(1)
    hl = SUBLANES
    x = x_ref[0]
    xin = jnp.concatenate([halo_ref[0], x], axis=0).astype(BF)
    started = (t > 0).astype(F32)
    n_chunks = d_ff // chunk
    cols_of = lambda c: (slice(c * chunk, (c + 1) * chunk), slice(d_ff + c * chunk, d_ff + (c + 1) * chunk))

    def up(c):
        return [_dot(xin, wu_ref[:, cols]) for cols in cols_of(c)]

    def shifted(hd, cols):
        body = hd[hl:] + bu_ref[:, cols]
        tail_ref[0, :, cols] = body[tm - hl:tm]
        h = jnp.concatenate([(hd[0:hl] + bu_ref[:, cols]) * started, body], axis=0)
        return [_shift_rows(h, CONV_WIDTH - 1 - j) for j in range(CONV_WIDTH - 1)] + [body]

    acc = jnp.zeros((tm, x.shape[1]), F32)
    nxt = up(0)
    for c in range(n_chunks):
        cur = nxt
        if c + 1 < n_chunks:
            nxt = up(c + 1)
        gcols, vcols = cols_of(c)
        act = _ffn_chunk(shifted(cur[0], gcols), shifted(cur[1], vcols), wc_ref, bc_ref, gcols, vcols)
        acc = acc + _dot(act, wd_ref[gcols, :])
    o_ref[0] = _layer_norm(alpha * x + acc, g_ref[...], b_ref[...])


def _ffn_prompt(x1, w_up, b_up, w_conv, b_conv, w_down, g, b, alpha, tm, chunk):
    bsz, t, d = x1.shape
    d_ff = w_down.shape[0]
    hl = SUBLANES
    kern = functools.partial(_ffn_prompt_kernel, tm=tm, d_ff=d_ff, chunk=chunk, alpha=alpha)
    row = pl.BlockSpec((1, tm, d), lambda i, j: (i, j, 0))
    halo = pl.BlockSpec((1, hl, d), lambda i, j: (i, jnp.maximum(j * (tm // hl) - 1, 0), 0))
    return pl.pallas_call(
        kern,
        grid=(bsz, t // tm),
        in_specs=[row, halo, _const_spec((d, 2 * d_ff)), _const_spec((1, 2 * d_ff)),
                  _const_spec((CONV_WIDTH, 2 * d_ff)), _const_spec((1, 2 * d_ff)),
                  _const_spec((d_ff, d)), _const_spec((1, d)), _const_spec((1, d))],
        out_specs=[row, pl.BlockSpec((1, hl, 2 * d_ff), lambda i, j: (i, 0, 0))],
        out_shape=[jax.ShapeDtypeStruct((bsz, t, d), F32),
                   jax.ShapeDtypeStruct((bsz, hl, 2 * d_ff), F32)],
        compiler_params=_cparams(("parallel", "arbitrary")),
        name="ffn_prompt",
    )(x1, x1, w_up, b_up, w_conv, b_conv, w_down, g, b)


def _ffn_sample_kernel(x_ref, st_ref, wu_ref, bu_ref, wc_ref, bc_ref, wd_ref, g_ref, b_ref,
                       o_ref, h_ref, buf_ref, *, ts, d_ff, chunk, alpha):
    nseq = st_ref.shape[0]
    n = nseq * ts
    hl = SUBLANES
    nst = CONV_WIDTH - 1
    x = x_ref[...]
    xb = x.astype(BF)
    acc = jnp.zeros((n, x.shape[1]), F32)
    for c in range(d_ff // chunk):
        gcols = slice(c * chunk, (c + 1) * chunk)
        vcols = slice(d_ff + c * chunk, d_ff + (c + 1) * chunk)
        h3 = []
        for cols in (gcols, vcols):
            h = _dot(xb, wu_ref[:, cols]) + bu_ref[:, cols]
            h_ref[:, cols] = h
            buf_ref[:, hl - nst:hl, :] = st_ref[:, :, cols]
            buf_ref[:, hl:hl + ts, :] = h.reshape(nseq, ts, chunk)
            h3.append([buf_ref[:, hl - 2 + j:hl - 2 + j + ts, :].reshape(n, chunk)
                       for j in range(CONV_WIDTH)])
        act = _ffn_chunk(h3[0], h3[1], wc_ref, bc_ref, gcols, vcols)
        acc = acc + _dot(act, wd_ref[gcols, :])
    o_ref[...] = _layer_norm(alpha * x + acc, g_ref[...], b_ref[...])


def _ffn_sample(x2, state, w_up, b_up, w_conv, b_conv, w_down, g, b, alpha, ts, chunk):
    n, d = x2.shape
    d_ff = w_down.shape[0]
    nseq = n // ts
    kern = functools.partial(_ffn_sample_kernel, ts=ts, d_ff=d_ff, chunk=chunk, alpha=alpha)
    return pl.pallas_call(
        kern,
        out_shape=[jax.ShapeDtypeStruct((n, d), F32), jax.ShapeDtypeStruct((n, 2 * d_ff), F32)],
        scratch_shapes=[pltpu.VMEM((nseq, SUBLANES + ts, chunk), F32)],
        compiler_params=pltpu.CompilerParams(vmem_limit_bytes=VMEM_LIMIT),
        name="ffn_sample",
    )(x2, state, w_up, b_up, w_conv, b_conv, w_down, g, b)


def _ffn_chunk_size(d_ff):
    for c in (512, 256, 128):
        if d_ff % c == 0:
            return c
    raise ValueError("d_ff must be a multiple of 128")


def kernel(x_prompt, x_sample, cache_k, cache_v, cache_logf, state_pool, state_conv, page_table,
           w_in, b_f, w_pool, pool_scale, w_out, ln1_g, ln1_b, w_up, b_up, w_conv, b_conv, w_down,
           ln2_g, ln2_b):
    bsz, t_p, d = x_prompt.shape
    db, t_s, _ = x_sample.shape
    depth = w_in.shape[0]
    n_heads = cache_k.shape[3]
    att_w = n_heads * HEAD_DIM
    pool_w = state_pool.shape[3]
    d_ff = w_down.shape[1]
    n_past = page_table.shape[1] * PAGE_SIZE
    alpha = (2 * depth) ** 0.25
    assert HEAD_DIM * 2 == LANES and n_heads % 2 == 0 and n_heads <= SUBLANES
    assert t_s == SUBLANES and cache_k.shape[2] == PAGE_SIZE and t_p >= MAX_WINDOW
    assert w_in.shape[2] == 3 * att_w + n_heads + pool_w

    tm_proj = min(512, t_p)
    tq = min(256, t_p)
    tm_mix = min(256, t_p)
    tm_ffn = min(256, t_p)
    pages = next(p for p in (16, 8, 4, 2, 1) if page_table.shape[1] % p == 0)
    chunk = _ffn_chunk_size(d_ff)

    yp, ys = x_prompt, x_sample
    outs = [[] for _ in range(10)]
    for l in range(depth):
        wi = w_in[l]
        fcols = wi[:, 3 * att_w:3 * att_w + n_heads]
        w_row = jnp.concatenate(
            [wi[:, :2 * att_w], wi[:, 3 * att_w + n_heads:], jnp.pad(fcols, ((0, 0), (0, LANES - n_heads)))],
            axis=1).astype(BF)
        w_t = jnp.concatenate(
            [wi[:, :3 * att_w].T, jnp.pad(fcols.T, ((0, BF_ROWS - n_heads), (0, 0)))],
            axis=0).astype(BF)
        bf_row = jnp.pad(b_f[l], (0, LANES - n_heads)).reshape(1, LANES)
        bf_col = b_f[l].reshape(n_heads, 1)
        wp_b = w_pool[l].astype(BF)
        ps = pool_scale[l].reshape(1, pool_w)
        wo_b = w_out[l].astype(BF)
        g1, b1 = ln1_g[l].reshape(1, d), ln1_b[l].reshape(1, d)
        wu_b = w_up[l].astype(BF)
        bu = b_up[l].reshape(1, 2 * d_ff)
        wc = w_conv[l]
        bc = b_conv[l].reshape(1, 2 * d_ff)
        wd_b = w_down[l].astype(BF)
        g2, b2 = ln2_g[l].reshape(1, d), ln2_b[l].reshape(1, d)

        qt, kb, kt, vt, vtb, u, lf, lft = _proj_prompt(yp, w_row, w_t, bf_row, bf_col, att_w, pool_w,
                                                       n_heads, tm_proj)
        att = _attn(qt, kb, vtb, lf, lft, tq)
        x1 = _mix_prompt(yp, att, u, wp_b, ps, wo_b, g1, b1, alpha, tm_mix)
        yp, tail = _ffn_prompt(x1, wu_b, bu, wc, bc, wd_b, g2, b2, alpha, tm_ffn, chunk)
        to_bthd = lambda a: jnp.transpose(a.reshape(bsz, n_heads, HEAD_DIM, t_p), (0, 3, 1, 2))
        outs[0].append(to_bthd(kt))
        outs[1].append(to_bthd(vt))
        outs[2].append(jnp.swapaxes(lft, 1, 2))
        outs[3].append(u[:, t_p - (MAX_WINDOW - 1):, :])
        outs[4].append(tail[:, SUBLANES - (CONV_WIDTH - 1):, :])

        n_s = db * t_s
        qs, ks, vs, kbs, vbs, us, lfs = _proj_sample(ys.reshape(n_s, d), w_row, w_t, bf_row, att_w, pool_w)
        lfs3 = lfs.reshape(db, t_s, LANES)
        lfts = jnp.pad(jnp.swapaxes(lfs3[:, :, :n_heads], 1, 2), ((0, 0), (0, 0), (0, LANES - t_s)))
        padk = lambda a: jnp.pad(a.reshape(db, t_s, att_w), ((0, 0), (0, PAGE_SIZE - t_s), (0, 0)))
        ckt = jnp.transpose(cache_k[l], (0, 2, 3, 1))
        cvt = jnp.transpose(cache_v[l], (0, 2, 3, 1))
        clft = jnp.swapaxes(cache_logf[l], 1, 2)
        att_s = _paged_attn(page_table, qs.reshape(db, t_s, att_w), padk(kbs), padk(vbs), lfs3, lfts,
                            ckt, cvt, clft, pages)
        us3 = us.reshape(db, t_s, pool_w)
        full = jnp.concatenate([jnp.zeros((db, 1, pool_w), F32), state_pool[l], us3], axis=1)
        x1s = _mix_sample(ys.reshape(n_s, d), att_s.reshape(n_s, att_w), full, wp_b, ps, wo_b, g1, b1,
                          alpha, t_s, n_past)
        ys2, hs = _ffn_sample(x1s, state_conv[l], wu_b, bu, wc, bc, wd_b, g2, b2, alpha, t_s, chunk)
        ys = ys2.reshape(db, t_s, d)
        outs[5].append(ks.reshape(db, t_s, n_heads, HEAD_DIM))
        outs[6].append(vs.reshape(db, t_s, n_heads, HEAD_DIM))
        outs[7].append(lfs3[:, :, :n_heads])
        outs[8].append(full[:, t_s + 1:, :])
        outs[9].append(hs.reshape(db, t_s, 2 * d_ff)[:, t_s - (CONV_WIDTH - 1):, :])

    st = [jnp.stack(o, 0) for o in outs]
    return (yp, ys, st[0], st[1], st[2], st[3], st[4], st[5], st[6], st[7], st[8], st[9])
```

```python
import functools

import jax
import jax.numpy as jnp
import numpy as np
from jax import lax
from jax.experimental import pallas as pl
from jax.experimental.pallas import tpu as pltpu

BF = jnp.bfloat16
F32 = jnp.float32

HEAD_DIM = 64
PAGE_SIZE = 128
POOL_WINDOWS = (2, 4, 8, 16)
MAX_WINDOW = max(POOL_WINDOWS)
CONV_WIDTH = 3
LN_EPS = 1e-5
NEG_INF = -1e30
GELU_C1 = float(np.float32(np.sqrt(2.0 / np.pi)))
GELU_C0 = float(np.float32(0.044715))
LOG2E = float(np.log2(np.e))
LANES = 128
SUBLANES = 8
BF_ROWS = 16
VMEM_LIMIT = 56 * 1024 * 1024


def _cparams(sem):
    return pltpu.CompilerParams(dimension_semantics=sem, vmem_limit_bytes=VMEM_LIMIT)


def _const_spec(shape):
    nd = len(shape)
    return pl.BlockSpec(shape, lambda *_: (0,) * nd, pipeline_mode=pl.Buffered(1))


def _dot(a, b):
    return jnp.dot(a, b, preferred_element_type=F32)


def _dot_nt(a, b):
    return lax.dot_general(a, b, (((1,), (1,)), ((), ())), preferred_element_type=F32)


def _layer_norm(r, g, b):
    mu = jnp.mean(r, axis=-1, keepdims=True)
    d = r - mu
    var = jnp.mean(d * d, axis=-1, keepdims=True)
    return d * lax.rsqrt(var + LN_EPS) * g + b


def _scan_incl(x, axis, reverse=False):
    n = x.shape[axis]
    idx = lax.broadcasted_iota(jnp.int32, x.shape, axis)
    s = 1
    while s < n:
        if reverse:
            x = x + jnp.where(idx + s < n, pltpu.roll(x, n - s, axis), 0.0)
        else:
            x = x + jnp.where(idx >= s, pltpu.roll(x, s, axis), 0.0)
        s *= 2
    return x


def _proj_prompt_kernel(x_ref, wr_ref, wt_ref, bfr_ref, bfc_ref,
                        qt_ref, kb_ref, kt_ref, vt_ref, vtb_ref, u_ref, lf_ref, lft_ref,
                        *, att_w, pool_w, n_heads, scale):
    xb = x_ref[0].astype(BF)
    kb_ref[0] = _dot(xb, wr_ref[:, att_w:2 * att_w]).astype(BF)
    u_ref[0] = _dot(xb, wr_ref[:, 2 * att_w:2 * att_w + pool_w])
    z = _dot(xb, wr_ref[:, 2 * att_w + pool_w:2 * att_w + pool_w + LANES]) + bfr_ref[...]
    lf_ref[0] = jax.nn.log_sigmoid(z)
    qt_ref[0] = (_dot_nt(wt_ref[0:att_w, :], xb) * scale).astype(BF)
    kt_ref[0] = _dot_nt(wt_ref[att_w:2 * att_w, :], xb)
    vt = _dot_nt(wt_ref[2 * att_w:3 * att_w, :], xb)
    vt_ref[0] = vt
    vtb_ref[0] = vt.astype(BF)
    zt = _dot_nt(wt_ref[3 * att_w:3 * att_w + BF_ROWS, :], xb)[0:n_heads] + bfc_ref[...]
    lft_ref[0] = jax.nn.log_sigmoid(zt)


def _proj_prompt(x, w_row, w_t, bf_row, bf_col, att_w, pool_w, n_heads, tm):
    bsz, t, d = x.shape
    kern = functools.partial(_proj_prompt_kernel, att_w=att_w, pool_w=pool_w, n_heads=n_heads,
                             scale=HEAD_DIM ** -0.5 * LOG2E)
    row = lambda w: pl.BlockSpec((1, tm, w), lambda i, j: (i, j, 0))
    col = lambda h: pl.BlockSpec((1, h, tm), lambda i, j: (i, 0, j))
    return pl.pallas_call(
        kern,
        grid=(bsz, t // tm),
        in_specs=[row(d), _const_spec(w_row.shape), _const_spec(w_t.shape),
                  _const_spec((1, LANES)), _const_spec((n_heads, 1))],
        out_specs=[col(att_w), row(att_w), col(att_w), col(att_w), col(att_w), row(pool_w),
                   row(LANES), col(n_heads)],
        out_shape=[jax.ShapeDtypeStruct((bsz, att_w, t), BF),
                   jax.ShapeDtypeStruct((bsz, t, att_w), BF),
                   jax.ShapeDtypeStruct((bsz, att_w, t), F32),
                   jax.ShapeDtypeStruct((bsz, att_w, t), F32),
                   jax.ShapeDtypeStruct((bsz, att_w, t), BF),
                   jax.ShapeDtypeStruct((bsz, t, pool_w), F32),
                   jax.ShapeDtypeStruct((bsz, t, LANES), F32),
                   jax.ShapeDtypeStruct((bsz, n_heads, t), F32)],
        compiler_params=_cparams(("parallel", "parallel")),
        name="proj_prompt",
    )(x, w_row, w_t, bf_row, bf_col)


def _proj_sample_kernel(x_ref, wr_ref, wt_ref, bfr_ref, q_ref, k_ref, v_ref, kb_ref, vb_ref, u_ref, lf_ref,
                        *, att_w, pool_w, scale):
    xb = x_ref[...].astype(BF)
    q_ref[...] = (_dot(xb, wr_ref[:, 0:att_w]) * scale).astype(BF)
    u_ref[...] = _dot(xb, wr_ref[:, 2 * att_w:2 * att_w + pool_w])
    z = _dot(xb, wr_ref[:, 2 * att_w + pool_w:2 * att_w + pool_w + LANES]) + bfr_ref[...]
    lf_ref[...] = jax.nn.log_sigmoid(z)
    k = _dot(xb, wr_ref[:, att_w:2 * att_w])
    k_ref[...] = k
    kb_ref[...] = k.astype(BF)
    v = _dot_nt(xb, wt_ref[2 * att_w:3 * att_w, :])
    v_ref[...] = v
    vb_ref[...] = v.astype(BF)


def _proj_sample(x2, w_row, w_t, bf_row, att_w, pool_w):
    n, d = x2.shape
    kern = functools.partial(_proj_sample_kernel, att_w=att_w, pool_w=pool_w, scale=HEAD_DIM ** -0.5)
    return pl.pallas_call(
        kern,
        out_shape=[jax.ShapeDtypeStruct((n, att_w), BF),
                   jax.ShapeDtypeStruct((n, att_w), F32),
                   jax.ShapeDtypeStruct((n, att_w), F32),
                   jax.ShapeDtypeStruct((n, att_w), BF),
                   jax.ShapeDtypeStruct((n, att_w), BF),
                   jax.ShapeDtypeStruct((n, pool_w), F32),
                   jax.ShapeDtypeStruct((n, LANES), F32)],
        compiler_params=pltpu.CompilerParams(vmem_limit_bytes=VMEM_LIMIT),
        name="proj_sample",
    )(x2, w_row, w_t, bf_row)


KEY_CHUNK = 32


def _group_reduce(fn, x):
    parts = [x[i * SUBLANES:(i + 1) * SUBLANES] for i in range(x.shape[0] // SUBLANES)]
    while len(parts) > 1:
        parts = [fn(parts[i], parts[i + 1]) if i + 1 < len(parts) else parts[i]
                 for i in range(0, len(parts), 2)]
    return parts[0]


AUG_ROWS = 16


def _split3(x):
    hi = x.astype(BF)
    r1 = x - hi.astype(F32)
    mid = r1.astype(BF)
    lo = (r1 - mid.astype(F32)).astype(BF)
    return hi, mid, lo


def _attn_kernel(qt_ref, k_ref, vt_ref, lf_ref, lft_ref, sel_ref, one_ref, o_ref,
                 ct_ref, kx_ref, qx_ref, ml_ref, *per_head, tq, n_heads):
    qi = pl.program_id(1)
    tk = tq
    rc = KEY_CHUNK
    t = k_ref.shape[1]
    n_pairs = n_heads // 2
    s_refs, p_refs, acc_refs = (per_head[i * n_heads:(i + 1) * n_heads] for i in range(3))

    @pl.when(qi == 0)
    def _():
        ct_ref[...] = _scan_incl(lft_ref[0], 1) * LOG2E
        blk = 512 if t % 512 == 0 else t
        for r0 in range(0, t, blk):
            rows = slice(r0, r0 + blk)
            c = _scan_incl(lf_ref[0, rows, :], 0)
            if r0:
                c = c + tot
            tot = c[blk - 1:blk, :]
            hi, mid, lo = _split3(c * LOG2E)
            aug = _dot(jnp.concatenate([hi, mid, lo], axis=1), sel_ref[...]) + one_ref[...]
            for pr in range(n_pairs):
                kx_ref[pr, rows, 0:LANES] = k_ref[0, rows, pr * LANES:(pr + 1) * LANES]
                kx_ref[pr, rows, LANES:2 * LANES] = aug[:, pr * LANES:(pr + 1) * LANES].astype(BF)
        qx_ref[:, LANES + AUG_ROWS:, :] = jnp.zeros((n_heads, LANES - AUG_ROWS, tq), BF)

    q0 = pl.multiple_of(qi * tq, tq)
    first = lax.broadcasted_iota(jnp.int32, (LANES, 1), 0) < HEAD_DIM
    causal = (lax.broadcasted_iota(jnp.int32, (tk, tq), 0)
              <= lax.broadcasted_iota(jnp.int32, (tk, tq), 1))
    arow = lax.broadcasted_iota(jnp.int32, (AUG_ROWS, 1), 0)

    heads = range(n_heads)
    for pair in range(n_pairs):
        qtp = qt_ref[0, pair * LANES:(pair + 1) * LANES, :]
        zero = jnp.zeros_like(qtp)
        qx_ref[2 * pair, 0:LANES, :] = jnp.where(first, qtp, zero)
        qx_ref[2 * pair + 1, 0:LANES, :] = jnp.where(first, zero, qtp)
    for h in heads:
        hf = h % 2
        ci = ct_ref[h:h + 1, pl.ds(q0, tq)]
        hi = ci.astype(BF).astype(F32)
        r1 = ci - hi
        mid = r1.astype(BF).astype(F32)
        lo = r1 - mid
        pick = ((arow >= 3 * hf) & (arow < 3 * hf + 3)).astype(F32)
        blk = jnp.where(arow == 6, hi, jnp.where(arow == 7, mid, jnp.where(arow == 8, lo, pick)))
        qx_ref[h, LANES:LANES + AUG_ROWS, :] = blk.astype(BF)

    def step(j, masked):
        k0 = pl.multiple_of(j * tk, tk)
        m_new, alpha = [], []
        for h in heads:
            s = _dot(kx_ref[h // 2, pl.ds(k0, tk), :], qx_ref[h])
            if masked:
                s = jnp.where(causal, s, NEG_INF)
            s_refs[h][...] = s
            mb = _group_reduce(jnp.maximum, s)
            m_old = ml_ref[2 * h:2 * h + 1, :]
            m_new.append(jnp.maximum(m_old, jnp.max(mb, axis=0, keepdims=True)))
            alpha.append(jnp.exp2(m_old - m_new[h]))
            ml_ref[2 * h:2 * h + 1, :] = m_new[h]
        for h in heads:
            lb = jnp.zeros((SUBLANES, tq), F32)
            for r in range(tk // rc):
                p = jnp.exp2(s_refs[h][r * rc:(r + 1) * rc, :] - m_new[h])
                lb = lb + _group_reduce(jnp.add, p)
                p_refs[h][r * rc:(r + 1) * rc, :] = p.astype(BF)
            ml_ref[2 * h + 1:2 * h + 2, :] = (alpha[h] * ml_ref[2 * h + 1:2 * h + 2, :]
                                              + jnp.sum(lb, axis=0, keepdims=True))
        for h in heads:
            vh = vt_ref[0, h * HEAD_DIM:(h + 1) * HEAD_DIM, pl.ds(k0, tk)]
            acc_refs[h][...] = alpha[h] * acc_refs[h][...] + _dot(vh, p_refs[h][...])

    is_max = lax.broadcasted_iota(jnp.int32, (2 * n_heads, 1), 0) % 2 == 0
    ml_ref[...] = jnp.where(is_max, jnp.full((2 * n_heads, tq), NEG_INF, F32), 0.0)
    for h in heads:
        acc_refs[h][...] = jnp.zeros((HEAD_DIM, tq), F32)

    def body(j, _):
        step(j, False)
        return 0

    lax.fori_loop(0, qi, body, 0)
    step(qi, True)
    for pair in range(n_heads // 2):
        h0, h1 = 2 * pair, 2 * pair + 1
        out_t = jnp.concatenate([acc_refs[h0][...] / ml_ref[2 * h0 + 1:2 * h0 + 2, :],
                                 acc_refs[h1][...] / ml_ref[2 * h1 + 1:2 * h1 + 2, :]], axis=0)
        o_ref[0, :, pair * LANES:(pair + 1) * LANES] = out_t.T.astype(o_ref.dtype)


def _attn(qt, kb, vtb, lf, lft, tq):
    b, w, t = qt.shape
    n_heads = w // HEAD_DIM
    n_pairs = n_heads // 2
    sel = np.zeros((3 * LANES, n_pairs * LANES), np.float32)
    one = np.zeros((1, n_pairs * LANES), np.float32)
    for pr in range(n_pairs):
        for hf in range(2):
            for piece in range(3):
                sel[piece * LANES + 2 * pr + hf, pr * LANES + 3 * hf + piece] = -1.0
        one[0, pr * LANES + 6:pr * LANES + 9] = 1.0
    kern = functools.partial(_attn_kernel, tq=tq, n_heads=n_heads)
    per_b = lambda s1, s2: pl.BlockSpec((1, s1, s2), lambda i, j: (i, 0, 0))
    return pl.pallas_call(
        kern,
        grid=(b, t // tq),
        in_specs=[pl.BlockSpec((1, w, tq), lambda i, j: (i, 0, j)),
                  per_b(t, w), per_b(w, t), per_b(t, LANES), per_b(n_heads, t),
                  _const_spec(sel.shape), _const_spec(one.shape)],
        out_specs=pl.BlockSpec((1, tq, w), lambda i, j: (i, j, 0)),
        out_shape=jax.ShapeDtypeStruct((b, t, w), BF),
        scratch_shapes=[pltpu.VMEM((n_heads, t), F32),
                        pltpu.VMEM((n_pairs, t, 2 * LANES), BF),
                        pltpu.VMEM((n_heads, 2 * LANES, tq), BF),
                        pltpu.VMEM((2 * n_heads, tq), F32)]
                       + [pltpu.VMEM((tq, tq), F32)] * n_heads
                       + [pltpu.VMEM((tq, tq), BF)] * n_heads
                       + [pltpu.VMEM((HEAD_DIM, tq), F32)] * n_heads,
        compiler_params=_cparams(("parallel", "arbitrary")),
        name="attn_prompt",
    )(qt, kb, vtb, lf, lft, jnp.asarray(sel, BF), jnp.asarray(one, F32))


def _paged_kernel(pt_ref, q_ref, kn_ref, vn_ref, lf_ref, lft_ref, tri_ref, clf_ref, *rest,
                  n_heads, pages, tq):
    k_refs = rest[0:pages]
    v_refs = rest[pages:2 * pages]
    o_ref = rest[2 * pages]
    qbd_ref, ci_ref, m_ref, l_ref, acc_ref, carry_ref = rest[2 * pages + 1:]
    b = pl.program_id(0)
    g = pl.program_id(1)
    first_page = (pl.num_programs(1) - 1 - g) * pages
    rows = n_heads * tq
    w = q_ref.shape[2]

    def expand(x):
        return jnp.concatenate(
            [jnp.broadcast_to(x[h:h + 1, :], (tq, x.shape[1])) for h in range(n_heads)], axis=0)

    def update(s, pv_fn):
        m = m_ref[...]
        m_new = jnp.maximum(m, jnp.max(s, axis=-1, keepdims=True))
        alpha = jnp.exp(m - m_new)
        p = jnp.exp(s - m_new)
        l_ref[...] = alpha * l_ref[...] + jnp.sum(p, axis=-1, keepdims=True)
        acc_ref[...] = alpha * acc_ref[...] + pv_fn(p.astype(BF))
        m_ref[...] = m_new

    @pl.when(g == 0)
    def _():
        q = q_ref[0]
        lane = lax.broadcasted_iota(jnp.int32, (1, w), 1)
        zero = jnp.zeros_like(q)
        qbd_ref[...] = jnp.concatenate(
            [jnp.where((lane >= h * HEAD_DIM) & (lane < (h + 1) * HEAD_DIM), q, zero)
             for h in range(n_heads)], axis=0)
        c = _scan_incl(lf_ref[0], 0)
        ci_ref[...] = jnp.concatenate([c[:, h:h + 1] for h in range(n_heads)], axis=0)
        m_ref[...] = jnp.full((rows, 1), NEG_INF, F32)
        l_ref[...] = jnp.zeros((rows, 1), F32)
        acc_ref[...] = jnp.zeros((rows, w), F32)
        carry_ref[...] = jnp.zeros((n_heads, PAGE_SIZE), F32)
        cj = _scan_incl(lft_ref[0], 1)
        s = _dot_nt(qbd_ref[...], kn_ref[0]) + ci_ref[...] - expand(cj)
        r_idx = lax.broadcasted_iota(jnp.int32, (rows, LANES), 0)
        c_idx = lax.broadcasted_iota(jnp.int32, (rows, LANES), 1)
        s = jnp.where(c_idx <= jnp.bitwise_and(r_idx, tq - 1), s, NEG_INF)
        update(s, lambda p: _dot(p, vn_ref[0]))

    ktc = jnp.concatenate([r[0].reshape(w, PAGE_SIZE) for r in k_refs], axis=1).astype(BF)
    vtc = jnp.concatenate([r[0].reshape(w, PAGE_SIZE) for r in v_refs], axis=1).astype(BF)
    lf = jnp.concatenate([clf_ref[pt_ref[b, first_page + i]] for i in range(pages)], axis=0)
    hi = lf.astype(BF)
    rem = lf - hi.astype(F32)
    mid = rem.astype(BF)
    lo = (rem - mid.astype(F32)).astype(BF)
    y = _dot(jnp.concatenate([hi, mid, lo], axis=0), tri_ref[...])
    ph = pages * n_heads
    y = y[0:ph] + y[ph:2 * ph] + y[2 * ph:3 * ph]
    run = carry_ref[...]
    cjs = [None] * pages
    for pg in reversed(range(pages)):
        blk = slice(pg * n_heads, (pg + 1) * n_heads)
        cjs[pg] = -(run + (y[blk, 0:PAGE_SIZE] - lf[blk]))
        run = run + y[blk, PAGE_SIZE:2 * PAGE_SIZE]
    carry_ref[...] = run
    cj = jnp.concatenate(cjs, axis=1)
    s = _dot(qbd_ref[...], ktc) + ci_ref[...] - expand(cj)
    update(s, lambda p: _dot_nt(p, vtc))

    @pl.when(g == pl.num_programs(1) - 1)
    def _():
        lane = lax.broadcasted_iota(jnp.int32, (1, w), 1)
        out = jnp.zeros((tq, w), F32)
        for h in range(n_heads):
            sl = slice(h * tq, (h + 1) * tq)
            val = acc_ref[sl, :] / l_ref[sl, :]
            out = jnp.where((lane >= h * HEAD_DIM) & (lane < (h + 1) * HEAD_DIM), val, out)
        o_ref[0] = out.astype(o_ref.dtype)


def _paged_attn(page_table, q, kn, vn, lf, lft, cache_kt, cache_vt, cache_lft, pages):
    db, tq, w = q.shape
    n_heads = w // HEAD_DIM
    n_pages = page_table.shape[1]
    ng = n_pages // pages
    rows = n_heads * tq
    assert tq & (tq - 1) == 0

    def page_spec(i, shape):
        nd = len(shape)
        return pl.BlockSpec(shape, lambda b, g, pt: (pt[b, (ng - 1 - g) * pages + i],) + (0,) * (nd - 1))

    per_b = lambda s1, s2: pl.BlockSpec((1, s1, s2), lambda b, g, pt: (b, 0, 0))
    in_specs = [per_b(tq, w), per_b(PAGE_SIZE, w), per_b(PAGE_SIZE, w), per_b(tq, LANES), per_b(n_heads, LANES),
                _const_spec((PAGE_SIZE, 2 * PAGE_SIZE)), _const_spec(cache_lft.shape)]
    in_specs += [page_spec(i, (1, n_heads, HEAD_DIM, PAGE_SIZE)) for i in range(pages)]
    in_specs += [page_spec(i, (1, n_heads, HEAD_DIM, PAGE_SIZE)) for i in range(pages)]
    kern = functools.partial(_paged_kernel, n_heads=n_heads, pages=pages, tq=tq)
    grid_spec = pltpu.PrefetchScalarGridSpec(
        num_scalar_prefetch=1,
        grid=(db, ng),
        in_specs=in_specs,
        out_specs=per_b(tq, w),
        scratch_shapes=[pltpu.VMEM((rows, w), BF), pltpu.VMEM((rows, 1), F32),
                        pltpu.VMEM((rows, 1), F32), pltpu.VMEM((rows, 1), F32),
                        pltpu.VMEM((rows, w), F32), pltpu.VMEM((n_heads, PAGE_SIZE), F32)])
    pos = lax.broadcasted_iota(jnp.int32, (PAGE_SIZE, PAGE_SIZE), 0)
    col = lax.broadcasted_iota(jnp.int32, (PAGE_SIZE, PAGE_SIZE), 1)
    tri = jnp.concatenate([(pos >= col).astype(BF), jnp.ones((PAGE_SIZE, PAGE_SIZE), BF)], axis=1)
    return pl.pallas_call(
        kern,
        grid_spec=grid_spec,
        out_shape=jax.ShapeDtypeStruct((db, tq, w), BF),
        compiler_params=_cparams(("parallel", "arbitrary")),
        name="attn_paged",
    )(page_table, q, kn, vn, lf, lft, tri, cache_lft, *([cache_kt] * pages), *([cache_vt] * pages))


def _mix_tail(pooled, att, x, wp_ref, ps_ref, wo_ref, g_ref, b_ref, alpha):
    gd = pooled[0].shape[1]
    att_w = att.shape[1]
    mixed = [_dot(pooled[g].astype(BF), wp_ref[g]) * ps_ref[:, g * gd:(g + 1) * gd]
             for g in range(len(pooled))]
    pool = jnp.concatenate(mixed, axis=1).astype(BF)
    mix = _dot(att, wo_ref[0:att_w, :]) + _dot(pool, wo_ref[att_w:, :])
    return _layer_norm(alpha * x + mix, g_ref[...], b_ref[...])


def _shift_down(x, s):
    n8, c = x.shape[0] // SUBLANES, x.shape[1]
    if s == SUBLANES:
        return jnp.concatenate([x[0:SUBLANES], x[:-SUBLANES]], axis=0)
    r = pltpu.roll(x.reshape(n8, SUBLANES, c), s, 1)
    prev = jnp.concatenate([r[0:1], r[:-1]], axis=0)
    sub = lax.broadcasted_iota(jnp.int32, (1, SUBLANES, c), 1)
    return jnp.where(sub < s, prev, r).reshape(n8 * SUBLANES, c)


def _mix_prompt_kernel(x_ref, att_ref, u_ref, halo_ref, wp_ref, ps_ref, wo_ref, g_ref, b_ref,
                       o_ref, *, tm, alpha):
    t = pl.program_id(1)
    hl = MAX_WINDOW
    started = (t > 0).astype(F32)
    pos = t * tm + lax.broadcasted_iota(jnp.int32, (tm, 1), 0)
    gd = u_ref.shape[2] // len(POOL_WINDOWS)
    pooled = []
    for g, w in enumerate(POOL_WINDOWS):
        lanes = slice(g * gd, (g + 1) * gd)
        tok = u_ref[0, :, lanes]
        win = jnp.concatenate([halo_ref[0, :, lanes] * started, tok], axis=0)
        k = 1
        while k < w:
            win = win + _shift_down(win, k)
            k *= 2
        cnt = jnp.minimum(pos + 1, w).astype(F32)
        pooled.append(win[hl:] / cnt - tok)
    o_ref[0] = _mix_tail(pooled, att_ref[0], x_ref[0], wp_ref, ps_ref, wo_ref, g_ref, b_ref, alpha)


def _mix_prompt(x, att, u, w_pool, pool_scale, w_out, g, b, alpha, tm):
    bsz, t, d = x.shape
    att_w, pool_w = att.shape[2], u.shape[2]
    hl = MAX_WINDOW
    ng, gd = w_pool.shape[0], w_pool.shape[1]
    kern = functools.partial(_mix_prompt_kernel, tm=tm, alpha=alpha)
    row = lambda w: pl.BlockSpec((1, tm, w), lambda i, j: (i, j, 0))
    halo = pl.BlockSpec((1, hl, pool_w), lambda i, j: (i, jnp.maximum(j * (tm // hl) - 1, 0), 0))
    return pl.pallas_call(
        kern,
        grid=(bsz, t // tm),
        in_specs=[row(d), row(att_w), row(pool_w), halo,
                  _const_spec((ng, gd, gd)), _const_spec((1, pool_w)), _const_spec((att_w + pool_w, d)),
                  _const_spec((1, d)), _const_spec((1, d))],
        out_specs=row(d),
        out_shape=jax.ShapeDtypeStruct((bsz, t, d), F32),
        compiler_params=_cparams(("parallel", "parallel")),
        name="mix_prompt",
    )(x, att, u, u, w_pool, pool_scale, w_out, g, b)


def _mix_sample_kernel(x_ref, att_ref, full_ref, wp_ref, ps_ref, wo_ref, g_ref, b_ref, o_ref,
                       *, ts, n_past, alpha):
    nseq = full_ref.shape[0]
    hl = MAX_WINDOW
    gd = full_ref.shape[2] // len(POOL_WINDOWS)
    pos = n_past + lax.broadcasted_iota(jnp.int32, (1, ts, 1), 1)
    pooled = []
    for g, w in enumerate(POOL_WINDOWS):
        lanes = slice(g * gd, (g + 1) * gd)
        tok = full_ref[:, hl:hl + ts, lanes]
        acc = tok
        for k in range(1, w):
            acc = acc + full_ref[:, hl - k:hl - k + ts, lanes]
        cnt = jnp.minimum(pos + 1, w).astype(F32)
        pooled.append((acc / cnt - tok).reshape(nseq * ts, gd))
    o_ref[...] = _mix_tail(pooled, att_ref[...], x_ref[...], wp_ref, ps_ref, wo_ref, g_ref, b_ref, alpha)


def _mix_sample(x2, att2, full, w_pool, pool_scale, w_out, g, b, alpha, ts, n_past):
    n, d = x2.shape
    kern = functools.partial(_mix_sample_kernel, ts=ts, n_past=n_past, alpha=alpha)
    return pl.pallas_call(
        kern,
        out_shape=jax.ShapeDtypeStruct((n, d), F32),
        compiler_params=pltpu.CompilerParams(vmem_limit_bytes=VMEM_LIMIT),
        name="mix_sample",
    )(x2, att2, full, w_pool, pool_scale, w_out, g, b)


def _shift_rows(h, s):
    n8, c = h.shape[0] // SUBLANES, h.shape[1]
    r = pltpu.roll(h.reshape(n8, SUBLANES, c), s, 1)
    sub = lax.broadcasted_iota(jnp.int32, (1, SUBLANES, c), 1)
    return jnp.where(sub < s, r[:-1], r[1:]).reshape((n8 - 1) * SUBLANES, c)


def _ffn_chunk(hg3, hv3, wc_ref, bc_ref, gcols, vcols):
    def conv(h3, cols, scale):
        hc = bc_ref[:, cols] * scale + h3[0] * (wc_ref[0:1, cols] * scale)
        for j in range(1, CONV_WIDTH):
            hc = hc + h3[j] * (wc_ref[j:j + 1, cols] * scale)
        return hc
    g = conv(hg3, gcols, 1.0)
    vh = conv(hv3, vcols, 0.5)
    t = jnp.tanh(g * (GELU_C1 + (GELU_C1 * GELU_C0) * (g * g)))
    return ((g * (1.0 + t)) * vh).astype(BF)


def _ffn_prompt_kernel(x_ref, halo_ref, wu_ref, bu_ref, wc_ref, bc_ref, wd_ref, g_ref, b_ref,
                       o_ref, tail_ref, *, tm, d_ff, chunk, alpha):
    t = pl.program_id(1)
    hl = SUBLANES
    x = x_ref[0]
    xin = jnp.concatenate([halo_ref[0], x], axis=0).astype(BF)
    n_chunks = d_ff // chunk
    cols_of = lambda c: (slice(c * chunk, (c + 1) * chunk), slice(d_ff + c * chunk, d_ff + (c + 1) * chunk))

    def up(c):
        return [_dot(xin, wu_ref[:, cols]) for cols in cols_of(c)]

    started = (t > 0).astype(F32)

    def shifted(hd, cols):
        body = hd[hl:] + bu_ref[:, cols]
        tail_ref[0, :, cols] = body[tm - hl:tm]
        h = jnp.concatenate([(hd[0:hl] + bu_ref[:, cols]) * started, body], axis=0)
        return [_shift_rows(h, CONV_WIDTH - 1 - j) for j in range(CONV_WIDTH - 1)] + [body]

    acc = jnp.zeros((tm, x.shape[1]), F32)
    nxt = up(0)
    for c in range(n_chunks):
        cur = nxt
        if c + 1 < n_chunks:
            nxt = up(c + 1)
        gcols, vcols = cols_of(c)
        act = _ffn_chunk(shifted(cur[0], gcols), shifted(cur[1], vcols), wc_ref, bc_ref, gcols, vcols)
        acc = acc + _dot(act, wd_ref[gcols, :])
    o_ref[0] = _layer_norm(alpha * x + acc, g_ref[...], b_ref[...])


def _ffn_prompt(x1, w_up, b_up, w_conv, b_conv, w_down, g, b, alpha, tm, chunk):
    bsz, t, d = x1.shape
    d_ff = w_down.shape[0]
    hl = SUBLANES
    kern = functools.partial(_ffn_prompt_kernel, tm=tm, d_ff=d_ff, chunk=chunk, alpha=alpha)
    row = pl.BlockSpec((1, tm, d), lambda i, j: (i, j, 0))
    halo = pl.BlockSpec((1, hl, d), lambda i, j: (i, jnp.maximum(j * (tm // hl) - 1, 0), 0))
    return pl.pallas_call(
        kern,
        grid=(bsz, t // tm),
        in_specs=[row, halo, _const_spec((d, 2 * d_ff)), _const_spec((1, 2 * d_ff)),
                  _const_spec((CONV_WIDTH, 2 * d_ff)), _const_spec((1, 2 * d_ff)),
                  _const_spec((d_ff, d)), _const_spec((1, d)), _const_spec((1, d))],
        out_specs=[row, pl.BlockSpec((1, hl, 2 * d_ff), lambda i, j: (i, 0, 0))],
        out_shape=[jax.ShapeDtypeStruct((bsz, t, d), F32),
                   jax.ShapeDtypeStruct((bsz, hl, 2 * d_ff), F32)],
        compiler_params=_cparams(("parallel", "arbitrary")),
        name="ffn_prompt",
    )(x1, x1, w_up, b_up, w_conv, b_conv, w_down, g, b)


def _ffn_sample_kernel(x_ref, st_ref, wu_ref, bu_ref, wc_ref, bc_ref, wd_ref, g_ref, b_ref,
                       o_ref, h_ref, buf_ref, *, ts, d_ff, chunk, alpha):
    nseq = st_ref.shape[0]
    n = nseq * ts
    hl = SUBLANES
    nst = CONV_WIDTH - 1
    x = x_ref[...]
    xb = x.astype(BF)
    acc = jnp.zeros((n, x.shape[1]), F32)
    for c in range(d_ff // chunk):
        gcols = slice(c * chunk, (c + 1) * chunk)
        vcols = slice(d_ff + c * chunk, d_ff + (c + 1) * chunk)
        h3 = []
        for cols in (gcols, vcols):
            h = _dot(xb, wu_ref[:, cols]) + bu_ref[:, cols]
            h_ref[:, cols] = h
            buf_ref[:, hl - nst:hl, :] = st_ref[:, :, cols]
            buf_ref[:, hl:hl + ts, :] = h.reshape(nseq, ts, chunk)
            h3.append([buf_ref[:, hl - 2 + j:hl - 2 + j + ts, :].reshape(n, chunk)
                       for j in range(CONV_WIDTH)])
        act = _ffn_chunk(h3[0], h3[1], wc_ref, bc_ref, gcols, vcols)
        acc = acc + _dot(act, wd_ref[gcols, :])
    o_ref[...] = _layer_norm(alpha * x + acc, g_ref[...], b_ref[...])


def _ffn_sample(x2, state, w_up, b_up, w_conv, b_conv, w_down, g, b, alpha, ts, chunk):
    n, d = x2.shape
    d_ff = w_down.shape[0]
    nseq = n // ts
    kern = functools.partial(_ffn_sample_kernel, ts=ts, d_ff=d_ff, chunk=chunk, alpha=alpha)
    return pl.pallas_call(
        kern,
        out_shape=[jax.ShapeDtypeStruct((n, d), F32), jax.ShapeDtypeStruct((n, 2 * d_ff), F32)],
        scratch_shapes=[pltpu.VMEM((nseq, SUBLANES + ts, chunk), F32)],
        compiler_params=pltpu.CompilerParams(vmem_limit_bytes=VMEM_LIMIT),
        name="ffn_sample",
    )(x2, state, w_up, b_up, w_conv, b_conv, w_down, g, b)


def _ffn_chunk_size(d_ff):
    for c in (512, 256, 128):
        if d_ff % c == 0:
            return c
    raise ValueError("d_ff must be a multiple of 128")


def kernel(x_prompt, x_sample, cache_k, cache_v, cache_logf, state_pool, state_conv, page_table,
           w_in, b_f, w_pool, pool_scale, w_out, ln1_g, ln1_b, w_up, b_up, w_conv, b_conv, w_down,
           ln2_g, ln2_b):
    bsz, t_p, d = x_prompt.shape
    db, t_s, _ = x_sample.shape
    depth = w_in.shape[0]
    n_heads = cache_k.shape[3]
    att_w = n_heads * HEAD_DIM
    pool_w = state_pool.shape[3]
    d_ff = w_down.shape[1]
    n_past = page_table.shape[1] * PAGE_SIZE
    alpha = (2 * depth) ** 0.25
    assert HEAD_DIM * 2 == LANES and n_heads % 2 == 0 and n_heads <= SUBLANES
    assert t_s == SUBLANES and cache_k.shape[2] == PAGE_SIZE and t_p >= MAX_WINDOW
    assert w_in.shape[2] == 3 * att_w + n_heads + pool_w

    tm_proj = min(512, t_p)
    tq = min(256, t_p)
    tm_mix = min(256, t_p)
    tm_ffn = min(256, t_p)
    pages = next(p for p in (16, 8, 4, 2, 1) if page_table.shape[1] % p == 0)
    chunk = _ffn_chunk_size(d_ff)

    yp, ys = x_prompt, x_sample
    outs = [[] for _ in range(10)]
    for l in range(depth):
        wi = w_in[l]
        fcols = wi[:, 3 * att_w:3 * att_w + n_heads]
        w_row = jnp.concatenate(
            [wi[:, :2 * att_w], wi[:, 3 * att_w + n_heads:], jnp.pad(fcols, ((0, 0), (0, LANES - n_heads)))],
            axis=1).astype(BF)
        w_t = jnp.concatenate(
            [wi[:, :3 * att_w].T, jnp.pad(fcols.T, ((0, BF_ROWS - n_heads), (0, 0)))],
            axis=0).astype(BF)
        bf_row = jnp.pad(b_f[l], (0, LANES - n_heads)).reshape(1, LANES)
        bf_col = b_f[l].reshape(n_heads, 1)
        wp_b = w_pool[l].astype(BF)
        ps = pool_scale[l].reshape(1, pool_w)
        wo_b = w_out[l].astype(BF)
        g1, b1 = ln1_g[l].reshape(1, d), ln1_b[l].reshape(1, d)
        wu_b = w_up[l].astype(BF)
        bu = b_up[l].reshape(1, 2 * d_ff)
        wc = w_conv[l]
        bc = b_conv[l].reshape(1, 2 * d_ff)
        wd_b = w_down[l].astype(BF)
        g2, b2 = ln2_g[l].reshape(1, d), ln2_b[l].reshape(1, d)

        qt, kb, kt, vt, vtb, u, lf, lft = _proj_prompt(yp, w_row, w_t, bf_row, bf_col, att_w, pool_w,
                                                       n_heads, tm_proj)
        att = _attn(qt, kb, vtb, lf, lft, tq)
        x1 = _mix_prompt(yp, att, u, wp_b, ps, wo_b, g1, b1, alpha, tm_mix)
        yp, tail = _ffn_prompt(x1, wu_b, bu, wc, bc, wd_b, g2, b2, alpha, tm_ffn, chunk)
        to_bthd = lambda a: jnp.transpose(a.reshape(bsz, n_heads, HEAD_DIM, t_p), (0, 3, 1, 2))
        outs[0].append(to_bthd(kt))
        outs[1].append(to_bthd(vt))
        outs[2].append(jnp.swapaxes(lft, 1, 2))
        outs[3].append(u[:, t_p - (MAX_WINDOW - 1):, :])
        outs[4].append(tail[:, SUBLANES - (CONV_WIDTH - 1):, :])

        n_s = db * t_s
        qs, ks, vs, kbs, vbs, us, lfs = _proj_sample(ys.reshape(n_s, d), w_row, w_t, bf_row, att_w, pool_w)
        lfs3 = lfs.reshape(db, t_s, LANES)
        lfts = jnp.pad(jnp.swapaxes(lfs3[:, :, :n_heads], 1, 2), ((0, 0), (0, 0), (0, LANES - t_s)))
        padk = lambda a: jnp.pad(a.reshape(db, t_s, att_w), ((0, 0), (0, PAGE_SIZE - t_s), (0, 0)))
        ckt = jnp.transpose(cache_k[l], (0, 2, 3, 1))
        cvt = jnp.transpose(cache_v[l], (0, 2, 3, 1))
        clft = jnp.swapaxes(cache_logf[l], 1, 2)
        att_s = _paged_attn(page_table, qs.reshape(db, t_s, att_w), padk(kbs), padk(vbs), lfs3, lfts,
                            ckt, cvt, clft, pages)
        us3 = us.reshape(db, t_s, pool_w)
        full = jnp.concatenate([jnp.zeros((db, 1, pool_w), F32), state_pool[l], us3], axis=1)
        x1s = _mix_sample(ys.reshape(n_s, d), att_s.reshape(n_s, att_w), full, wp_b, ps, wo_b, g1, b1,
                          alpha, t_s, n_past)
        ys2, hs = _ffn_sample(x1s, state_conv[l], wu_b, bu, wc, bc, wd_b, g2, b2, alpha, t_s, chunk)
        ys = ys2.reshape(db, t_s, d)
        outs[5].append(ks.reshape(db, t_s, n_heads, HEAD_DIM))
        outs[6].append(vs.reshape(db, t_s, n_heads, HEAD_DIM))
        outs[7].append(lfs3[:, :, :n_heads])
        outs[8].append(full[:, t_s + 1:, :])
        outs[9].append(hs.reshape(db, t_s, 2 * d_ff)[:, t_s - (CONV_WIDTH - 1):, :])

    st = [jnp.stack(o, 0) for o in outs]
    return (yp, ys, st[0], st[1], st[2], st[3], st[4], st[5], st[6], st[7], st[8], st[9])
```

```python
import functools

import jax
import jax.numpy as jnp
import numpy as np
from jax import lax
from jax.experimental import pallas as pl
from jax.experimental.pallas import tpu as pltpu

BF = jnp.bfloat16
F32 = jnp.float32

HEAD_DIM = 64
PAGE_SIZE = 128
POOL_WINDOWS = (2, 4, 8, 16)
MAX_WINDOW = max(POOL_WINDOWS)
CONV_WIDTH = 3
LN_EPS = 1e-5
NEG_INF = -1e30
GELU_C1 = float(np.float32(np.sqrt(2.0 / np.pi)))
GELU_C0 = float(np.float32(0.044715))
LOG2E = float(np.log2(np.e))
LANES = 128
SUBLANES = 8
BF_ROWS = 16
VMEM_LIMIT = 56 * 1024 * 1024


def _cparams(sem):
    return pltpu.CompilerParams(dimension_semantics=sem, vmem_limit_bytes=VMEM_LIMIT)


def _const_spec(shape):
    nd = len(shape)
    return pl.BlockSpec(shape, lambda *_: (0,) * nd, pipeline_mode=pl.Buffered(1))


def _dot(a, b):
    return jnp.dot(a, b, preferred_element_type=F32)


def _dot_nt(a, b):
    return lax.dot_general(a, b, (((1,), (1,)), ((), ())), preferred_element_type=F32)


def _layer_norm(r, g, b):
    mu = jnp.mean(r, axis=-1, keepdims=True)
    d = r - mu
    var = jnp.mean(d * d, axis=-1, keepdims=True)
    return d * lax.rsqrt(var + LN_EPS) * g + b


def _scan_incl(x, axis, reverse=False):
    n = x.shape[axis]
    idx = lax.broadcasted_iota(jnp.int32, x.shape, axis)
    s = 1
    while s < n:
        if reverse:
            x = x + jnp.where(idx + s < n, pltpu.roll(x, n - s, axis), 0.0)
        else:
            x = x + jnp.where(idx >= s, pltpu.roll(x, s, axis), 0.0)
        s *= 2
    return x


def _proj_prompt_kernel(x_ref, wr_ref, wt_ref, bfr_ref, bfc_ref,
                        qt_ref, kb_ref, kt_ref, vt_ref, vtb_ref, u_ref, lf_ref, lft_ref,
                        *, att_w, pool_w, n_heads, scale):
    xb = x_ref[0].astype(BF)
    kb_ref[0] = _dot(xb, wr_ref[:, att_w:2 * att_w]).astype(BF)
    u_ref[0] = _dot(xb, wr_ref[:, 2 * att_w:2 * att_w + pool_w])
    z = _dot(xb, wr_ref[:, 2 * att_w + pool_w:2 * att_w + pool_w + LANES]) + bfr_ref[...]
    lf_ref[0] = jax.nn.log_sigmoid(z)
    qt_ref[0] = (_dot_nt(wt_ref[0:att_w, :], xb) * scale).astype(BF)
    kt_ref[0] = _dot_nt(wt_ref[att_w:2 * att_w, :], xb)
    vt = _dot_nt(wt_ref[2 * att_w:3 * att_w, :], xb)
    vt_ref[0] = vt
    vtb_ref[0] = vt.astype(BF)
    zt = _dot_nt(wt_ref[3 * att_w:3 * att_w + BF_ROWS, :], xb)[0:n_heads] + bfc_ref[...]
    lft_ref[0] = jax.nn.log_sigmoid(zt)


def _proj_prompt(x, w_row, w_t, bf_row, bf_col, att_w, pool_w, n_heads, tm):
    bsz, t, d = x.shape
    kern = functools.partial(_proj_prompt_kernel, att_w=att_w, pool_w=pool_w, n_heads=n_heads,
                             scale=HEAD_DIM ** -0.5 * LOG2E)
    row = lambda w: pl.BlockSpec((1, tm, w), lambda i, j: (i, j, 0))
    col = lambda h: pl.BlockSpec((1, h, tm), lambda i, j: (i, 0, j))
    return pl.pallas_call(
        kern,
        grid=(bsz, t // tm),
        in_specs=[row(d), _const_spec(w_row.shape), _const_spec(w_t.shape),
                  _const_spec((1, LANES)), _const_spec((n_heads, 1))],
        out_specs=[col(att_w), row(att_w), col(att_w), col(att_w), col(att_w), row(pool_w),
                   row(LANES), col(n_heads)],
        out_shape=[jax.ShapeDtypeStruct((bsz, att_w, t), BF),
                   jax.ShapeDtypeStruct((bsz, t, att_w), BF),
                   jax.ShapeDtypeStruct((bsz, att_w, t), F32),
                   jax.ShapeDtypeStruct((bsz, att_w, t), F32),
                   jax.ShapeDtypeStruct((bsz, att_w, t), BF),
                   jax.ShapeDtypeStruct((bsz, t, pool_w), F32),
                   jax.ShapeDtypeStruct((bsz, t, LANES), F32),
                   jax.ShapeDtypeStruct((bsz, n_heads, t), F32)],
        compiler_params=_cparams(("parallel", "parallel")),
        name="proj_prompt",
    )(x, w_row, w_t, bf_row, bf_col)


def _proj_sample_kernel(x_ref, wr_ref, wt_ref, bfr_ref, q_ref, k_ref, v_ref, kb_ref, vb_ref, u_ref, lf_ref,
                        *, att_w, pool_w, scale):
    xb = x_ref[...].astype(BF)
    q_ref[...] = (_dot(xb, wr_ref[:, 0:att_w]) * scale).astype(BF)
    u_ref[...] = _dot(xb, wr_ref[:, 2 * att_w:2 * att_w + pool_w])
    z = _dot(xb, wr_ref[:, 2 * att_w + pool_w:2 * att_w + pool_w + LANES]) + bfr_ref[...]
    lf_ref[...] = jax.nn.log_sigmoid(z)
    k = _dot(xb, wr_ref[:, att_w:2 * att_w])
    k_ref[...] = k
    kb_ref[...] = k.astype(BF)
    v = _dot_nt(xb, wt_ref[2 * att_w:3 * att_w, :])
    v_ref[...] = v
    vb_ref[...] = v.astype(BF)


def _proj_sample(x2, w_row, w_t, bf_row, att_w, pool_w):
    n, d = x2.shape
    kern = functools.partial(_proj_sample_kernel, att_w=att_w, pool_w=pool_w, scale=HEAD_DIM ** -0.5)
    return pl.pallas_call(
        kern,
        out_shape=[jax.ShapeDtypeStruct((n, att_w), BF),
                   jax.ShapeDtypeStruct((n, att_w), F32),
                   jax.ShapeDtypeStruct((n, att_w), F32),
                   jax.ShapeDtypeStruct((n, att_w), BF),
                   jax.ShapeDtypeStruct((n, att_w), BF),
                   jax.ShapeDtypeStruct((n, pool_w), F32),
                   jax.ShapeDtypeStruct((n, LANES), F32)],
        compiler_params=pltpu.CompilerParams(vmem_limit_bytes=VMEM_LIMIT),
        name="proj_sample",
    )(x2, w_row, w_t, bf_row)


KEY_CHUNK = 32


def _group_reduce(fn, x):
    parts = [x[i * SUBLANES:(i + 1) * SUBLANES] for i in range(x.shape[0] // SUBLANES)]
    while len(parts) > 1:
        parts = [fn(parts[i], parts[i + 1]) if i + 1 < len(parts) else parts[i]
                 for i in range(0, len(parts), 2)]
    return parts[0]


AUG_ROWS = 16


def _split3(x):
    hi = x.astype(BF)
    r1 = x - hi.astype(F32)
    mid = r1.astype(BF)
    lo = (r1 - mid.astype(F32)).astype(BF)
    return hi, mid, lo


def _attn_kernel(qt_ref, k_ref, vt_ref, lf_ref, lft_ref, sel_ref, one_ref, o_ref,
                 ct_ref, kx_ref, qx_ref, ml_ref, mb_ref, *per_head, tq, n_heads):
    qi = pl.program_id(1)
    tk = tq
    rc = KEY_CHUNK
    t = k_ref.shape[1]
    n_pairs = n_heads // 2
    s_refs, p_refs, acc_refs = (per_head[i * n_heads:(i + 1) * n_heads] for i in range(3))

    @pl.when(qi == 0)
    def _():
        ct_ref[...] = _scan_incl(lft_ref[0], 1) * LOG2E
        blk = 512 if t % 512 == 0 else t
        for r0 in range(0, t, blk):
            rows = slice(r0, r0 + blk)
            c = _scan_incl(lf_ref[0, rows, :], 0)
            if r0:
                c = c + tot
            tot = c[blk - 1:blk, :]
            hi, mid, lo = _split3(c * LOG2E)
            aug = _dot(jnp.concatenate([hi, mid, lo], axis=1), sel_ref[...]) + one_ref[...]
            for pr in range(n_pairs):
                kx_ref[pr, rows, 0:LANES] = k_ref[0, rows, pr * LANES:(pr + 1) * LANES]
                kx_ref[pr, rows, LANES:2 * LANES] = aug[:, pr * LANES:(pr + 1) * LANES].astype(BF)
        qx_ref[:, LANES + AUG_ROWS:, :] = jnp.zeros((n_heads, LANES - AUG_ROWS, tq), BF)

    q0 = pl.multiple_of(qi * tq, tq)
    first = lax.broadcasted_iota(jnp.int32, (LANES, 1), 0) < HEAD_DIM
    causal = (lax.broadcasted_iota(jnp.int32, (tk, tq), 0)
              <= lax.broadcasted_iota(jnp.int32, (tk, tq), 1))
    arow = lax.broadcasted_iota(jnp.int32, (AUG_ROWS, 1), 0)

    heads = range(n_heads)
    for pair in range(n_pairs):
        qtp = qt_ref[0, pair * LANES:(pair + 1) * LANES, :]
        zero = jnp.zeros_like(qtp)
        qx_ref[2 * pair, 0:LANES, :] = jnp.where(first, qtp, zero)
        qx_ref[2 * pair + 1, 0:LANES, :] = jnp.where(first, zero, qtp)
    for h in heads:
        hf = h % 2
        ci = ct_ref[h:h + 1, pl.ds(q0, tq)]
        hi = ci.astype(BF).astype(F32)
        r1 = ci - hi
        mid = r1.astype(BF).astype(F32)
        lo = r1 - mid
        pick = ((arow >= 3 * hf) & (arow < 3 * hf + 3)).astype(F32)
        blk = jnp.where(arow == 6, hi, jnp.where(arow == 7, mid, jnp.where(arow == 8, lo, pick)))
        qx_ref[h, LANES:LANES + AUG_ROWS, :] = blk.astype(BF)

    def scores(jb, masked):
        k0 = pl.multiple_of(jb * tk, tk)
        out = []
        for h in heads:
            s = _dot(kx_ref[h // 2, pl.ds(k0, tk), :], qx_ref[h])
            out.append(jnp.where(causal, s, NEG_INF) if masked else s)
        return out

    def keep(h, s):
        s_refs[h][...] = s
        mb_ref[h:h + 1, :] = jnp.max(_group_reduce(jnp.maximum, s), axis=0, keepdims=True)

    def consume(jb, nxt=None):
        k0 = pl.multiple_of(jb * tk, tk)
        m_new, alpha = [], []
        for h in heads:
            m_old = ml_ref[2 * h:2 * h + 1, :]
            m_new.append(jnp.maximum(m_old, mb_ref[h:h + 1, :]))
            alpha.append(jnp.exp2(m_old - m_new[h]))
            ml_ref[2 * h:2 * h + 1, :] = m_new[h]
        for h in heads:
            lb = jnp.zeros((SUBLANES, tq), F32)
            for r in range(tk // rc):
                p = jnp.exp2(s_refs[h][r * rc:(r + 1) * rc, :] - m_new[h])
                lb = lb + _group_reduce(jnp.add, p)
                p_refs[h][r * rc:(r + 1) * rc, :] = p.astype(BF)
            ml_ref[2 * h + 1:2 * h + 2, :] = (alpha[h] * ml_ref[2 * h + 1:2 * h + 2, :]
                                              + jnp.sum(lb, axis=0, keepdims=True))
            if nxt is not None:
                keep(h, nxt[h])
        for h in heads:
            vh = vt_ref[0, h * HEAD_DIM:(h + 1) * HEAD_DIM, pl.ds(k0, tk)]
            acc_refs[h][...] = alpha[h] * acc_refs[h][...] + _dot(vh, p_refs[h][...])

    is_max = lax.broadcasted_iota(jnp.int32, (2 * n_heads, 1), 0) % 2 == 0
    ml_ref[...] = jnp.where(is_max, jnp.full((2 * n_heads, tq), NEG_INF, F32), 0.0)
    for h in heads:
        acc_refs[h][...] = jnp.zeros((HEAD_DIM, tq), F32)

    @pl.when(qi == 0)
    def _():
        for h, s in enumerate(scores(0, True)):
            keep(h, s)

    @pl.when(qi > 0)
    def _():
        for h, s in enumerate(scores(0, False)):
            keep(h, s)

        def body(j, _):
            consume(j, scores(j + 1, False))
            return 0

        lax.fori_loop(0, qi - 1, body, 0)
        consume(qi - 1, scores(qi, True))

    consume(qi)
    for pair in range(n_heads // 2):
        h0, h1 = 2 * pair, 2 * pair + 1
        out_t = jnp.concatenate([acc_refs[h0][...] / ml_ref[2 * h0 + 1:2 * h0 + 2, :],
                                 acc_refs[h1][...] / ml_ref[2 * h1 + 1:2 * h1 + 2, :]], axis=0)
        o_ref[0, :, pair * LANES:(pair + 1) * LANES] = out_t.T.astype(o_ref.dtype)


def _attn(qt, kb, vtb, lf, lft, tq):
    b, w, t = qt.shape
    n_heads = w // HEAD_DIM
    n_pairs = n_heads // 2
    sel = np.zeros((3 * LANES, n_pairs * LANES), np.float32)
    one = np.zeros((1, n_pairs * LANES), np.float32)
    for pr in range(n_pairs):
        for hf in range(2):
            for piece in range(3):
                sel[piece * LANES + 2 * pr + hf, pr * LANES + 3 * hf + piece] = -1.0
        one[0, pr * LANES + 6:pr * LANES + 9] = 1.0
    kern = functools.partial(_attn_kernel, tq=tq, n_heads=n_heads)
    per_b = lambda s1, s2: pl.BlockSpec((1, s1, s2), lambda i, j: (i, 0, 0))
    return pl.pallas_call(
        kern,
        grid=(b, t // tq),
        in_specs=[pl.BlockSpec((1, w, tq), lambda i, j: (i, 0, j)),
                  per_b(t, w), per_b(w, t), per_b(t, LANES), per_b(n_heads, t),
                  _const_spec(sel.shape), _const_spec(one.shape)],
        out_specs=pl.BlockSpec((1, tq, w), lambda i, j: (i, j, 0)),
        out_shape=jax.ShapeDtypeStruct((b, t, w), BF),
        scratch_shapes=[pltpu.VMEM((n_heads, t), F32),
                        pltpu.VMEM((n_pairs, t, 2 * LANES), BF),
                        pltpu.VMEM((n_heads, 2 * LANES, tq), BF),
                        pltpu.VMEM((2 * n_heads, tq), F32), pltpu.VMEM((n_heads, tq), F32)]
                       + [pltpu.VMEM((tq, tq), F32)] * n_heads
                       + [pltpu.VMEM((tq, tq), BF)] * n_heads
                       + [pltpu.VMEM((HEAD_DIM, tq), F32)] * n_heads,
        compiler_params=_cparams(("parallel", "arbitrary")),
        name="attn_prompt",
    )(qt, kb, vtb, lf, lft, jnp.asarray(sel, BF), jnp.asarray(one, F32))


def _paged_kernel(pt_ref, q_ref, kn_ref, vn_ref, lf_ref, lft_ref, tri_ref, clf_ref, ck_hbm, cv_hbm,
                  o_ref, qbd_ref, ci_ref, m_ref, l_ref, acc_ref, carry_ref, kbuf, vbuf, sem,
                  *, n_heads, pages, tq):
    b = pl.program_id(0)
    g = pl.program_id(1)
    ng = pl.num_programs(1)
    step = b * ng + g
    slot = lax.rem(step, 2)
    first_page = (ng - 1 - g) * pages
    rows = n_heads * tq
    w = q_ref.shape[2]

    def page_copies(bb, gg, sl):
        first = (ng - 1 - gg) * pages
        out = []
        for i in range(pages):
            pid = pt_ref[bb, first + i]
            out.append(pltpu.make_async_copy(ck_hbm.at[pid], kbuf.at[sl, i], sem.at[sl, 0]))
            out.append(pltpu.make_async_copy(cv_hbm.at[pid], vbuf.at[sl, i], sem.at[sl, 1]))
        return out

    @pl.when(step == 0)
    def _():
        for cp in page_copies(0, 0, 0):
            cp.start()

    @pl.when(step + 1 < pl.num_programs(0) * ng)
    def _():
        last_g = g + 1 == ng
        for cp in page_copies(jnp.where(last_g, b + 1, b), jnp.where(last_g, 0, g + 1), 1 - slot):
            cp.start()

    for cp in page_copies(b, g, slot):
        cp.wait()

    def expand(x):
        return jnp.concatenate(
            [jnp.broadcast_to(x[h:h + 1, :], (tq, x.shape[1])) for h in range(n_heads)], axis=0)

    def update(s, pv_fn):
        m = m_ref[...]
        m_new = jnp.maximum(m, jnp.max(s, axis=-1, keepdims=True))
        alpha = jnp.exp(m - m_new)
        p = jnp.exp(s - m_new)
        l_ref[...] = alpha * l_ref[...] + jnp.sum(p, axis=-1, keepdims=True)
        acc_ref[...] = alpha * acc_ref[...] + pv_fn(p.astype(BF))
        m_ref[...] = m_new

    @pl.when(g == 0)
    def _():
        q = q_ref[0]
        lane = lax.broadcasted_iota(jnp.int32, (1, w), 1)
        zero = jnp.zeros_like(q)
        qbd_ref[...] = jnp.concatenate(
            [jnp.where((lane >= h * HEAD_DIM) & (lane < (h + 1) * HEAD_DIM), q, zero)
             for h in range(n_heads)], axis=0)
        c = _scan_incl(lf_ref[0], 0)
        ci_ref[...] = jnp.concatenate([c[:, h:h + 1] for h in range(n_heads)], axis=0)
        m_ref[...] = jnp.full((rows, 1), NEG_INF, F32)
        l_ref[...] = jnp.zeros((rows, 1), F32)
        acc_ref[...] = jnp.zeros((rows, w), F32)
        carry_ref[...] = jnp.zeros((n_heads, PAGE_SIZE), F32)
        cj = _scan_incl(lft_ref[0], 1)
        s = _dot_nt(qbd_ref[...], kn_ref[0]) + ci_ref[...] - expand(cj)
        r_idx = lax.broadcasted_iota(jnp.int32, (rows, LANES), 0)
        c_idx = lax.broadcasted_iota(jnp.int32, (rows, LANES), 1)
        s = jnp.where(c_idx <= jnp.bitwise_and(r_idx, tq - 1), s, NEG_INF)
        update(s, lambda p: _dot(p, vn_ref[0]))

    ktc = jnp.concatenate([kbuf[slot, i].reshape(w, PAGE_SIZE) for i in range(pages)], axis=1).astype(BF)
    vtc = jnp.concatenate([vbuf[slot, i].reshape(w, PAGE_SIZE) for i in range(pages)], axis=1).astype(BF)
    lf = jnp.concatenate([clf_ref[pt_ref[b, first_page + i]] for i in range(pages)], axis=0)
    hi = lf.astype(BF)
    rem = lf - hi.astype(F32)
    mid = rem.astype(BF)
    lo = (rem - mid.astype(F32)).astype(BF)
    y = _dot(jnp.concatenate([hi, mid, lo], axis=0), tri_ref[...])
    ph = pages * n_heads
    y = y[0:ph] + y[ph:2 * ph] + y[2 * ph:3 * ph]
    run = carry_ref[...]
    cjs = [None] * pages
    for pg in reversed(range(pages)):
        blk = slice(pg * n_heads, (pg + 1) * n_heads)
        cjs[pg] = -(run + (y[blk, 0:PAGE_SIZE] - lf[blk]))
        run = run + y[blk, PAGE_SIZE:2 * PAGE_SIZE]
    carry_ref[...] = run
    cj = jnp.concatenate(cjs, axis=1)
    s = _dot(qbd_ref[...], ktc) + ci_ref[...] - expand(cj)
    update(s, lambda p: _dot_nt(p, vtc))

    @pl.when(g == pl.num_programs(1) - 1)
    def _():
        lane = lax.broadcasted_iota(jnp.int32, (1, w), 1)
        out = jnp.zeros((tq, w), F32)
        for h in range(n_heads):
            sl = slice(h * tq, (h + 1) * tq)
            val = acc_ref[sl, :] / l_ref[sl, :]
            out = jnp.where((lane >= h * HEAD_DIM) & (lane < (h + 1) * HEAD_DIM), val, out)
        o_ref[0] = out.astype(o_ref.dtype)


def _paged_attn(page_table, q, kn, vn, lf, lft, cache_kt, cache_vt, cache_lft, pages):
    db, tq, w = q.shape
    n_heads = w // HEAD_DIM
    n_pages = page_table.shape[1]
    ng = n_pages // pages
    rows = n_heads * tq
    assert tq & (tq - 1) == 0

    per_b = lambda s1, s2: pl.BlockSpec((1, s1, s2), lambda b, g, pt: (b, 0, 0))
    in_specs = [per_b(tq, w), per_b(PAGE_SIZE, w), per_b(PAGE_SIZE, w), per_b(tq, LANES), per_b(n_heads, LANES),
                _const_spec((PAGE_SIZE, 2 * PAGE_SIZE)), _const_spec(cache_lft.shape),
                pl.BlockSpec(memory_space=pl.ANY), pl.BlockSpec(memory_space=pl.ANY)]
    page_buf = pltpu.VMEM((2, pages, n_heads, HEAD_DIM, PAGE_SIZE), cache_kt.dtype)
    kern = functools.partial(_paged_kernel, n_heads=n_heads, pages=pages, tq=tq)
    grid_spec = pltpu.PrefetchScalarGridSpec(
        num_scalar_prefetch=1,
        grid=(db, ng),
        in_specs=in_specs,
        out_specs=per_b(tq, w),
        scratch_shapes=[pltpu.VMEM((rows, w), BF), pltpu.VMEM((rows, 1), F32),
                        pltpu.VMEM((rows, 1), F32), pltpu.VMEM((rows, 1), F32),
                        pltpu.VMEM((rows, w), F32), pltpu.VMEM((n_heads, PAGE_SIZE), F32),
                        page_buf, page_buf, pltpu.SemaphoreType.DMA((2, 2))])
    pos = lax.broadcasted_iota(jnp.int32, (PAGE_SIZE, PAGE_SIZE), 0)
    col = lax.broadcasted_iota(jnp.int32, (PAGE_SIZE, PAGE_SIZE), 1)
    tri = jnp.concatenate([(pos >= col).astype(BF), jnp.ones((PAGE_SIZE, PAGE_SIZE), BF)], axis=1)
    return pl.pallas_call(
        kern,
        grid_spec=grid_spec,
        out_shape=jax.ShapeDtypeStruct((db, tq, w), BF),
        compiler_params=_cparams(("arbitrary", "arbitrary")),
        name="attn_paged",
    )(page_table, q, kn, vn, lf, lft, tri, cache_lft, cache_kt, cache_vt)


def _mix_tail(pooled, att, x, wp_ref, ps_ref, wo_ref, g_ref, b_ref, alpha):
    gd = pooled[0].shape[1]
    att_w = att.shape[1]
    mixed = [_dot(pooled[g].astype(BF), wp_ref[g]) * ps_ref[:, g * gd:(g + 1) * gd]
             for g in range(len(pooled))]
    pool = jnp.concatenate(mixed, axis=1).astype(BF)
    mix = _dot(att, wo_ref[0:att_w, :]) + _dot(pool, wo_ref[att_w:, :])
    return _layer_norm(alpha * x + mix, g_ref[...], b_ref[...])


def _shift_down(x, s):
    n8, c = x.shape[0] // SUBLANES, x.shape[1]
    if s == SUBLANES:
        return jnp.concatenate([x[0:SUBLANES], x[:-SUBLANES]], axis=0)
    r = pltpu.roll(x.reshape(n8, SUBLANES, c), s, 1)
    prev = jnp.concatenate([r[0:1], r[:-1]], axis=0)
    sub = lax.broadcasted_iota(jnp.int32, (1, SUBLANES, c), 1)
    return jnp.where(sub < s, prev, r).reshape(n8 * SUBLANES, c)


def _mix_prompt_kernel(x_ref, att_ref, u_ref, halo_ref, wp_ref, ps_ref, wo_ref, g_ref, b_ref,
                       o_ref, *, tm, alpha):
    t = pl.program_id(1)
    hl = MAX_WINDOW
    started = (t > 0).astype(F32)
    pos = t * tm + lax.broadcasted_iota(jnp.int32, (tm, 1), 0)
    gd = u_ref.shape[2] // len(POOL_WINDOWS)
    pooled = []
    for g, w in enumerate(POOL_WINDOWS):
        lanes = slice(g * gd, (g + 1) * gd)
        tok = u_ref[0, :, lanes]
        win = jnp.concatenate([halo_ref[0, :, lanes] * started, tok], axis=0)
        k = 1
        while k < w:
            win = win + _shift_down(win, k)
            k *= 2
        cnt = jnp.minimum(pos + 1, w).astype(F32)
        pooled.append(win[hl:] / cnt - tok)
    o_ref[0] = _mix_tail(pooled, att_ref[0], x_ref[0], wp_ref, ps_ref, wo_ref, g_ref, b_ref, alpha)


def _mix_prompt(x, att, u, w_pool, pool_scale, w_out, g, b, alpha, tm):
    bsz, t, d = x.shape
    att_w, pool_w = att.shape[2], u.shape[2]
    hl = MAX_WINDOW
    ng, gd = w_pool.shape[0], w_pool.shape[1]
    kern = functools.partial(_mix_prompt_kernel, tm=tm, alpha=alpha)
    row = lambda w: pl.BlockSpec((1, tm, w), lambda i, j: (i, j, 0))
    halo = pl.BlockSpec((1, hl, pool_w), lambda i, j: (i, jnp.maximum(j * (tm // hl) - 1, 0), 0))
    return pl.pallas_call(
        kern,
        grid=(bsz, t // tm),
        in_specs=[row(d), row(att_w), row(pool_w), halo,
                  _const_spec((ng, gd, gd)), _const_spec((1, pool_w)), _const_spec((att_w + pool_w, d)),
                  _const_spec((1, d)), _const_spec((1, d))],
        out_specs=row(d),
        out_shape=jax.ShapeDtypeStruct((bsz, t, d), F32),
        compiler_params=_cparams(("parallel", "parallel")),
        name="mix_prompt",
    )(x, att, u, u, w_pool, pool_scale, w_out, g, b)


def _mix_sample_kernel(x_ref, att_ref, full_ref, wp_ref, ps_ref, wo_ref, g_ref, b_ref, o_ref,
                       *, ts, n_past, alpha):
    nseq = full_ref.shape[0]
    hl = MAX_WINDOW
    gd = full_ref.shape[2] // len(POOL_WINDOWS)
    pos = n_past + lax.broadcasted_iota(jnp.int32, (1, ts, 1), 1)
    pooled = []
    for g, w in enumerate(POOL_WINDOWS):
        lanes = slice(g * gd, (g + 1) * gd)
        tok = full_ref[:, hl:hl + ts, lanes]
        acc = tok
        for k in range(1, w):
            acc = acc + full_ref[:, hl - k:hl - k + ts, lanes]
        cnt = jnp.minimum(pos + 1, w).astype(F32)
        pooled.append((acc / cnt - tok).reshape(nseq * ts, gd))
    o_ref[...] = _mix_tail(pooled, att_ref[...], x_ref[...], wp_ref, ps_ref, wo_ref, g_ref, b_ref, alpha)


def _mix_sample(x2, att2, full, w_pool, pool_scale, w_out, g, b, alpha, ts, n_past):
    n, d = x2.shape
    kern = functools.partial(_mix_sample_kernel, ts=ts, n_past=n_past, alpha=alpha)
    return pl.pallas_call(
        kern,
        out_shape=jax.ShapeDtypeStruct((n, d), F32),
        compiler_params=pltpu.CompilerParams(vmem_limit_bytes=VMEM_LIMIT),
        name="mix_sample",
    )(x2, att2, full, w_pool, pool_scale, w_out, g, b)


def _shift_rows(h, s):
    n8, c = h.shape[0] // SUBLANES, h.shape[1]
    r = pltpu.roll(h.reshape(n8, SUBLANES, c), s, 1)
    sub = lax.broadcasted_iota(jnp.int32, (1, SUBLANES, c), 1)
    return jnp.where(sub < s, r[:-1], r[1:]).reshape((n8 - 1) * SUBLANES, c)


def _ffn_chunk(hg3, hv3, wc_ref, bc_ref, gcols, vcols):
    def conv(h3, cols, scale):
        hc = bc_ref[:, cols] * scale + h3[0] * (wc_ref[0:1, cols] * scale)
        for j in range(1, CONV_WIDTH):
            hc = hc + h3[j] * (wc_ref[j:j + 1, cols] * scale)
        return hc
    g = conv(hg3, gcols, 1.0)
    vh = conv(hv3, vcols, 0.5)
    t = jnp.tanh(g * (GELU_C1 + (GELU_C1 * GELU_C0) * (g * g)))
    return ((g * (1.0 + t)) * vh).astype(BF)


def _ffn_prompt_kernel(x_ref, halo_ref, wu_ref, bu_ref, wc_ref, bc_ref, wd_ref, g_ref, b_ref,
                       o_ref, tail_ref, *, tm, d_ff, chunk, alpha):
    t = pl.program_id(1)
    hl = SUBLANES
    x = x_ref[0]
    xin = jnp.concatenate([halo_ref[0], x], axis=0).astype(BF)
    n_chunks = d_ff // chunk
    cols_of = lambda c: (slice(c * chunk, (c + 1) * chunk), slice(d_ff + c * chunk, d_ff + (c + 1) * chunk))

    def up(c):
        return [_dot(xin, wu_ref[:, cols]) for cols in cols_of(c)]

    started = (t > 0).astype(F32)

    def shifted(hd, cols):
        body = hd[hl:] + bu_ref[:, cols]
        tail_ref[0, :, cols] = body[tm - hl:tm]
        h = jnp.concatenate([(hd[0:hl] + bu_ref[:, cols]) * started, body], axis=0)
        return [_shift_rows(h, CONV_WIDTH - 1 - j) for j in range(CONV_WIDTH - 1)] + [body]

    acc = jnp.zeros((tm, x.shape[1]), F32)
    nxt = up(0)
    for c in range(n_chunks):
        cur = nxt
        if c + 1 < n_chunks:
            nxt = up(c + 1)
        gcols, vcols = cols_of(c)
        act = _ffn_chunk(shifted(cur[0], gcols), shifted(cur[1], vcols), wc_ref, bc_ref, gcols, vcols)
        acc = acc + _dot(act, wd_ref[gcols, :])
    o_ref[0] = _layer_norm(alpha * x + acc, g_ref[...], b_ref[...])


def _ffn_prompt(x1, w_up, b_up, w_conv, b_conv, w_down, g, b, alpha, tm, chunk):
    bsz, t, d = x1.shape
    d_ff = w_down.shape[0]
    hl = SUBLANES
    kern = functools.partial(_ffn_prompt_kernel, tm=tm, d_ff=d_ff, chunk=chunk, alpha=alpha)
    row = pl.BlockSpec((1, tm, d), lambda i, j: (i, j, 0))
    halo = pl.BlockSpec((1, hl, d), lambda i, j: (i, jnp.maximum(j * (tm // hl) - 1, 0), 0))
    return pl.pallas_call(
        kern,
        grid=(bsz, t // tm),
        in_specs=[row, halo, _const_spec((d, 2 * d_ff)), _const_spec((1, 2 * d_ff)),
                  _const_spec((CONV_WIDTH, 2 * d_ff)), _const_spec((1, 2 * d_ff)),
                  _const_spec((d_ff, d)), _const_spec((1, d)), _const_spec((1, d))],
        out_specs=[row, pl.BlockSpec((1, hl, 2 * d_ff), lambda i, j: (i, 0, 0))],
        out_shape=[jax.ShapeDtypeStruct((bsz, t, d), F32),
                   jax.ShapeDtypeStruct((bsz, hl, 2 * d_ff), F32)],
        compiler_params=_cparams(("parallel", "arbitrary")),
        name="ffn_prompt",
    )(x1, x1, w_up, b_up, w_conv, b_conv, w_down, g, b)


def _ffn_sample_kernel(x_ref, st_ref, wu_ref, bu_ref, wc_ref, bc_ref, wd_ref, g_ref, b_ref,
                       o_ref, h_ref, buf_ref, *, ts, d_ff, chunk, alpha):
    nseq = st_ref.shape[0]
    n = nseq * ts
    hl = SUBLANES
    nst = CONV_WIDTH - 1
    x = x_ref[...]
    xb = x.astype(BF)
    acc = jnp.zeros((n, x.shape[1]), F32)
    for c in range(d_ff // chunk):
        gcols = slice(c * chunk, (c + 1) * chunk)
        vcols = slice(d_ff + c * chunk, d_ff + (c + 1) * chunk)
        h3 = []
        for cols in (gcols, vcols):
            h = _dot(xb, wu_ref[:, cols]) + bu_ref[:, cols]
            h_ref[:, cols] = h
            buf_ref[:, hl - nst:hl, :] = st_ref[:, :, cols]
            buf_ref[:, hl:hl + ts, :] = h.reshape(nseq, ts, chunk)
            h3.append([buf_ref[:, hl - 2 + j:hl - 2 + j + ts, :].reshape(n, chunk)
                       for j in range(CONV_WIDTH)])
        act = _ffn_chunk(h3[0], h3[1], wc_ref, bc_ref, gcols, vcols)
        acc = acc + _dot(act, wd_ref[gcols, :])
    o_ref[...] = _layer_norm(alpha * x + acc, g_ref[...], b_ref[...])


def _ffn_sample(x2, state, w_up, b_up, w_conv, b_conv, w_down, g, b, alpha, ts, chunk):
    n, d = x2.shape
    d_ff = w_down.shape[0]
    nseq = n // ts
    kern = functools.partial(_ffn_sample_kernel, ts=ts, d_ff=d_ff, chunk=chunk, alpha=alpha)
    return pl.pallas_call(
        kern,
        out_shape=[jax.ShapeDtypeStruct((n, d), F32), jax.ShapeDtypeStruct((n, 2 * d_ff), F32)],
        scratch_shapes=[pltpu.VMEM((nseq, SUBLANES + ts, chunk), F32)],
        compiler_params=pltpu.CompilerParams(vmem_limit_bytes=VMEM_LIMIT),
        name="ffn_sample",
    )(x2, state, w_up, b_up, w_conv, b_conv, w_down, g, b)


def _ffn_chunk_size(d_ff):
    for c in (512, 256, 128):
        if d_ff % c == 0:
            return c
    raise ValueError("d_ff must be a multiple of 128")


def kernel(x_prompt, x_sample, cache_k, cache_v, cache_logf, state_pool, state_conv, page_table,
           w_in, b_f, w_pool, pool_scale, w_out, ln1_g, ln1_b, w_up, b_up, w_conv, b_conv, w_down,
           ln2_g, ln2_b):
    bsz, t_p, d = x_prompt.shape
    db, t_s, _ = x_sample.shape
    depth = w_in.shape[0]
    n_heads = cache_k.shape[3]
    att_w = n_heads * HEAD_DIM
    pool_w = state_pool.shape[3]
    d_ff = w_down.shape[1]
    n_past = page_table.shape[1] * PAGE_SIZE
    alpha = (2 * depth) ** 0.25
    assert HEAD_DIM * 2 == LANES and n_heads % 2 == 0 and n_heads <= SUBLANES
    assert t_s == SUBLANES and cache_k.shape[2] == PAGE_SIZE and t_p >= MAX_WINDOW
    assert w_in.shape[2] == 3 * att_w + n_heads + pool_w

    tm_proj = min(512, t_p)
    tq = min(256, t_p)
    tm_mix = min(256, t_p)
    tm_ffn = min(256, t_p)
    pages = next(p for p in (16, 8, 4, 2, 1) if page_table.shape[1] % p == 0)
    chunk = _ffn_chunk_size(d_ff)

    yp, ys = x_prompt, x_sample
    outs = [[] for _ in range(10)]
    for l in range(depth):
        wi = w_in[l]
        fcols = wi[:, 3 * att_w:3 * att_w + n_heads]
        w_row = jnp.concatenate(
            [wi[:, :2 * att_w], wi[:, 3 * att_w + n_heads:], jnp.pad(fcols, ((0, 0), (0, LANES - n_heads)))],
            axis=1).astype(BF)
        w_t = jnp.concatenate(
            [wi[:, :3 * att_w].T, jnp.pad(fcols.T, ((0, BF_ROWS - n_heads), (0, 0)))],
            axis=0).astype(BF)
        bf_row = jnp.pad(b_f[l], (0, LANES - n_heads)).reshape(1, LANES)
        bf_col = b_f[l].reshape(n_heads, 1)
        wp_b = w_pool[l].astype(BF)
        ps = pool_scale[l].reshape(1, pool_w)
        wo_b = w_out[l].astype(BF)
        g1, b1 = ln1_g[l].reshape(1, d), ln1_b[l].reshape(1, d)
        wu_b = w_up[l].astype(BF)
        bu = b_up[l].reshape(1, 2 * d_ff)
        wc = w_conv[l]
        bc = b_conv[l].reshape(1, 2 * d_ff)
        wd_b = w_down[l].astype(BF)
        g2, b2 = ln2_g[l].reshape(1, d), ln2_b[l].reshape(1, d)

        qt, kb, kt, vt, vtb, u, lf, lft = _proj_prompt(yp, w_row, w_t, bf_row, bf_col, att_w, pool_w,
                                                       n_heads, tm_proj)
        att = _attn(qt, kb, vtb, lf, lft, tq)
        x1 = _mix_prompt(yp, att, u, wp_b, ps, wo_b, g1, b1, alpha, tm_mix)
        yp, tail = _ffn_prompt(x1, wu_b, bu, wc, bc, wd_b, g2, b2, alpha, tm_ffn, chunk)
        to_bthd = lambda a: jnp.transpose(a.reshape(bsz, n_heads, HEAD_DIM, t_p), (0, 3, 1, 2))
        outs[0].append(to_bthd(kt))
        outs[1].append(to_bthd(vt))
        outs[2].append(jnp.swapaxes(lft, 1, 2))
        outs[3].append(u[:, t_p - (MAX_WINDOW - 1):, :])
        outs[4].append(tail[:, SUBLANES - (CONV_WIDTH - 1):, :])

        n_s = db * t_s
        qs, ks, vs, kbs, vbs, us, lfs = _proj_sample(ys.reshape(n_s, d), w_row, w_t, bf_row, att_w, pool_w)
        lfs3 = lfs.reshape(db, t_s, LANES)
        lfts = jnp.pad(jnp.swapaxes(lfs3[:, :, :n_heads], 1, 2), ((0, 0), (0, 0), (0, LANES - t_s)))
        padk = lambda a: jnp.pad(a.reshape(db, t_s, att_w), ((0, 0), (0, PAGE_SIZE - t_s), (0, 0)))
        ckt = jnp.transpose(cache_k[l], (0, 2, 3, 1))
        cvt = jnp.transpose(cache_v[l], (0, 2, 3, 1))
        clft = jnp.swapaxes(cache_logf[l], 1, 2)
        att_s = _paged_attn(page_table, qs.reshape(db, t_s, att_w), padk(kbs), padk(vbs), lfs3, lfts,
                            ckt, cvt, clft, pages)
        us3 = us.reshape(db, t_s, pool_w)
        full = jnp.concatenate([jnp.zeros((db, 1, pool_w), F32), state_pool[l], us3], axis=1)
        x1s = _mix_sample(ys.reshape(n_s, d), att_s.reshape(n_s, att_w), full, wp_b, ps, wo_b, g1, b1,
                          alpha, t_s, n_past)
        ys2, hs = _ffn_sample(x1s, state_conv[l], wu_b, bu, wc, bc, wd_b, g2, b2, alpha, t_s, chunk)
        ys = ys2.reshape(db, t_s, d)
        outs[5].append(ks.reshape(db, t_s, n_heads, HEAD_DIM))
        outs[6].append(vs.reshape(db, t_s, n_heads, HEAD_DIM))
        outs[7].append(lfs3[:, :, :n_heads])
        outs[8].append(full[:, t_s + 1:, :])
        outs[9].append(hs.reshape(db, t_s, 2 * d_ff)[:, t_s - (CONV_WIDTH - 1):, :])

    st = [jnp.stack(o, 0) for o in outs]
    return (yp, ys, st[0], st[1], st[2], st[3], st[4], st[5], st[6], st[7], st[8], st[9])
```

```python
import functools

import jax
import jax.numpy as jnp
import numpy as np
from jax import lax
from jax.experimental import pallas as pl
from jax.experimental.pallas import tpu as pltpu

BF = jnp.bfloat16
F32 = jnp.float32

HEAD_DIM = 64
PAGE_SIZE = 128
POOL_WINDOWS = (2, 4, 8, 16)
MAX_WINDOW = max(POOL_WINDOWS)
CONV_WIDTH = 3
LN_EPS = 1e-5
NEG_INF = -1e30
GELU_C1 = float(np.float32(np.sqrt(2.0 / np.pi)))
GELU_C0 = float(np.float32(0.044715))
LOG2E = float(np.log2(np.e))
LANES = 128
SUBLANES = 8
BF_ROWS = 16
VMEM_LIMIT = 56 * 1024 * 1024


def _cparams(sem):
    return pltpu.CompilerParams(dimension_semantics=sem, vmem_limit_bytes=VMEM_LIMIT)


def _const_spec(shape):
    nd = len(shape)
    return pl.BlockSpec(shape, lambda *_: (0,) * nd, pipeline_mode=pl.Buffered(1))


def _dot(a, b):
    return jnp.dot(a, b, preferred_element_type=F32)


def _dot_nt(a, b):
    return lax.dot_general(a, b, (((1,), (1,)), ((), ())), preferred_element_type=F32)


def _layer_norm(r, g, b):
    mu = jnp.mean(r, axis=-1, keepdims=True)
    d = r - mu
    var = jnp.mean(d * d, axis=-1, keepdims=True)
    return d * lax.rsqrt(var + LN_EPS) * g + b


def _scan_incl(x, axis, reverse=False):
    n = x.shape[axis]
    idx = lax.broadcasted_iota(jnp.int32, x.shape, axis)
    s = 1
    while s < n:
        if reverse:
            x = x + jnp.where(idx + s < n, pltpu.roll(x, n - s, axis), 0.0)
        else:
            x = x + jnp.where(idx >= s, pltpu.roll(x, s, axis), 0.0)
        s *= 2
    return x


def _proj_prompt_kernel(x_ref, wr_ref, wt_ref, bfr_ref, bfc_ref,
                        qt_ref, kb_ref, kt_ref, vt_ref, vtb_ref, u_ref, lf_ref, lft_ref,
                        *, att_w, pool_w, n_heads, scale):
    xb = x_ref[0].astype(BF)
    kb_ref[0] = _dot(xb, wr_ref[:, att_w:2 * att_w]).astype(BF)
    u_ref[0] = _dot(xb, wr_ref[:, 2 * att_w:2 * att_w + pool_w])
    z = _dot(xb, wr_ref[:, 2 * att_w + pool_w:2 * att_w + pool_w + LANES]) + bfr_ref[...]
    lf_ref[0] = jax.nn.log_sigmoid(z)
    qt_ref[0] = (_dot_nt(wt_ref[0:att_w, :], xb) * scale).astype(BF)
    kt_ref[0] = _dot_nt(wt_ref[att_w:2 * att_w, :], xb)
    vt = _dot_nt(wt_ref[2 * att_w:3 * att_w, :], xb)
    vt_ref[0] = vt
    vtb_ref[0] = vt.astype(BF)
    zt = _dot_nt(wt_ref[3 * att_w:3 * att_w + BF_ROWS, :], xb)[0:n_heads] + bfc_ref[...]
    lft_ref[0] = jax.nn.log_sigmoid(zt)


def _proj_prompt(x, w_row, w_t, bf_row, bf_col, att_w, pool_w, n_heads, tm):
    bsz, t, d = x.shape
    kern = functools.partial(_proj_prompt_kernel, att_w=att_w, pool_w=pool_w, n_heads=n_heads,
                             scale=HEAD_DIM ** -0.5 * LOG2E)
    row = lambda w: pl.BlockSpec((1, tm, w), lambda i, j: (i, j, 0))
    col = lambda h: pl.BlockSpec((1, h, tm), lambda i, j: (i, 0, j))
    return pl.pallas_call(
        kern,
        grid=(bsz, t // tm),
        in_specs=[row(d), _const_spec(w_row.shape), _const_spec(w_t.shape),
                  _const_spec((1, LANES)), _const_spec((n_heads, 1))],
        out_specs=[col(att_w), row(att_w), col(att_w), col(att_w), col(att_w), row(pool_w),
                   row(LANES), col(n_heads)],
        out_shape=[jax.ShapeDtypeStruct((bsz, att_w, t), BF),
                   jax.ShapeDtypeStruct((bsz, t, att_w), BF),
                   jax.ShapeDtypeStruct((bsz, att_w, t), F32),
                   jax.ShapeDtypeStruct((bsz, att_w, t), F32),
                   jax.ShapeDtypeStruct((bsz, att_w, t), BF),
                   jax.ShapeDtypeStruct((bsz, t, pool_w), F32),
                   jax.ShapeDtypeStruct((bsz, t, LANES), F32),
                   jax.ShapeDtypeStruct((bsz, n_heads, t), F32)],
        compiler_params=_cparams(("parallel", "parallel")),
        name="proj_prompt",
    )(x, w_row, w_t, bf_row, bf_col)


def _proj_sample_kernel(x_ref, wr_ref, wt_ref, bfr_ref, q_ref, k_ref, v_ref, kb_ref, vb_ref, u_ref, lf_ref,
                        *, att_w, pool_w, scale):
    xb = x_ref[...].astype(BF)
    q_ref[...] = (_dot(xb, wr_ref[:, 0:att_w]) * scale).astype(BF)
    u_ref[...] = _dot(xb, wr_ref[:, 2 * att_w:2 * att_w + pool_w])
    z = _dot(xb, wr_ref[:, 2 * att_w + pool_w:2 * att_w + pool_w + LANES]) + bfr_ref[...]
    lf_ref[...] = jax.nn.log_sigmoid(z)
    k = _dot(xb, wr_ref[:, att_w:2 * att_w])
    k_ref[...] = k
    kb_ref[...] = k.astype(BF)
    v = _dot_nt(xb, wt_ref[2 * att_w:3 * att_w, :])
    v_ref[...] = v
    vb_ref[...] = v.astype(BF)


def _proj_sample(x2, w_row, w_t, bf_row, att_w, pool_w):
    n, d = x2.shape
    kern = functools.partial(_proj_sample_kernel, att_w=att_w, pool_w=pool_w, scale=HEAD_DIM ** -0.5)
    return pl.pallas_call(
        kern,
        out_shape=[jax.ShapeDtypeStruct((n, att_w), BF),
                   jax.ShapeDtypeStruct((n, att_w), F32),
                   jax.ShapeDtypeStruct((n, att_w), F32),
                   jax.ShapeDtypeStruct((n, att_w), BF),
                   jax.ShapeDtypeStruct((n, att_w), BF),
                   jax.ShapeDtypeStruct((n, pool_w), F32),
                   jax.ShapeDtypeStruct((n, LANES), F32)],
        compiler_params=pltpu.CompilerParams(vmem_limit_bytes=VMEM_LIMIT),
        name="proj_sample",
    )(x2, w_row, w_t, bf_row)


KEY_CHUNK = 32


def _group_reduce(fn, x):
    parts = [x[i * SUBLANES:(i + 1) * SUBLANES] for i in range(x.shape[0] // SUBLANES)]
    while len(parts) > 1:
        parts = [fn(parts[i], parts[i + 1]) if i + 1 < len(parts) else parts[i]
                 for i in range(0, len(parts), 2)]
    return parts[0]


AUG_ROWS = 16


def _split3(x):
    hi = x.astype(BF)
    r1 = x - hi.astype(F32)
    mid = r1.astype(BF)
    lo = (r1 - mid.astype(F32)).astype(BF)
    return hi, mid, lo


def _attn_kernel(qt_ref, k_ref, vt_ref, lf_ref, lft_ref, sel_ref, one_ref, o_ref,
                 ct_ref, kx_ref, qx_ref, ml_ref, mb_ref, *per_head, tq, n_heads):
    qi = pl.program_id(1)
    tk = tq
    rc = KEY_CHUNK
    t = k_ref.shape[1]
    n_pairs = n_heads // 2
    s_refs, p_refs, acc_refs = (per_head[i * n_heads:(i + 1) * n_heads] for i in range(3))

    @pl.when(qi == 0)
    def _():
        ct_ref[...] = _scan_incl(lft_ref[0], 1) * LOG2E
        blk = 512 if t % 512 == 0 else t
        for r0 in range(0, t, blk):
            rows = slice(r0, r0 + blk)
            c = _scan_incl(lf_ref[0, rows, :], 0)
            if r0:
                c = c + tot
            tot = c[blk - 1:blk, :]
            hi, mid, lo = _split3(c * LOG2E)
            aug = _dot(jnp.concatenate([hi, mid, lo], axis=1), sel_ref[...]) + one_ref[...]
            for pr in range(n_pairs):
                kx_ref[pr, rows, 0:LANES] = k_ref[0, rows, pr * LANES:(pr + 1) * LANES]
                kx_ref[pr, rows, LANES:2 * LANES] = aug[:, pr * LANES:(pr + 1) * LANES].astype(BF)
        qx_ref[:, LANES + AUG_ROWS:, :] = jnp.zeros((n_heads, LANES - AUG_ROWS, tq), BF)

    q0 = pl.multiple_of(qi * tq, tq)
    first = lax.broadcasted_iota(jnp.int32, (LANES, 1), 0) < HEAD_DIM
    causal = (lax.broadcasted_iota(jnp.int32, (tk, tq), 0)
              <= lax.broadcasted_iota(jnp.int32, (tk, tq), 1))
    arow = lax.broadcasted_iota(jnp.int32, (AUG_ROWS, 1), 0)

    heads = range(n_heads)
    for pair in range(n_pairs):
        qtp = qt_ref[0, pair * LANES:(pair + 1) * LANES, :]
        zero = jnp.zeros_like(qtp)
        qx_ref[2 * pair, 0:LANES, :] = jnp.where(first, qtp, zero)
        qx_ref[2 * pair + 1, 0:LANES, :] = jnp.where(first, zero, qtp)
    for h in heads:
        hf = h % 2
        ci = ct_ref[h:h + 1, pl.ds(q0, tq)]
        hi = ci.astype(BF).astype(F32)
        r1 = ci - hi
        mid = r1.astype(BF).astype(F32)
        lo = r1 - mid
        pick = ((arow >= 3 * hf) & (arow < 3 * hf + 3)).astype(F32)
        blk = jnp.where(arow == 6, hi, jnp.where(arow == 7, mid, jnp.where(arow == 8, lo, pick)))
        qx_ref[h, LANES:LANES + AUG_ROWS, :] = blk.astype(BF)

    def scores(jb, masked):
        k0 = pl.multiple_of(jb * tk, tk)
        out = []
        for h in heads:
            s = _dot(kx_ref[h // 2, pl.ds(k0, tk), :], qx_ref[h])
            out.append(jnp.where(causal, s, NEG_INF) if masked else s)
        return out

    def keep(h, s):
        s_refs[h][...] = s
        mb_ref[h:h + 1, :] = jnp.max(_group_reduce(jnp.maximum, s), axis=0, keepdims=True)

    def consume(jb, nxt=None):
        k0 = pl.multiple_of(jb * tk, tk)
        m_new, alpha = [], []
        for h in heads:
            m_old = ml_ref[2 * h:2 * h + 1, :]
            m_new.append(jnp.maximum(m_old, mb_ref[h:h + 1, :]))
            alpha.append(jnp.exp2(m_old - m_new[h]))
            ml_ref[2 * h:2 * h + 1, :] = m_new[h]
        for h in heads:
            lb = jnp.zeros((SUBLANES, tq), F32)
            for r in range(tk // rc):
                p = jnp.exp2(s_refs[h][r * rc:(r + 1) * rc, :] - m_new[h])
                lb = lb + _group_reduce(jnp.add, p)
                p_refs[h][r * rc:(r + 1) * rc, :] = p.astype(BF)
            ml_ref[2 * h + 1:2 * h + 2, :] = (alpha[h] * ml_ref[2 * h + 1:2 * h + 2, :]
                                              + jnp.sum(lb, axis=0, keepdims=True))
            if nxt is not None:
                keep(h, nxt[h])
        for h in heads:
            vh = vt_ref[0, h * HEAD_DIM:(h + 1) * HEAD_DIM, pl.ds(k0, tk)]
            acc_refs[h][...] = alpha[h] * acc_refs[h][...] + _dot(vh, p_refs[h][...])

    is_max = lax.broadcasted_iota(jnp.int32, (2 * n_heads, 1), 0) % 2 == 0
    ml_ref[...] = jnp.where(is_max, jnp.full((2 * n_heads, tq), NEG_INF, F32), 0.0)
    for h in heads:
        acc_refs[h][...] = jnp.zeros((HEAD_DIM, tq), F32)

    @pl.when(qi == 0)
    def _():
        for h, s in enumerate(scores(0, True)):
            keep(h, s)

    @pl.when(qi > 0)
    def _():
        for h, s in enumerate(scores(0, False)):
            keep(h, s)

        def body(j, _):
            consume(j, scores(j + 1, False))
            return 0

        lax.fori_loop(0, qi - 1, body, 0)
        consume(qi - 1, scores(qi, True))

    consume(qi)
    for pair in range(n_heads // 2):
        h0, h1 = 2 * pair, 2 * pair + 1
        out_t = jnp.concatenate([acc_refs[h0][...] / ml_ref[2 * h0 + 1:2 * h0 + 2, :],
                                 acc_refs[h1][...] / ml_ref[2 * h1 + 1:2 * h1 + 2, :]], axis=0)
        o_ref[0, :, pair * LANES:(pair + 1) * LANES] = out_t.T.astype(o_ref.dtype)


def _attn(qt, kb, vtb, lf, lft, tq):
    b, w, t = qt.shape
    n_heads = w // HEAD_DIM
    n_pairs = n_heads // 2
    sel = np.zeros((3 * LANES, n_pairs * LANES), np.float32)
    one = np.zeros((1, n_pairs * LANES), np.float32)
    for pr in range(n_pairs):
        for hf in range(2):
            for piece in range(3):
                sel[piece * LANES + 2 * pr + hf, pr * LANES + 3 * hf + piece] = -1.0
        one[0, pr * LANES + 6:pr * LANES + 9] = 1.0
    kern = functools.partial(_attn_kernel, tq=tq, n_heads=n_heads)
    per_b = lambda s1, s2: pl.BlockSpec((1, s1, s2), lambda i, j: (i, 0, 0))
    return pl.pallas_call(
        kern,
        grid=(b, t // tq),
        in_specs=[pl.BlockSpec((1, w, tq), lambda i, j: (i, 0, j)),
                  per_b(t, w), per_b(w, t), per_b(t, LANES), per_b(n_heads, t),
                  _const_spec(sel.shape), _const_spec(one.shape)],
        out_specs=pl.BlockSpec((1, tq, w), lambda i, j: (i, j, 0)),
        out_shape=jax.ShapeDtypeStruct((b, t, w), BF),
        scratch_shapes=[pltpu.VMEM((n_heads, t), F32),
                        pltpu.VMEM((n_pairs, t, 2 * LANES), BF),
                        pltpu.VMEM((n_heads, 2 * LANES, tq), BF),
                        pltpu.VMEM((2 * n_heads, tq), F32), pltpu.VMEM((n_heads, tq), F32)]
                       + [pltpu.VMEM((tq, tq), F32)] * n_heads
                       + [pltpu.VMEM((tq, tq), BF)] * n_heads
                       + [pltpu.VMEM((HEAD_DIM, tq), F32)] * n_heads,
        compiler_params=_cparams(("parallel", "arbitrary")),
        name="attn_prompt",
    )(qt, kb, vtb, lf, lft, jnp.asarray(sel, BF), jnp.asarray(one, F32))


def _page_copies(step, slot, ng, pages, pt_ref, ck_hbm, cv_hbm, kbuf, vbuf, sem):
    bb = step // ng
    first = (ng - 1 - step % ng) * pages
    out = []
    for i in range(pages):
        pid = pt_ref[bb, first + i]
        out.append(pltpu.make_async_copy(ck_hbm.at[pid], kbuf.at[slot, i], sem.at[slot, 0]))
        out.append(pltpu.make_async_copy(cv_hbm.at[pid], vbuf.at[slot, i], sem.at[slot, 1]))
    return out


def _paged_step(step, n, ng, total, pt_ref, q_ref, kn_ref, vn_ref, lf_ref, lft_ref, tri_ref,
                clf_ref, ck_hbm, cv_hbm, o_ref, qbd_ref, ci_ref, m_ref, l_ref, acc_ref, carry_ref,
                kbuf, vbuf, sem, *, n_heads, pages, tq):
    b = step // ng
    g = step % ng
    mp = n * pages
    first_page = (ng - g - n) * pages
    rows = n_heads * tq
    w = q_ref.shape[2]
    dma = (pt_ref, ck_hbm, cv_hbm, kbuf, vbuf, sem)
    order = [(k, i) for k in reversed(range(n)) for i in range(pages)]

    for k in range(n):
        for cp in _page_copies(step + k, k, ng, pages, *dma):
            cp.wait()

    def expand(x):
        return jnp.concatenate(
            [jnp.broadcast_to(x[h:h + 1, :], (tq, x.shape[1])) for h in range(n_heads)], axis=0)

    def update(s, pv_fn):
        m = m_ref[...]
        m_new = jnp.maximum(m, jnp.max(s, axis=-1, keepdims=True))
        alpha = jnp.exp(m - m_new)
        p = jnp.exp(s - m_new)
        l_ref[...] = alpha * l_ref[...] + jnp.sum(p, axis=-1, keepdims=True)
        acc_ref[...] = alpha * acc_ref[...] + pv_fn(p.astype(BF))
        m_ref[...] = m_new

    @pl.when(g == 0)
    def _():
        q = q_ref[0]
        lane = lax.broadcasted_iota(jnp.int32, (1, w), 1)
        zero = jnp.zeros_like(q)
        qbd_ref[...] = jnp.concatenate(
            [jnp.where((lane >= h * HEAD_DIM) & (lane < (h + 1) * HEAD_DIM), q, zero)
             for h in range(n_heads)], axis=0)
        c = _scan_incl(lf_ref[0], 0)
        ci_ref[...] = jnp.concatenate([c[:, h:h + 1] for h in range(n_heads)], axis=0)
        m_ref[...] = jnp.full((rows, 1), NEG_INF, F32)
        l_ref[...] = jnp.zeros((rows, 1), F32)
        acc_ref[...] = jnp.zeros((rows, w), F32)
        carry_ref[...] = jnp.zeros((n_heads, PAGE_SIZE), F32)
        cj = _scan_incl(lft_ref[0], 1)
        s = _dot_nt(qbd_ref[...], kn_ref[0]) + ci_ref[...] - expand(cj)
        r_idx = lax.broadcasted_iota(jnp.int32, (rows, LANES), 0)
        c_idx = lax.broadcasted_iota(jnp.int32, (rows, LANES), 1)
        s = jnp.where(c_idx <= jnp.bitwise_and(r_idx, tq - 1), s, NEG_INF)
        update(s, lambda p: _dot(p, vn_ref[0]))

    ktc = jnp.concatenate([kbuf[k, i].reshape(w, PAGE_SIZE) for k, i in order], axis=1).astype(BF)
    vtc = jnp.concatenate([vbuf[k, i].reshape(w, PAGE_SIZE) for k, i in order], axis=1).astype(BF)
    lf = jnp.concatenate([clf_ref[pt_ref[b, first_page + i]] for i in range(mp)], axis=0)
    hi, mid, lo = _split3(lf)
    y = _dot(jnp.concatenate([hi, mid, lo], axis=0), tri_ref[...])
    ph = mp * n_heads
    y = y[0:ph] + y[ph:2 * ph] + y[2 * ph:3 * ph]
    run = carry_ref[...]
    cjs = [None] * mp
    for pg in reversed(range(mp)):
        blk = slice(pg * n_heads, (pg + 1) * n_heads)
        cjs[pg] = -(run + (y[blk, 0:PAGE_SIZE] - lf[blk]))
        run = run + y[blk, PAGE_SIZE:2 * PAGE_SIZE]
    carry_ref[...] = run
    cj = jnp.concatenate(cjs, axis=1)
    s = _dot(qbd_ref[...], ktc) + ci_ref[...] - expand(cj)
    update(s, lambda p: _dot_nt(p, vtc))

    @pl.when(g + n == ng)
    def _():
        lane = lax.broadcasted_iota(jnp.int32, (1, w), 1)
        out = jnp.zeros((tq, w), F32)
        for h in range(n_heads):
            sl = slice(h * tq, (h + 1) * tq)
            val = acc_ref[sl, :] / l_ref[sl, :]
            out = jnp.where((lane >= h * HEAD_DIM) & (lane < (h + 1) * HEAD_DIM), val, out)
        o_ref[0] = out.astype(o_ref.dtype)

    @pl.when(step + n < total)
    def _():
        for k in range(n):
            for cp in _page_copies(step + n + k, k, ng, pages, *dma):
                cp.start()


def _paged_operands(q, cache_kt, cache_lft, pages, slots, seq_of_step):
    db, tq, w = q.shape
    n_heads = w // HEAD_DIM
    rows = n_heads * tq
    assert tq & (tq - 1) == 0
    per_seq = lambda s1, s2: pl.BlockSpec((1, s1, s2), lambda i, j, pt: (seq_of_step(i, j), 0, 0))
    in_specs = [per_seq(tq, w), per_seq(PAGE_SIZE, w), per_seq(PAGE_SIZE, w), per_seq(tq, LANES),
                per_seq(n_heads, LANES),
                _const_spec((PAGE_SIZE, 2 * PAGE_SIZE)), _const_spec(cache_lft.shape),
                pl.BlockSpec(memory_space=pl.ANY), pl.BlockSpec(memory_space=pl.ANY)]
    page_buf = pltpu.VMEM((slots, pages, n_heads, HEAD_DIM, PAGE_SIZE), cache_kt.dtype)
    scratch = [pltpu.VMEM((rows, w), BF), pltpu.VMEM((rows, 1), F32),
               pltpu.VMEM((rows, 1), F32), pltpu.VMEM((rows, 1), F32),
               pltpu.VMEM((rows, w), F32), pltpu.VMEM((n_heads, PAGE_SIZE), F32),
               page_buf, page_buf, pltpu.SemaphoreType.DMA((slots, 2))]
    pos = lax.broadcasted_iota(jnp.int32, (PAGE_SIZE, PAGE_SIZE), 0)
    col = lax.broadcasted_iota(jnp.int32, (PAGE_SIZE, PAGE_SIZE), 1)
    tri = jnp.concatenate([(pos >= col).astype(BF), jnp.ones((PAGE_SIZE, PAGE_SIZE), BF)], axis=1)
    return in_specs, per_seq(tq, w), scratch, tri


def _mix_tail(pooled, att, x, wp_ref, ps_ref, wo_ref, g_ref, b_ref, alpha):
    gd = pooled[0].shape[1]
    att_w = att.shape[1]
    mixed = [_dot(pooled[g].astype(BF), wp_ref[g]) * ps_ref[:, g * gd:(g + 1) * gd]
             for g in range(len(pooled))]
    pool = jnp.concatenate(mixed, axis=1).astype(BF)
    mix = _dot(att, wo_ref[0:att_w, :]) + _dot(pool, wo_ref[att_w:, :])
    return _layer_norm(alpha * x + mix, g_ref[...], b_ref[...])


def _shift_down(x, s):
    n8, c = x.shape[0] // SUBLANES, x.shape[1]
    if s == SUBLANES:
        return jnp.concatenate([x[0:SUBLANES], x[:-SUBLANES]], axis=0)
    r = pltpu.roll(x.reshape(n8, SUBLANES, c), s, 1)
    prev = jnp.concatenate([r[0:1], r[:-1]], axis=0)
    sub = lax.broadcasted_iota(jnp.int32, (1, SUBLANES, c), 1)
    return jnp.where(sub < s, prev, r).reshape(n8 * SUBLANES, c)


def _mix_prompt_kernel(x_ref, att_ref, u_ref, halo_ref, wp_ref, ps_ref, wo_ref, g_ref, b_ref,
                       o_ref, *, tm, alpha):
    t = pl.program_id(1)
    hl = MAX_WINDOW
    started = (t > 0).astype(F32)
    pos = t * tm + lax.broadcasted_iota(jnp.int32, (tm, 1), 0)
    gd = u_ref.shape[2] // len(POOL_WINDOWS)
    pooled = []
    for g, w in enumerate(POOL_WINDOWS):
        lanes = slice(g * gd, (g + 1) * gd)
        tok = u_ref[0, :, lanes]
        win = jnp.concatenate([halo_ref[0, :, lanes] * started, tok], axis=0)
        k = 1
        while k < w:
            win = win + _shift_down(win, k)
            k *= 2
        cnt = jnp.minimum(pos + 1, w).astype(F32)
        pooled.append(win[hl:] / cnt - tok)
    o_ref[0] = _mix_tail(pooled, att_ref[0], x_ref[0], wp_ref, ps_ref, wo_ref, g_ref, b_ref, alpha)


def _mix_prompt(x, att, u, w_pool, pool_scale, w_out, g, b, alpha, tm):
    bsz, t, d = x.shape
    att_w, pool_w = att.shape[2], u.shape[2]
    hl = MAX_WINDOW
    ng, gd = w_pool.shape[0], w_pool.shape[1]
    kern = functools.partial(_mix_prompt_kernel, tm=tm, alpha=alpha)
    row = lambda w: pl.BlockSpec((1, tm, w), lambda i, j: (i, j, 0))
    halo = pl.BlockSpec((1, hl, pool_w), lambda i, j: (i, jnp.maximum(j * (tm // hl) - 1, 0), 0))
    return pl.pallas_call(
        kern,
        grid=(bsz, t // tm),
        in_specs=[row(d), row(att_w), row(pool_w), halo,
                  _const_spec((ng, gd, gd)), _const_spec((1, pool_w)), _const_spec((att_w + pool_w, d)),
                  _const_spec((1, d)), _const_spec((1, d))],
        out_specs=row(d),
        out_shape=jax.ShapeDtypeStruct((bsz, t, d), F32),
        compiler_params=_cparams(("parallel", "parallel")),
        name="mix_prompt",
    )(x, att, u, u, w_pool, pool_scale, w_out, g, b)


def _mix_sample_kernel(x_ref, att_ref, full_ref, wp_ref, ps_ref, wo_ref, g_ref, b_ref, o_ref,
                       *, ts, n_past, alpha):
    nseq = full_ref.shape[0]
    hl = MAX_WINDOW
    gd = full_ref.shape[2] // len(POOL_WINDOWS)
    pos = n_past + lax.broadcasted_iota(jnp.int32, (1, ts, 1), 1)
    pooled = []
    for g, w in enumerate(POOL_WINDOWS):
        lanes = slice(g * gd, (g + 1) * gd)
        tok = full_ref[:, hl:hl + ts, lanes]
        acc = tok
        for k in range(1, w):
            acc = acc + full_ref[:, hl - k:hl - k + ts, lanes]
        cnt = jnp.minimum(pos + 1, w).astype(F32)
        pooled.append((acc / cnt - tok).reshape(nseq * ts, gd))
    o_ref[...] = _mix_tail(pooled, att_ref[...], x_ref[...], wp_ref, ps_ref, wo_ref, g_ref, b_ref, alpha)


def _mix_sample(x2, att2, full, w_pool, pool_scale, w_out, g, b, alpha, ts, n_past):
    n, d = x2.shape
    kern = functools.partial(_mix_sample_kernel, ts=ts, n_past=n_past, alpha=alpha)
    return pl.pallas_call(
        kern,
        out_shape=jax.ShapeDtypeStruct((n, d), F32),
        compiler_params=pltpu.CompilerParams(vmem_limit_bytes=VMEM_LIMIT),
        name="mix_sample",
    )(x2, att2, full, w_pool, pool_scale, w_out, g, b)


def _shift_rows(h, s):
    n8, c = h.shape[0] // SUBLANES, h.shape[1]
    r = pltpu.roll(h.reshape(n8, SUBLANES, c), s, 1)
    sub = lax.broadcasted_iota(jnp.int32, (1, SUBLANES, c), 1)
    return jnp.where(sub < s, r[:-1], r[1:]).reshape((n8 - 1) * SUBLANES, c)


def _ffn_chunk(hg3, hv3, wc_ref, bc_ref, gcols, vcols):
    def conv(h3, cols, scale):
        hc = bc_ref[:, cols] * scale + h3[0] * (wc_ref[0:1, cols] * scale)
        for j in range(1, CONV_WIDTH):
            hc = hc + h3[j] * (wc_ref[j:j + 1, cols] * scale)
        return hc
    g = conv(hg3, gcols, 1.0)
    vh = conv(hv3, vcols, 0.5)
    t = jnp.tanh(g * (GELU_C1 + (GELU_C1 * GELU_C0) * (g * g)))
    return ((g * (1.0 + t)) * vh).astype(BF)


def _ffn_prompt_kernel(pt_ref, x_ref, halo_ref, wu_ref, bu_ref, wc_ref, bc_ref, wd_ref, g_ref, b_ref,
                       *rest, tm, d_ff, chunk, alpha, paged):
    sub, ng, total = paged["sub"], paged["ng"], paged["total"]
    pa_in, (o_ref, tail_ref, oa_ref), pa_scratch = rest[0:9], rest[9:12], rest[12:]
    dma = (pt_ref, pa_in[7], pa_in[8], pa_scratch[6], pa_scratch[7], pa_scratch[8])
    step0 = (pl.program_id(0) * pl.num_programs(1) + pl.program_id(1)) * sub

    @pl.when(step0 == 0)
    def _():
        for k in range(sub):
            for cp in _page_copies(k, k, ng, paged["pages"], *dma):
                cp.start()

    _paged_step(step0, sub, ng, total, pt_ref, *pa_in, oa_ref, *pa_scratch,
                n_heads=paged["n_heads"], pages=paged["pages"], tq=paged["tq"])

    t = pl.program_id(1)
    hl = SUBLANES
    x = x_ref[0]
    xin = jnp.concatenate([halo_ref[0], x], axis=0).astype(BF)
    n_chunks = d_ff // chunk
    cols_of = lambda c: (slice(c * chunk, (c + 1) * chunk), slice(d_ff + c * chunk, d_ff + (c + 1) * chunk))

    def up(c):
        return [_dot(xin, wu_ref[:, cols]) for cols in cols_of(c)]

    started = (t > 0).astype(F32)

    def shifted(hd, cols):
        body = hd[hl:] + bu_ref[:, cols]
        tail_ref[0, :, cols] = body[tm - hl:tm]
        h = jnp.concatenate([(hd[0:hl] + bu_ref[:, cols]) * started, body], axis=0)
        return [_shift_rows(h, CONV_WIDTH - 1 - j) for j in range(CONV_WIDTH - 1)] + [body]

    acc = jnp.zeros((tm, x.shape[1]), F32)
    nxt = up(0)
    for c in range(n_chunks):
        cur = nxt
        if c + 1 < n_chunks:
            nxt = up(c + 1)
        gcols, vcols = cols_of(c)
        act = _ffn_chunk(shifted(cur[0], gcols), shifted(cur[1], vcols), wc_ref, bc_ref, gcols, vcols)
        acc = acc + _dot(act, wd_ref[gcols, :])
    o_ref[0] = _layer_norm(alpha * x + acc, g_ref[...], b_ref[...])


def _ffn_prompt_and_paged_attn(x1, w_up, b_up, w_conv, b_conv, w_down, g, b, alpha, tm, chunk,
                               page_table, q, kn, vn, lf, lft, cache_kt, cache_vt, cache_lft):
    bsz, t, d = x1.shape
    d_ff = w_down.shape[0]
    hl = SUBLANES
    nt = t // tm
    db, tq, w = q.shape
    n_pages = page_table.shape[1]
    fits = lambda p: (n_pages % p == 0 and (db * (n_pages // p)) % (bsz * nt) == 0
                      and (n_pages // p) % ((db * (n_pages // p)) // (bsz * nt)) == 0)
    pages = next(p for p in (16, 8, 4, 2, 1) if fits(p))
    ng = n_pages // pages
    total = db * ng
    sub = total // (bsz * nt)
    paged = dict(sub=sub, ng=ng, total=total, pages=pages, n_heads=w // HEAD_DIM, tq=tq)
    pa_specs, pa_out, pa_scratch, tri = _paged_operands(
        q, cache_kt, cache_lft, pages, sub, lambda i, j: ((i * nt + j) * sub) // ng)
    kern = functools.partial(_ffn_prompt_kernel, tm=tm, d_ff=d_ff, chunk=chunk, alpha=alpha, paged=paged)
    row = pl.BlockSpec((1, tm, d), lambda i, j, pt: (i, j, 0))
    halo = pl.BlockSpec((1, hl, d), lambda i, j, pt: (i, jnp.maximum(j * (tm // hl) - 1, 0), 0))
    grid_spec = pltpu.PrefetchScalarGridSpec(
        num_scalar_prefetch=1,
        grid=(bsz, nt),
        in_specs=[row, halo, _const_spec((d, 2 * d_ff)), _const_spec((1, 2 * d_ff)),
                  _const_spec((CONV_WIDTH, 2 * d_ff)), _const_spec((1, 2 * d_ff)),
                  _const_spec((d_ff, d)), _const_spec((1, d)), _const_spec((1, d))] + pa_specs,
        out_specs=[row, pl.BlockSpec((1, hl, 2 * d_ff), lambda i, j, pt: (i, 0, 0)), pa_out],
        scratch_shapes=pa_scratch)
    return pl.pallas_call(
        kern,
        grid_spec=grid_spec,
        out_shape=[jax.ShapeDtypeStruct((bsz, t, d), F32),
                   jax.ShapeDtypeStruct((bsz, hl, 2 * d_ff), F32),
                   jax.ShapeDtypeStruct((db, tq, w), BF)],
        compiler_params=_cparams(("arbitrary", "arbitrary")),
        name="ffn_prompt_attn_paged",
    )(page_table, x1, x1, w_up, b_up, w_conv, b_conv, w_down, g, b,
      q, kn, vn, lf, lft, tri, cache_lft, cache_kt, cache_vt)


def _ffn_sample_kernel(x_ref, st_ref, wu_ref, bu_ref, wc_ref, bc_ref, wd_ref, g_ref, b_ref,
                       o_ref, h_ref, buf_ref, *, ts, d_ff, chunk, alpha):
    nseq = st_ref.shape[0]
    n = nseq * ts
    hl = SUBLANES
    nst = CONV_WIDTH - 1
    x = x_ref[...]
    xb = x.astype(BF)
    acc = jnp.zeros((n, x.shape[1]), F32)
    for c in range(d_ff // chunk):
        gcols = slice(c * chunk, (c + 1) * chunk)
        vcols = slice(d_ff + c * chunk, d_ff + (c + 1) * chunk)
        h3 = []
        for cols in (gcols, vcols):
            h = _dot(xb, wu_ref[:, cols]) + bu_ref[:, cols]
            h_ref[:, cols] = h
            buf_ref[:, hl - nst:hl, :] = st_ref[:, :, cols]
            buf_ref[:, hl:hl + ts, :] = h.reshape(nseq, ts, chunk)
            h3.append([buf_ref[:, hl - 2 + j:hl - 2 + j + ts, :].reshape(n, chunk)
                       for j in range(CONV_WIDTH)])
        act = _ffn_chunk(h3[0], h3[1], wc_ref, bc_ref, gcols, vcols)
        acc = acc + _dot(act, wd_ref[gcols, :])
    o_ref[...] = _layer_norm(alpha * x + acc, g_ref[...], b_ref[...])


def _ffn_sample(x2, state, w_up, b_up, w_conv, b_conv, w_down, g, b, alpha, ts, chunk):
    n, d = x2.shape
    d_ff = w_down.shape[0]
    nseq = n // ts
    kern = functools.partial(_ffn_sample_kernel, ts=ts, d_ff=d_ff, chunk=chunk, alpha=alpha)
    return pl.pallas_call(
        kern,
        out_shape=[jax.ShapeDtypeStruct((n, d), F32), jax.ShapeDtypeStruct((n, 2 * d_ff), F32)],
        scratch_shapes=[pltpu.VMEM((nseq, SUBLANES + ts, chunk), F32)],
        compiler_params=pltpu.CompilerParams(vmem_limit_bytes=VMEM_LIMIT),
        name="ffn_sample",
    )(x2, state, w_up, b_up, w_conv, b_conv, w_down, g, b)


def _ffn_chunk_size(d_ff):
    for c in (512, 256, 128):
        if d_ff % c == 0:
            return c
    raise ValueError("d_ff must be a multiple of 128")


def kernel(x_prompt, x_sample, cache_k, cache_v, cache_logf, state_pool, state_conv, page_table,
           w_in, b_f, w_pool, pool_scale, w_out, ln1_g, ln1_b, w_up, b_up, w_conv, b_conv, w_down,
           ln2_g, ln2_b):
    bsz, t_p, d = x_prompt.shape
    db, t_s, _ = x_sample.shape
    depth = w_in.shape[0]
    n_heads = cache_k.shape[3]
    att_w = n_heads * HEAD_DIM
    pool_w = state_pool.shape[3]
    d_ff = w_down.shape[1]
    n_past = page_table.shape[1] * PAGE_SIZE
    alpha = (2 * depth) ** 0.25
    assert HEAD_DIM * 2 == LANES and n_heads % 2 == 0 and n_heads <= SUBLANES
    assert t_s == SUBLANES and cache_k.shape[2] == PAGE_SIZE and t_p >= MAX_WINDOW
    assert w_in.shape[2] == 3 * att_w + n_heads + pool_w

    tm_proj = min(512, t_p)
    tq = min(256, t_p)
    tm_mix = min(256, t_p)
    tm_ffn = min(256, t_p)
    chunk = _ffn_chunk_size(d_ff)

    yp, ys = x_prompt, x_sample
    outs = [[] for _ in range(10)]
    for l in range(depth):
        wi = w_in[l]
        fcols = wi[:, 3 * att_w:3 * att_w + n_heads]
        w_row = jnp.concatenate(
            [wi[:, :2 * att_w], wi[:, 3 * att_w + n_heads:], jnp.pad(fcols, ((0, 0), (0, LANES - n_heads)))],
            axis=1).astype(BF)
        w_t = jnp.concatenate(
            [wi[:, :3 * att_w].T, jnp.pad(fcols.T, ((0, BF_ROWS - n_heads), (0, 0)))],
            axis=0).astype(BF)
        bf_row = jnp.pad(b_f[l], (0, LANES - n_heads)).reshape(1, LANES)
        bf_col = b_f[l].reshape(n_heads, 1)
        wp_b = w_pool[l].astype(BF)
        ps = pool_scale[l].reshape(1, pool_w)
        wo_b = w_out[l].astype(BF)
        g1, b1 = ln1_g[l].reshape(1, d), ln1_b[l].reshape(1, d)
        wu_b = w_up[l].astype(BF)
        bu = b_up[l].reshape(1, 2 * d_ff)
        wc = w_conv[l]
        bc = b_conv[l].reshape(1, 2 * d_ff)
        wd_b = w_down[l].astype(BF)
        g2, b2 = ln2_g[l].reshape(1, d), ln2_b[l].reshape(1, d)

        qt, kb, kt, vt, vtb, u, lf, lft = _proj_prompt(yp, w_row, w_t, bf_row, bf_col, att_w, pool_w,
                                                       n_heads, tm_proj)
        n_s = db * t_s
        qs, ks, vs, kbs, vbs, us, lfs = _proj_sample(ys.reshape(n_s, d), w_row, w_t, bf_row, att_w, pool_w)
        att = _attn(qt, kb, vtb, lf, lft, tq)
        x1 = _mix_prompt(yp, att, u, wp_b, ps, wo_b, g1, b1, alpha, tm_mix)

        lfs3 = lfs.reshape(db, t_s, LANES)
        lfts = jnp.pad(jnp.swapaxes(lfs3[:, :, :n_heads], 1, 2), ((0, 0), (0, 0), (0, LANES - t_s)))
        padk = lambda a: jnp.pad(a.reshape(db, t_s, att_w), ((0, 0), (0, PAGE_SIZE - t_s), (0, 0)))
        ckt = jnp.transpose(cache_k[l], (0, 2, 3, 1))
        cvt = jnp.transpose(cache_v[l], (0, 2, 3, 1))
        clft = jnp.swapaxes(cache_logf[l], 1, 2)
        yp, tail, att_s = _ffn_prompt_and_paged_attn(
            x1, wu_b, bu, wc, bc, wd_b, g2, b2, alpha, tm_ffn, chunk,
            page_table, qs.reshape(db, t_s, att_w), padk(kbs), padk(vbs), lfs3, lfts, ckt, cvt, clft)
        to_bthd = lambda a: jnp.transpose(a.reshape(bsz, n_heads, HEAD_DIM, t_p), (0, 3, 1, 2))
        outs[0].append(to_bthd(kt))
        outs[1].append(to_bthd(vt))
        outs[2].append(jnp.swapaxes(lft, 1, 2))
        outs[3].append(u[:, t_p - (MAX_WINDOW - 1):, :])
        outs[4].append(tail[:, SUBLANES - (CONV_WIDTH - 1):, :])

        us3 = us.reshape(db, t_s, pool_w)
        full = jnp.concatenate([jnp.zeros((db, 1, pool_w), F32), state_pool[l], us3], axis=1)
        x1s = _mix_sample(ys.reshape(n_s, d), att_s.reshape(n_s, att_w), full, wp_b, ps, wo_b, g1, b1,
                          alpha, t_s, n_past)
        ys2, hs = _ffn_sample(x1s, state_conv[l], wu_b, bu, wc, bc, wd_b, g2, b2, alpha, t_s, chunk)
        ys = ys2.reshape(db, t_s, d)
        outs[5].append(ks.reshape(db, t_s, n_heads, HEAD_DIM))
        outs[6].append(vs.reshape(db, t_s, n_heads, HEAD_DIM))
        outs[7].append(lfs3[:, :, :n_heads])
        outs[8].append(full[:, t_s + 1:, :])
        outs[9].append(hs.reshape(db, t_s, 2 * d_ff)[:, t_s - (CONV_WIDTH - 1):, :])

    st = [jnp.stack(o, 0) for o in outs]
    return (yp, ys, st[0], st[1], st[2], st[3], st[4], st[5], st[6], st[7], st[8], st[9])
```

```python
import functools

import jax
import jax.numpy as jnp
import numpy as np
from jax import lax
from jax.experimental import pallas as pl
from jax.experimental.pallas import tpu as pltpu

BF = jnp.bfloat16
F32 = jnp.float32

HEAD_DIM = 64
PAGE_SIZE = 128
POOL_WINDOWS = (2, 4, 8, 16)
MAX_WINDOW = max(POOL_WINDOWS)
CONV_WIDTH = 3
LN_EPS = 1e-5
NEG_INF = -1e30
GELU_C1 = float(np.float32(np.sqrt(2.0 / np.pi)))
GELU_C0 = float(np.float32(0.044715))
LOG2E = float(np.log2(np.e))
LANES = 128
SUBLANES = 8
BF_ROWS = 16
VMEM_LIMIT = 56 * 1024 * 1024


def _cparams(sem):
    return pltpu.CompilerParams(dimension_semantics=sem, vmem_limit_bytes=VMEM_LIMIT)


def _const_spec(shape):
    nd = len(shape)
    return pl.BlockSpec(shape, lambda *_: (0,) * nd, pipeline_mode=pl.Buffered(1))


def _dot(a, b):
    return jnp.dot(a, b, preferred_element_type=F32)


def _dot_nt(a, b):
    return lax.dot_general(a, b, (((1,), (1,)), ((), ())), preferred_element_type=F32)


def _layer_norm(r, g, b):
    mu = jnp.mean(r, axis=-1, keepdims=True)
    d = r - mu
    var = jnp.mean(d * d, axis=-1, keepdims=True)
    return d * lax.rsqrt(var + LN_EPS) * g + b


def _scan_incl(x, axis, reverse=False):
    n = x.shape[axis]
    idx = lax.broadcasted_iota(jnp.int32, x.shape, axis)
    s = 1
    while s < n:
        if reverse:
            x = x + jnp.where(idx + s < n, pltpu.roll(x, n - s, axis), 0.0)
        else:
            x = x + jnp.where(idx >= s, pltpu.roll(x, s, axis), 0.0)
        s *= 2
    return x


def _proj_prompt_kernel(x_ref, wr_ref, wt_ref, bfr_ref, bfc_ref,
                        qt_ref, kb_ref, kt_ref, vt_ref, vtb_ref, u_ref, lf_ref, lft_ref,
                        *, att_w, pool_w, n_heads, scale):
    xb = x_ref[0].astype(BF)
    u_ref[0] = _dot(xb, wr_ref[:, 2 * att_w:2 * att_w + pool_w])
    z = _dot(xb, wr_ref[:, 2 * att_w + pool_w:2 * att_w + pool_w + LANES]) + bfr_ref[...]
    lf_ref[0] = jax.nn.log_sigmoid(z)
    qt_ref[0] = (_dot_nt(wt_ref[0:att_w, :], xb) * scale).astype(BF)
    kt = _dot_nt(wt_ref[att_w:2 * att_w, :], xb)
    kt_ref[0] = kt
    kb_ref[0] = kt.T.astype(BF)
    vt = _dot_nt(wt_ref[2 * att_w:3 * att_w, :], xb)
    vt_ref[0] = vt
    vtb_ref[0] = vt.astype(BF)
    zt = _dot_nt(wt_ref[3 * att_w:3 * att_w + BF_ROWS, :], xb)[0:n_heads] + bfc_ref[...]
    lft_ref[0] = jax.nn.log_sigmoid(zt)


def _proj_prompt(x, w_row, w_t, bf_row, bf_col, att_w, pool_w, n_heads, tm):
    bsz, t, d = x.shape
    kern = functools.partial(_proj_prompt_kernel, att_w=att_w, pool_w=pool_w, n_heads=n_heads,
                             scale=HEAD_DIM ** -0.5 * LOG2E)
    row = lambda w: pl.BlockSpec((1, tm, w), lambda i, j: (i, j, 0))
    col = lambda h: pl.BlockSpec((1, h, tm), lambda i, j: (i, 0, j))
    return pl.pallas_call(
        kern,
        grid=(bsz, t // tm),
        in_specs=[row(d), _const_spec(w_row.shape), _const_spec(w_t.shape),
                  _const_spec((1, LANES)), _const_spec((n_heads, 1))],
        out_specs=[col(att_w), row(att_w), col(att_w), col(att_w), col(att_w), row(pool_w),
                   row(LANES), col(n_heads)],
        out_shape=[jax.ShapeDtypeStruct((bsz, att_w, t), BF),
                   jax.ShapeDtypeStruct((bsz, t, att_w), BF),
                   jax.ShapeDtypeStruct((bsz, att_w, t), F32),
                   jax.ShapeDtypeStruct((bsz, att_w, t), F32),
                   jax.ShapeDtypeStruct((bsz, att_w, t), BF),
                   jax.ShapeDtypeStruct((bsz, t, pool_w), F32),
                   jax.ShapeDtypeStruct((bsz, t, LANES), F32),
                   jax.ShapeDtypeStruct((bsz, n_heads, t), F32)],
        compiler_params=_cparams(("parallel", "parallel")),
        name="proj_prompt",
    )(x, w_row, w_t, bf_row, bf_col)


def _proj_sample_kernel(x_ref, wr_ref, wt_ref, bfr_ref, q_ref, k_ref, v_ref, kb_ref, vb_ref, u_ref, lf_ref,
                        *, att_w, pool_w, scale):
    xb = x_ref[...].astype(BF)
    q_ref[...] = (_dot(xb, wr_ref[:, 0:att_w]) * scale).astype(BF)
    u_ref[...] = _dot(xb, wr_ref[:, 2 * att_w:2 * att_w + pool_w])
    z = _dot(xb, wr_ref[:, 2 * att_w + pool_w:2 * att_w + pool_w + LANES]) + bfr_ref[...]
    lf_ref[...] = jax.nn.log_sigmoid(z)
    k = _dot(xb, wr_ref[:, att_w:2 * att_w])
    k_ref[...] = k
    kb_ref[...] = k.astype(BF)
    v = _dot_nt(xb, wt_ref[2 * att_w:3 * att_w, :])
    v_ref[...] = v
    vb_ref[...] = v.astype(BF)


def _proj_sample(x2, w_row, w_t, bf_row, att_w, pool_w):
    n, d = x2.shape
    kern = functools.partial(_proj_sample_kernel, att_w=att_w, pool_w=pool_w, scale=HEAD_DIM ** -0.5)
    return pl.pallas_call(
        kern,
        out_shape=[jax.ShapeDtypeStruct((n, att_w), BF),
                   jax.ShapeDtypeStruct((n, att_w), F32),
                   jax.ShapeDtypeStruct((n, att_w), F32),
                   jax.ShapeDtypeStruct((n, att_w), BF),
                   jax.ShapeDtypeStruct((n, att_w), BF),
                   jax.ShapeDtypeStruct((n, pool_w), F32),
                   jax.ShapeDtypeStruct((n, LANES), F32)],
        compiler_params=pltpu.CompilerParams(vmem_limit_bytes=VMEM_LIMIT),
        name="proj_sample",
    )(x2, w_row, w_t, bf_row)


KEY_CHUNK = 32


def _group_reduce(fn, x):
    parts = [x[i * SUBLANES:(i + 1) * SUBLANES] for i in range(x.shape[0] // SUBLANES)]
    while len(parts) > 1:
        parts = [fn(parts[i], parts[i + 1]) if i + 1 < len(parts) else parts[i]
                 for i in range(0, len(parts), 2)]
    return parts[0]


AUG_ROWS = 16


def _split3(x):
    hi = x.astype(BF)
    r1 = x - hi.astype(F32)
    mid = r1.astype(BF)
    lo = (r1 - mid.astype(F32)).astype(BF)
    return hi, mid, lo


def _attn_kernel(qt_ref, k_ref, vt_ref, lf_ref, lft_ref, sel_ref, one_ref, o_ref,
                 ct_ref, kx_ref, qx_ref, ml_ref, mb_ref, *per_head, tq, n_heads):
    qi = pl.program_id(1)
    tk = tq
    rc = KEY_CHUNK
    t = k_ref.shape[1]
    n_pairs = n_heads // 2
    s_refs, p_refs, acc_refs = (per_head[i * n_heads:(i + 1) * n_heads] for i in range(3))

    @pl.when(qi == 0)
    def _():
        ct_ref[...] = _scan_incl(lft_ref[0], 1) * LOG2E
        blk = 512 if t % 512 == 0 else t
        for r0 in range(0, t, blk):
            rows = slice(r0, r0 + blk)
            c = _scan_incl(lf_ref[0, rows, :], 0)
            if r0:
                c = c + tot
            tot = c[blk - 1:blk, :]
            hi, mid, lo = _split3(c * LOG2E)
            aug = _dot(jnp.concatenate([hi, mid, lo], axis=1), sel_ref[...]) + one_ref[...]
            for pr in range(n_pairs):
                kx_ref[pr, rows, 0:LANES] = k_ref[0, rows, pr * LANES:(pr + 1) * LANES]
                kx_ref[pr, rows, LANES:2 * LANES] = aug[:, pr * LANES:(pr + 1) * LANES].astype(BF)
        qx_ref[:, LANES + AUG_ROWS:, :] = jnp.zeros((n_heads, LANES - AUG_ROWS, tq), BF)

    q0 = pl.multiple_of(qi * tq, tq)
    first = lax.broadcasted_iota(jnp.int32, (LANES, 1), 0) < HEAD_DIM
    causal = (lax.broadcasted_iota(jnp.int32, (tk, tq), 0)
              <= lax.broadcasted_iota(jnp.int32, (tk, tq), 1))
    arow = lax.broadcasted_iota(jnp.int32, (AUG_ROWS, 1), 0)

    heads = range(n_heads)
    for pair in range(n_pairs):
        qtp = qt_ref[0, pair * LANES:(pair + 1) * LANES, :]
        zero = jnp.zeros_like(qtp)
        qx_ref[2 * pair, 0:LANES, :] = jnp.where(first, qtp, zero)
        qx_ref[2 * pair + 1, 0:LANES, :] = jnp.where(first, zero, qtp)
    for h in heads:
        hf = h % 2
        ci = ct_ref[h:h + 1, pl.ds(q0, tq)]
        hi = ci.astype(BF).astype(F32)
        r1 = ci - hi
        mid = r1.astype(BF).astype(F32)
        lo = r1 - mid
        pick = ((arow >= 3 * hf) & (arow < 3 * hf + 3)).astype(F32)
        blk = jnp.where(arow == 6, hi, jnp.where(arow == 7, mid, jnp.where(arow == 8, lo, pick)))
        qx_ref[h, LANES:LANES + AUG_ROWS, :] = blk.astype(BF)

    def scores(jb, masked):
        k0 = pl.multiple_of(jb * tk, tk)
        out = []
        for h in heads:
            s = _dot(kx_ref[h // 2, pl.ds(k0, tk), :], qx_ref[h])
            out.append(jnp.where(causal, s, NEG_INF) if masked else s)
        return out

    def keep(h, s):
        s_refs[h][...] = s
        mb_ref[h:h + 1, :] = jnp.max(_group_reduce(jnp.maximum, s), axis=0, keepdims=True)

    def consume(jb, nxt=None):
        k0 = pl.multiple_of(jb * tk, tk)
        m_new, alpha = [], []
        for h in heads:
            m_old = ml_ref[2 * h:2 * h + 1, :]
            m_new.append(jnp.maximum(m_old, mb_ref[h:h + 1, :]))
            alpha.append(jnp.exp2(m_old - m_new[h]))
            ml_ref[2 * h:2 * h + 1, :] = m_new[h]
        for h in heads:
            lb = jnp.zeros((SUBLANES, tq), F32)
            for r in range(tk // rc):
                p = jnp.exp2(s_refs[h][r * rc:(r + 1) * rc, :] - m_new[h])
                lb = lb + _group_reduce(jnp.add, p)
                p_refs[h][r * rc:(r + 1) * rc, :] = p.astype(BF)
            ml_ref[2 * h + 1:2 * h + 2, :] = (alpha[h] * ml_ref[2 * h + 1:2 * h + 2, :]
                                              + jnp.sum(lb, axis=0, keepdims=True))
            if nxt is not None:
                keep(h, nxt[h])
        for h in heads:
            vh = vt_ref[0, h * HEAD_DIM:(h + 1) * HEAD_DIM, pl.ds(k0, tk)]
            acc_refs[h][...] = alpha[h] * acc_refs[h][...] + _dot(vh, p_refs[h][...])

    is_max = lax.broadcasted_iota(jnp.int32, (2 * n_heads, 1), 0) % 2 == 0
    ml_ref[...] = jnp.where(is_max, jnp.full((2 * n_heads, tq), NEG_INF, F32), 0.0)
    for h in heads:
        acc_refs[h][...] = jnp.zeros((HEAD_DIM, tq), F32)

    @pl.when(qi == 0)
    def _():
        for h, s in enumerate(scores(0, True)):
            keep(h, s)

    @pl.when(qi > 0)
    def _():
        for h, s in enumerate(scores(0, False)):
            keep(h, s)

        def body(j, _):
            consume(j, scores(j + 1, False))
            return 0

        lax.fori_loop(0, qi - 1, body, 0)
        consume(qi - 1, scores(qi, True))

    consume(qi)
    for pair in range(n_heads // 2):
        h0, h1 = 2 * pair, 2 * pair + 1
        out_t = jnp.concatenate([acc_refs[h0][...] / ml_ref[2 * h0 + 1:2 * h0 + 2, :],
                                 acc_refs[h1][...] / ml_ref[2 * h1 + 1:2 * h1 + 2, :]], axis=0)
        o_ref[0, :, pair * LANES:(pair + 1) * LANES] = out_t.T.astype(o_ref.dtype)


def _attn(qt, kb, vtb, lf, lft, tq):
    b, w, t = qt.shape
    n_heads = w // HEAD_DIM
    n_pairs = n_heads // 2
    sel = np.zeros((3 * LANES, n_pairs * LANES), np.float32)
    one = np.zeros((1, n_pairs * LANES), np.float32)
    for pr in range(n_pairs):
        for hf in range(2):
            for piece in range(3):
                sel[piece * LANES + 2 * pr + hf, pr * LANES + 3 * hf + piece] = -1.0
        one[0, pr * LANES + 6:pr * LANES + 9] = 1.0
    kern = functools.partial(_attn_kernel, tq=tq, n_heads=n_heads)
    per_b = lambda s1, s2: pl.BlockSpec((1, s1, s2), lambda i, j: (i, 0, 0))
    return pl.pallas_call(
        kern,
        grid=(b, t // tq),
        in_specs=[pl.BlockSpec((1, w, tq), lambda i, j: (i, 0, j)),
                  per_b(t, w), per_b(w, t), per_b(t, LANES), per_b(n_heads, t),
                  _const_spec(sel.shape), _const_spec(one.shape)],
        out_specs=pl.BlockSpec((1, tq, w), lambda i, j: (i, j, 0)),
        out_shape=jax.ShapeDtypeStruct((b, t, w), BF),
        scratch_shapes=[pltpu.VMEM((n_heads, t), F32),
                        pltpu.VMEM((n_pairs, t, 2 * LANES), BF),
                        pltpu.VMEM((n_heads, 2 * LANES, tq), BF),
                        pltpu.VMEM((2 * n_heads, tq), F32), pltpu.VMEM((n_heads, tq), F32)]
                       + [pltpu.VMEM((tq, tq), F32)] * n_heads
                       + [pltpu.VMEM((tq, tq), BF)] * n_heads
                       + [pltpu.VMEM((HEAD_DIM, tq), F32)] * n_heads,
        compiler_params=_cparams(("parallel", "arbitrary")),
        name="attn_prompt",
    )(qt, kb, vtb, lf, lft, jnp.asarray(sel, BF), jnp.asarray(one, F32))


def _page_copies(step, slot, ng, pages, pt_ref, ck_hbm, cv_hbm, kbuf, vbuf, sem):
    bb = step // ng
    first = (ng - 1 - step % ng) * pages
    out = []
    for i in range(pages):
        pid = pt_ref[bb, first + i]
        out.append(pltpu.make_async_copy(ck_hbm.at[pid], kbuf.at[slot, i], sem.at[slot, 0]))
        out.append(pltpu.make_async_copy(cv_hbm.at[pid], vbuf.at[slot, i], sem.at[slot, 1]))
    return out


def _paged_step(step, n, ng, total, pt_ref, q_ref, kn_ref, vn_ref, lf_ref, lft_ref, tri_ref,
                clf_ref, ck_hbm, cv_hbm, o_ref, qbd_ref, ci_ref, m_ref, l_ref, acc_ref, carry_ref,
                kbuf, vbuf, sem, *, n_heads, pages, tq):
    b = step // ng
    g = step % ng
    mp = n * pages
    first_page = (ng - g - n) * pages
    rows = n_heads * tq
    w = q_ref.shape[2]
    dma = (pt_ref, ck_hbm, cv_hbm, kbuf, vbuf, sem)
    order = [(k, i) for k in reversed(range(n)) for i in range(pages)]

    for k in range(n):
        for cp in _page_copies(step + k, k, ng, pages, *dma):
            cp.wait()

    def expand(x):
        return jnp.concatenate(
            [jnp.broadcast_to(x[h:h + 1, :], (tq, x.shape[1])) for h in range(n_heads)], axis=0)

    def update(s, pv_fn):
        m = m_ref[...]
        m_new = jnp.maximum(m, jnp.max(s, axis=-1, keepdims=True))
        alpha = jnp.exp(m - m_new)
        p = jnp.exp(s - m_new)
        l_ref[...] = alpha * l_ref[...] + jnp.sum(p, axis=-1, keepdims=True)
        acc_ref[...] = alpha * acc_ref[...] + pv_fn(p.astype(BF))
        m_ref[...] = m_new

    @pl.when(g == 0)
    def _():
        q = q_ref[0]
        lane = lax.broadcasted_iota(jnp.int32, (1, w), 1)
        zero = jnp.zeros_like(q)
        qbd_ref[...] = jnp.concatenate(
            [jnp.where((lane >= h * HEAD_DIM) & (lane < (h + 1) * HEAD_DIM), q, zero)
             for h in range(n_heads)], axis=0)
        c = _scan_incl(lf_ref[0], 0)
        ci_ref[...] = jnp.concatenate([c[:, h:h + 1] for h in range(n_heads)], axis=0)
        m_ref[...] = jnp.full((rows, 1), NEG_INF, F32)
        l_ref[...] = jnp.zeros((rows, 1), F32)
        acc_ref[...] = jnp.zeros((rows, w), F32)
        carry_ref[...] = jnp.zeros((n_heads, PAGE_SIZE), F32)
        cj = _scan_incl(lft_ref[0], 1)
        s = _dot_nt(qbd_ref[...], kn_ref[0]) + ci_ref[...] - expand(cj)
        r_idx = lax.broadcasted_iota(jnp.int32, (rows, LANES), 0)
        c_idx = lax.broadcasted_iota(jnp.int32, (rows, LANES), 1)
        s = jnp.where(c_idx <= jnp.bitwise_and(r_idx, tq - 1), s, NEG_INF)
        update(s, lambda p: _dot(p, vn_ref[0]))

    ktc = jnp.concatenate([kbuf[k, i].reshape(w, PAGE_SIZE) for k, i in order], axis=1).astype(BF)
    vtc = jnp.concatenate([vbuf[k, i].reshape(w, PAGE_SIZE) for k, i in order], axis=1).astype(BF)
    lf = jnp.concatenate([clf_ref[pt_ref[b, first_page + i]] for i in range(mp)], axis=0)
    hi, mid, lo = _split3(lf)
    y = _dot(jnp.concatenate([hi, mid, lo], axis=0), tri_ref[...])
    ph = mp * n_heads
    y = y[0:ph] + y[ph:2 * ph] + y[2 * ph:3 * ph]
    run = carry_ref[...]
    cjs = [None] * mp
    for pg in reversed(range(mp)):
        blk = slice(pg * n_heads, (pg + 1) * n_heads)
        cjs[pg] = -(run + (y[blk, 0:PAGE_SIZE] - lf[blk]))
        run = run + y[blk, PAGE_SIZE:2 * PAGE_SIZE]
    carry_ref[...] = run
    cj = jnp.concatenate(cjs, axis=1)
    s = _dot(qbd_ref[...], ktc) + ci_ref[...] - expand(cj)
    update(s, lambda p: _dot_nt(p, vtc))

    @pl.when(g + n == ng)
    def _():
        lane = lax.broadcasted_iota(jnp.int32, (1, w), 1)
        out = jnp.zeros((tq, w), F32)
        for h in range(n_heads):
            sl = slice(h * tq, (h + 1) * tq)
            val = acc_ref[sl, :] / l_ref[sl, :]
            out = jnp.where((lane >= h * HEAD_DIM) & (lane < (h + 1) * HEAD_DIM), val, out)
        o_ref[0] = out.astype(o_ref.dtype)

    @pl.when(step + n < total)
    def _():
        for k in range(n):
            for cp in _page_copies(step + n + k, k, ng, pages, *dma):
                cp.start()


def _paged_operands(q, cache_kt, cache_lft, pages, slots, seq_of_step):
    db, tq, w = q.shape
    n_heads = w // HEAD_DIM
    rows = n_heads * tq
    assert tq & (tq - 1) == 0
    per_seq = lambda s1, s2: pl.BlockSpec((1, s1, s2), lambda i, j, pt: (seq_of_step(i, j), 0, 0))
    in_specs = [per_seq(tq, w), per_seq(PAGE_SIZE, w), per_seq(PAGE_SIZE, w), per_seq(tq, LANES),
                per_seq(n_heads, LANES),
                _const_spec((PAGE_SIZE, 2 * PAGE_SIZE)), _const_spec(cache_lft.shape),
                pl.BlockSpec(memory_space=pl.ANY), pl.BlockSpec(memory_space=pl.ANY)]
    page_buf = pltpu.VMEM((slots, pages, n_heads, HEAD_DIM, PAGE_SIZE), cache_kt.dtype)
    scratch = [pltpu.VMEM((rows, w), BF), pltpu.VMEM((rows, 1), F32),
               pltpu.VMEM((rows, 1), F32), pltpu.VMEM((rows, 1), F32),
               pltpu.VMEM((rows, w), F32), pltpu.VMEM((n_heads, PAGE_SIZE), F32),
               page_buf, page_buf, pltpu.SemaphoreType.DMA((slots, 2))]
    pos = lax.broadcasted_iota(jnp.int32, (PAGE_SIZE, PAGE_SIZE), 0)
    col = lax.broadcasted_iota(jnp.int32, (PAGE_SIZE, PAGE_SIZE), 1)
    tri = jnp.concatenate([(pos >= col).astype(BF), jnp.ones((PAGE_SIZE, PAGE_SIZE), BF)], axis=1)
    return in_specs, per_seq(tq, w), scratch, tri


def _mix_tail(pooled, att, x, wp_ref, ps_ref, wo_ref, g_ref, b_ref, alpha):
    gd = pooled[0].shape[1]
    att_w = att.shape[1]
    mixed = [_dot(pooled[g].astype(BF), wp_ref[g]) * ps_ref[:, g * gd:(g + 1) * gd]
             for g in range(len(pooled))]
    pool = jnp.concatenate(mixed, axis=1).astype(BF)
    mix = _dot(att, wo_ref[0:att_w, :]) + _dot(pool, wo_ref[att_w:, :])
    return _layer_norm(alpha * x + mix, g_ref[...], b_ref[...])


def _shift_down(x, s):
    n8, c = x.shape[0] // SUBLANES, x.shape[1]
    if s == SUBLANES:
        return jnp.concatenate([x[0:SUBLANES], x[:-SUBLANES]], axis=0)
    r = pltpu.roll(x.reshape(n8, SUBLANES, c), s, 1)
    prev = jnp.concatenate([r[0:1], r[:-1]], axis=0)
    sub = lax.broadcasted_iota(jnp.int32, (1, SUBLANES, c), 1)
    return jnp.where(sub < s, prev, r).reshape(n8 * SUBLANES, c)


def _mix_prompt_kernel(x_ref, att_ref, u_ref, halo_ref, wp_ref, ps_ref, wo_ref, g_ref, b_ref,
                       o_ref, *, tm, sub, alpha):
    t = pl.program_id(1)
    hl = MAX_WINDOW
    started = (t > 0).astype(F32)
    gd = u_ref.shape[2] // len(POOL_WINDOWS)
    att_w = att_ref.shape[2]
    blocks = [slice(r0, r0 + sub) for r0 in range(0, tm, sub)]
    pooled = []
    for rows in blocks:
        pos = t * tm + rows.start + lax.broadcasted_iota(jnp.int32, (sub, 1), 0)
        per_group = []
        for g, w in enumerate(POOL_WINDOWS):
            lanes = slice(g * gd, (g + 1) * gd)
            tok = u_ref[0, rows, lanes]
            halo = (halo_ref[0, :, lanes] * started if rows.start == 0
                    else u_ref[0, rows.start - hl:rows.start, lanes])
            win = jnp.concatenate([halo, tok], axis=0)
            k = 1
            while k < w:
                win = win + _shift_down(win, k)
                k *= 2
            cnt = jnp.minimum(pos + 1, w).astype(F32)
            per_group.append(win[hl:] / cnt - tok)
        pooled.append(per_group)
    pools = [jnp.concatenate([_dot(pg[g].astype(BF), wp_ref[g]) * ps_ref[:, g * gd:(g + 1) * gd]
                              for g in range(len(pg))], axis=1).astype(BF) for pg in pooled]
    mixes = [_dot(att_ref[0, rows, :], wo_ref[0:att_w, :]) + _dot(pool, wo_ref[att_w:, :])
             for rows, pool in zip(blocks, pools)]
    for rows, mix in zip(blocks, mixes):
        o_ref[0, rows, :] = _layer_norm(alpha * x_ref[0, rows, :] + mix, g_ref[...], b_ref[...])


def _mix_prompt(x, att, u, w_pool, pool_scale, w_out, g, b, alpha, tm):
    bsz, t, d = x.shape
    att_w, pool_w = att.shape[2], u.shape[2]
    hl = MAX_WINDOW
    ng, gd = w_pool.shape[0], w_pool.shape[1]
    kern = functools.partial(_mix_prompt_kernel, tm=tm, sub=min(tm, 256), alpha=alpha)
    row = lambda w: pl.BlockSpec((1, tm, w), lambda i, j: (i, j, 0))
    halo = pl.BlockSpec((1, hl, pool_w), lambda i, j: (i, jnp.maximum(j * (tm // hl) - 1, 0), 0))
    return pl.pallas_call(
        kern,
        grid=(bsz, t // tm),
        in_specs=[row(d), row(att_w), row(pool_w), halo,
                  _const_spec((ng, gd, gd)), _const_spec((1, pool_w)), _const_spec((att_w + pool_w, d)),
                  _const_spec((1, d)), _const_spec((1, d))],
        out_specs=row(d),
        out_shape=jax.ShapeDtypeStruct((bsz, t, d), F32),
        compiler_params=_cparams(("parallel", "parallel")),
        name="mix_prompt",
    )(x, att, u, u, w_pool, pool_scale, w_out, g, b)


def _mix_sample_kernel(x_ref, att_ref, full_ref, wp_ref, ps_ref, wo_ref, g_ref, b_ref, o_ref,
                       *, ts, n_past, alpha):
    nseq = full_ref.shape[0]
    hl = MAX_WINDOW
    gd = full_ref.shape[2] // len(POOL_WINDOWS)
    pos = n_past + lax.broadcasted_iota(jnp.int32, (1, ts, 1), 1)
    pooled = []
    for g, w in enumerate(POOL_WINDOWS):
        lanes = slice(g * gd, (g + 1) * gd)
        tok = full_ref[:, hl:hl + ts, lanes]
        acc = tok
        for k in range(1, w):
            acc = acc + full_ref[:, hl - k:hl - k + ts, lanes]
        cnt = jnp.minimum(pos + 1, w).astype(F32)
        pooled.append((acc / cnt - tok).reshape(nseq * ts, gd))
    o_ref[...] = _mix_tail(pooled, att_ref[...], x_ref[...], wp_ref, ps_ref, wo_ref, g_ref, b_ref, alpha)


def _mix_sample(x2, att2, full, w_pool, pool_scale, w_out, g, b, alpha, ts, n_past):
    n, d = x2.shape
    kern = functools.partial(_mix_sample_kernel, ts=ts, n_past=n_past, alpha=alpha)
    return pl.pallas_call(
        kern,
        out_shape=jax.ShapeDtypeStruct((n, d), F32),
        compiler_params=pltpu.CompilerParams(vmem_limit_bytes=VMEM_LIMIT),
        name="mix_sample",
    )(x2, att2, full, w_pool, pool_scale, w_out, g, b)


def _shift_rows(h, s):
    n8, c = h.shape[0] // SUBLANES, h.shape[1]
    r = pltpu.roll(h.reshape(n8, SUBLANES, c), s, 1)
    sub = lax.broadcasted_iota(jnp.int32, (1, SUBLANES, c), 1)
    return jnp.where(sub < s, r[:-1], r[1:]).reshape((n8 - 1) * SUBLANES, c)


def _ffn_chunk(hg3, hv3, wc_ref, bc_ref, gcols, vcols):
    def conv(h3, cols, scale):
        hc = bc_ref[:, cols] * scale + h3[0] * (wc_ref[0:1, cols] * scale)
        for j in range(1, CONV_WIDTH):
            hc = hc + h3[j] * (wc_ref[j:j + 1, cols] * scale)
        return hc
    g = conv(hg3, gcols, 1.0)
    vh = conv(hv3, vcols, 0.5)
    t = jnp.tanh(g * (GELU_C1 + (GELU_C1 * GELU_C0) * (g * g)))
    return ((g * (1.0 + t)) * vh).astype(BF)


def _ffn_prompt_kernel(pt_ref, x_ref, halo_ref, wu_ref, bu_ref, wc_ref, bc_ref, wd_ref, g_ref, b_ref,
                       *rest, tm, d_ff, chunk, alpha, paged):
    sub, ng, total = paged["sub"], paged["ng"], paged["total"]
    pa_in, (o_ref, tail_ref, oa_ref), pa_scratch = rest[0:9], rest[9:12], rest[12:]
    dma = (pt_ref, pa_in[7], pa_in[8], pa_scratch[6], pa_scratch[7], pa_scratch[8])
    step0 = (pl.program_id(0) * pl.num_programs(1) + pl.program_id(1)) * sub

    @pl.when(step0 == 0)
    def _():
        for k in range(sub):
            for cp in _page_copies(k, k, ng, paged["pages"], *dma):
                cp.start()

    t = pl.program_id(1)
    hl = SUBLANES
    n_chunks = d_ff // chunk
    cols_of = lambda c: (slice(c * chunk, (c + 1) * chunk), slice(d_ff + c * chunk, d_ff + (c + 1) * chunk))
    st = {}

    def up(c):
        return [_dot(st["xin"], wu_ref[:, cols]) for cols in cols_of(c)]

    def shifted(hd, cols):
        body = hd[hl:] + bu_ref[:, cols]
        tail_ref[0, :, cols] = body[tm - hl:tm]
        h = jnp.concatenate([(hd[0:hl] + bu_ref[:, cols]) * st["started"], body], axis=0)
        return [_shift_rows(h, CONV_WIDTH - 1 - j) for j in range(CONV_WIDTH - 1)] + [body]

    def run_chunks(c_from, c_to):
        for c in range(c_from, c_to):
            cur = st["nxt"]
            if c + 1 < n_chunks:
                st["nxt"] = up(c + 1)
            gcols, vcols = cols_of(c)
            act = _ffn_chunk(shifted(cur[0], gcols), shifted(cur[1], vcols), wc_ref, bc_ref, gcols, vcols)
            st["acc"] = st["acc"] + _dot(act, wd_ref[gcols, :])

    _paged_step(step0, sub, ng, total, pt_ref, *pa_in, oa_ref, *pa_scratch,
                n_heads=paged["n_heads"], pages=paged["pages"], tq=paged["tq"])

    st["x"] = x_ref[0]
    st["xin"] = jnp.concatenate([halo_ref[0], st["x"]], axis=0).astype(BF)
    st["started"] = (t > 0).astype(F32)
    st["acc"] = jnp.zeros((tm, st["x"].shape[1]), F32)
    st["nxt"] = up(0)
    run_chunks(0, n_chunks)
    o_ref[0] = _layer_norm(alpha * st["x"] + st["acc"], g_ref[...], b_ref[...])


def _ffn_prompt_and_paged_attn(x1, w_up, b_up, w_conv, b_conv, w_down, g, b, alpha, tm, chunk,
                               page_table, q, kn, vn, lf, lft, cache_kt, cache_vt, cache_lft):
    bsz, t, d = x1.shape
    d_ff = w_down.shape[0]
    hl = SUBLANES
    nt = t // tm
    db, tq, w = q.shape
    n_pages = page_table.shape[1]
    fits = lambda p: (n_pages % p == 0 and (db * (n_pages // p)) % (bsz * nt) == 0
                      and (n_pages // p) % ((db * (n_pages // p)) // (bsz * nt)) == 0)
    pages = next(p for p in (16, 8, 4, 2, 1) if fits(p))
    ng = n_pages // pages
    total = db * ng
    sub = total // (bsz * nt)
    paged = dict(sub=sub, ng=ng, total=total, pages=pages, n_heads=w // HEAD_DIM, tq=tq)
    pa_specs, pa_out, pa_scratch, tri = _paged_operands(
        q, cache_kt, cache_lft, pages, sub, lambda i, j: ((i * nt + j) * sub) // ng)
    kern = functools.partial(_ffn_prompt_kernel, tm=tm, d_ff=d_ff, chunk=chunk, alpha=alpha, paged=paged)
    row = pl.BlockSpec((1, tm, d), lambda i, j, pt: (i, j, 0))
    halo = pl.BlockSpec((1, hl, d), lambda i, j, pt: (i, jnp.maximum(j * (tm // hl) - 1, 0), 0))
    grid_spec = pltpu.PrefetchScalarGridSpec(
        num_scalar_prefetch=1,
        grid=(bsz, nt),
        in_specs=[row, halo, _const_spec((d, 2 * d_ff)), _const_spec((1, 2 * d_ff)),
                  _const_spec((CONV_WIDTH, 2 * d_ff)), _const_spec((1, 2 * d_ff)),
                  _const_spec((d_ff, d)), _const_spec((1, d)), _const_spec((1, d))] + pa_specs,
        out_specs=[row, pl.BlockSpec((1, hl, 2 * d_ff), lambda i, j, pt: (i, 0, 0)), pa_out],
        scratch_shapes=pa_scratch)
    return pl.pallas_call(
        kern,
        grid_spec=grid_spec,
        out_shape=[jax.ShapeDtypeStruct((bsz, t, d), F32),
                   jax.ShapeDtypeStruct((bsz, hl, 2 * d_ff), F32),
                   jax.ShapeDtypeStruct((db, tq, w), BF)],
        compiler_params=_cparams(("arbitrary", "arbitrary")),
        name="ffn_prompt_attn_paged",
    )(page_table, x1, x1, w_up, b_up, w_conv, b_conv, w_down, g, b,
      q, kn, vn, lf, lft, tri, cache_lft, cache_kt, cache_vt)


def _ffn_sample_kernel(x_ref, st_ref, wu_ref, bu_ref, wc_ref, bc_ref, wd_ref, g_ref, b_ref,
                       o_ref, h_ref, buf_ref, *, ts, d_ff, chunk, alpha):
    nseq = st_ref.shape[0]
    n = nseq * ts
    hl = SUBLANES
    nst = CONV_WIDTH - 1
    x = x_ref[...]
    xb = x.astype(BF)
    acc = jnp.zeros((n, x.shape[1]), F32)
    for c in range(d_ff // chunk):
        gcols = slice(c * chunk, (c + 1) * chunk)
        vcols = slice(d_ff + c * chunk, d_ff + (c + 1) * chunk)
        h3 = []
        for cols in (gcols, vcols):
            h = _dot(xb, wu_ref[:, cols]) + bu_ref[:, cols]
            h_ref[:, cols] = h
            buf_ref[:, hl - nst:hl, :] = st_ref[:, :, cols]
            buf_ref[:, hl:hl + ts, :] = h.reshape(nseq, ts, chunk)
            h3.append([buf_ref[:, hl - 2 + j:hl - 2 + j + ts, :].reshape(n, chunk)
                       for j in range(CONV_WIDTH)])
        act = _ffn_chunk(h3[0], h3[1], wc_ref, bc_ref, gcols, vcols)
        acc = acc + _dot(act, wd_ref[gcols, :])
    o_ref[...] = _layer_norm(alpha * x + acc, g_ref[...], b_ref[...])


def _ffn_sample(x2, state, w_up, b_up, w_conv, b_conv, w_down, g, b, alpha, ts, chunk):
    n, d = x2.shape
    d_ff = w_down.shape[0]
    nseq = n // ts
    kern = functools.partial(_ffn_sample_kernel, ts=ts, d_ff=d_ff, chunk=chunk, alpha=alpha)
    return pl.pallas_call(
        kern,
        out_shape=[jax.ShapeDtypeStruct((n, d), F32), jax.ShapeDtypeStruct((n, 2 * d_ff), F32)],
        scratch_shapes=[pltpu.VMEM((nseq, SUBLANES + ts, chunk), F32)],
        compiler_params=pltpu.CompilerParams(vmem_limit_bytes=VMEM_LIMIT),
        name="ffn_sample",
    )(x2, state, w_up, b_up, w_conv, b_conv, w_down, g, b)


def _ffn_chunk_size(d_ff):
    for c in (512, 256, 128):
        if d_ff % c == 0:
            return c
    raise ValueError("d_ff must be a multiple of 128")


def kernel(x_prompt, x_sample, cache_k, cache_v, cache_logf, state_pool, state_conv, page_table,
           w_in, b_f, w_pool, pool_scale, w_out, ln1_g, ln1_b, w_up, b_up, w_conv, b_conv, w_down,
           ln2_g, ln2_b):
    bsz, t_p, d = x_prompt.shape
    db, t_s, _ = x_sample.shape
    depth = w_in.shape[0]
    n_heads = cache_k.shape[3]
    att_w = n_heads * HEAD_DIM
    pool_w = state_pool.shape[3]
    d_ff = w_down.shape[1]
    n_past = page_table.shape[1] * PAGE_SIZE
    alpha = (2 * depth) ** 0.25
    assert HEAD_DIM * 2 == LANES and n_heads % 2 == 0 and n_heads <= SUBLANES
    assert t_s == SUBLANES and cache_k.shape[2] == PAGE_SIZE and t_p >= MAX_WINDOW
    assert w_in.shape[2] == 3 * att_w + n_heads + pool_w

    tm_proj = min(512, t_p)
    tq = min(256, t_p)
    tm_mix = min(512, t_p)
    tm_ffn = min(256, t_p)
    chunk = _ffn_chunk_size(d_ff)

    yp, ys = x_prompt, x_sample
    outs = [[] for _ in range(10)]
    for l in range(depth):
        wi = w_in[l]
        fcols = wi[:, 3 * att_w:3 * att_w + n_heads]
        w_row = jnp.concatenate(
            [wi[:, :2 * att_w], wi[:, 3 * att_w + n_heads:], jnp.pad(fcols, ((0, 0), (0, LANES - n_heads)))],
            axis=1).astype(BF)
        w_t = jnp.concatenate(
            [wi[:, :3 * att_w].T, jnp.pad(fcols.T, ((0, BF_ROWS - n_heads), (0, 0)))],
            axis=0).astype(BF)
        bf_row = jnp.pad(b_f[l], (0, LANES - n_heads)).reshape(1, LANES)
        bf_col = b_f[l].reshape(n_heads, 1)
        wp_b = w_pool[l].astype(BF)
        ps = pool_scale[l].reshape(1, pool_w)
        wo_b = w_out[l].astype(BF)
        g1, b1 = ln1_g[l].reshape(1, d), ln1_b[l].reshape(1, d)
        wu_b = w_up[l].astype(BF)
        bu = b_up[l].reshape(1, 2 * d_ff)
        wc = w_conv[l]
        bc = b_conv[l].reshape(1, 2 * d_ff)
        wd_b = w_down[l].astype(BF)
        g2, b2 = ln2_g[l].reshape(1, d), ln2_b[l].reshape(1, d)

        qt, kb, kt, vt, vtb, u, lf, lft = _proj_prompt(yp, w_row, w_t, bf_row, bf_col, att_w, pool_w,
                                                       n_heads, tm_proj)
        n_s = db * t_s
        qs, ks, vs, kbs, vbs, us, lfs = _proj_sample(ys.reshape(n_s, d), w_row, w_t, bf_row, att_w, pool_w)
        att = _attn(qt, kb, vtb, lf, lft, tq)
        x1 = _mix_prompt(yp, att, u, wp_b, ps, wo_b, g1, b1, alpha, tm_mix)

        lfs3 = lfs.reshape(db, t_s, LANES)
        lfts = jnp.pad(jnp.swapaxes(lfs3[:, :, :n_heads], 1, 2), ((0, 0), (0, 0), (0, LANES - t_s)))
        padk = lambda a: jnp.pad(a.reshape(db, t_s, att_w), ((0, 0), (0, PAGE_SIZE - t_s), (0, 0)))
        ckt = jnp.transpose(cache_k[l], (0, 2, 3, 1))
        cvt = jnp.transpose(cache_v[l], (0, 2, 3, 1))
        clft = jnp.swapaxes(cache_logf[l], 1, 2)
        yp, tail, att_s = _ffn_prompt_and_paged_attn(
            x1, wu_b, bu, wc, bc, wd_b, g2, b2, alpha, tm_ffn, chunk,
            page_table, qs.reshape(db, t_s, att_w), padk(kbs), padk(vbs), lfs3, lfts, ckt, cvt, clft)
        to_bthd = lambda a: jnp.transpose(a.reshape(bsz, n_heads, HEAD_DIM, t_p), (0, 3, 1, 2))
        outs[0].append(to_bthd(kt))
        outs[1].append(to_bthd(vt))
        outs[2].append(jnp.swapaxes(lft, 1, 2))
        outs[3].append(u[:, t_p - (MAX_WINDOW - 1):, :])
        outs[4].append(tail[:, SUBLANES - (CONV_WIDTH - 1):, :])

        us3 = us.reshape(db, t_s, pool_w)
        full = jnp.concatenate([jnp.zeros((db, 1, pool_w), F32), state_pool[l], us3], axis=1)
        x1s = _mix_sample(ys.reshape(n_s, d), att_s.reshape(n_s, att_w), full, wp_b, ps, wo_b, g1, b1,
                          alpha, t_s, n_past)
        ys2, hs = _ffn_sample(x1s, state_conv[l], wu_b, bu, wc, bc, wd_b, g2, b2, alpha, t_s, chunk)
        ys = ys2.reshape(db, t_s, d)
        outs[5].append(ks.reshape(db, t_s, n_heads, HEAD_DIM))
        outs[6].append(vs.reshape(db, t_s, n_heads, HEAD_DIM))
        outs[7].append(lfs3[:, :, :n_heads])
        outs[8].append(full[:, t_s + 1:, :])
        outs[9].append(hs.reshape(db, t_s, 2 * d_ff)[:, t_s - (CONV_WIDTH - 1):, :])

    st = [jnp.stack(o, 0) for o in outs]
    return (yp, ys, st[0], st[1], st[2], st[3], st[4], st[5], st[6], st[7], st[8], st[9])
```

```python
import functools

import jax
import jax.numpy as jnp
import numpy as np
from jax import lax
from jax.experimental import pallas as pl
from jax.experimental.pallas import tpu as pltpu

BF = jnp.bfloat16
F32 = jnp.float32

HEAD_DIM = 64
PAGE_SIZE = 128
POOL_WINDOWS = (2, 4, 8, 16)
MAX_WINDOW = max(POOL_WINDOWS)
CONV_WIDTH = 3
LN_EPS = 1e-5
NEG_INF = -1e30
GELU_C1 = float(np.float32(np.sqrt(2.0 / np.pi)))
GELU_C0 = float(np.float32(0.044715))
LOG2E = float(np.log2(np.e))
LANES = 128
SUBLANES = 8
BF_ROWS = 16
VMEM_LIMIT = 56 * 1024 * 1024


def _cparams(sem):
    return pltpu.CompilerParams(dimension_semantics=sem, vmem_limit_bytes=VMEM_LIMIT)


def _const_spec(shape):
    nd = len(shape)
    return pl.BlockSpec(shape, lambda *_: (0,) * nd, pipeline_mode=pl.Buffered(1))


def _dot(a, b):
    return jnp.dot(a, b, preferred_element_type=F32)


def _dot_nt(a, b):
    return lax.dot_general(a, b, (((1,), (1,)), ((), ())), preferred_element_type=F32)


def _layer_norm(r, g, b):
    mu = jnp.mean(r, axis=-1, keepdims=True)
    d = r - mu
    var = jnp.mean(d * d, axis=-1, keepdims=True)
    return d * lax.rsqrt(var + LN_EPS) * g + b


def _scan_incl(x, axis, reverse=False):
    n = x.shape[axis]
    idx = lax.broadcasted_iota(jnp.int32, x.shape, axis)
    s = 1
    while s < n:
        if reverse:
            x = x + jnp.where(idx + s < n, pltpu.roll(x, n - s, axis), 0.0)
        else:
            x = x + jnp.where(idx >= s, pltpu.roll(x, s, axis), 0.0)
        s *= 2
    return x


def _proj_prompt_kernel(x_ref, wr_ref, wt_ref, bfr_ref, bfc_ref,
                        qt_ref, kb_ref, kt_ref, vt_ref, vtb_ref, u_ref, lf_ref, lft_ref,
                        *, att_w, pool_w, n_heads, scale):
    xb = x_ref[0].astype(BF)
    u_ref[0] = _dot(xb, wr_ref[:, 2 * att_w:2 * att_w + pool_w])
    z = _dot(xb, wr_ref[:, 2 * att_w + pool_w:2 * att_w + pool_w + LANES]) + bfr_ref[...]
    lf_ref[0] = jax.nn.log_sigmoid(z)
    qt_ref[0] = (_dot_nt(wt_ref[0:att_w, :], xb) * scale).astype(BF)
    kt = _dot_nt(wt_ref[att_w:2 * att_w, :], xb)
    kt_ref[0] = kt
    kb_ref[0] = kt.T.astype(BF)
    vt = _dot_nt(wt_ref[2 * att_w:3 * att_w, :], xb)
    vt_ref[0] = vt
    vtb_ref[0] = vt.astype(BF)
    zt = _dot_nt(wt_ref[3 * att_w:3 * att_w + BF_ROWS, :], xb)[0:n_heads] + bfc_ref[...]
    lft_ref[0] = jax.nn.log_sigmoid(zt)


def _proj_prompt(x, w_row, w_t, bf_row, bf_col, att_w, pool_w, n_heads, tm):
    bsz, t, d = x.shape
    kern = functools.partial(_proj_prompt_kernel, att_w=att_w, pool_w=pool_w, n_heads=n_heads,
                             scale=HEAD_DIM ** -0.5 * LOG2E)
    row = lambda w: pl.BlockSpec((1, tm, w), lambda i, j: (i, j, 0))
    col = lambda h: pl.BlockSpec((1, h, tm), lambda i, j: (i, 0, j))
    return pl.pallas_call(
        kern,
        grid=(bsz, t // tm),
        in_specs=[row(d), _const_spec(w_row.shape), _const_spec(w_t.shape),
                  _const_spec((1, LANES)), _const_spec((n_heads, 1))],
        out_specs=[col(att_w), row(att_w), col(att_w), col(att_w), col(att_w), row(pool_w),
                   row(LANES), col(n_heads)],
        out_shape=[jax.ShapeDtypeStruct((bsz, att_w, t), BF),
                   jax.ShapeDtypeStruct((bsz, t, att_w), BF),
                   jax.ShapeDtypeStruct((bsz, att_w, t), F32),
                   jax.ShapeDtypeStruct((bsz, att_w, t), F32),
                   jax.ShapeDtypeStruct((bsz, att_w, t), BF),
                   jax.ShapeDtypeStruct((bsz, t, pool_w), F32),
                   jax.ShapeDtypeStruct((bsz, t, LANES), F32),
                   jax.ShapeDtypeStruct((bsz, n_heads, t), F32)],
        compiler_params=_cparams(("parallel", "parallel")),
        name="proj_prompt",
    )(x, w_row, w_t, bf_row, bf_col)


def _proj_sample_kernel(x_ref, wr_ref, wt_ref, bfr_ref, q_ref, k_ref, v_ref, kb_ref, vb_ref, u_ref, lf_ref,
                        *, att_w, pool_w, scale):
    xb = x_ref[...].astype(BF)
    q_ref[...] = (_dot(xb, wr_ref[:, 0:att_w]) * scale).astype(BF)
    u_ref[...] = _dot(xb, wr_ref[:, 2 * att_w:2 * att_w + pool_w])
    z = _dot(xb, wr_ref[:, 2 * att_w + pool_w:2 * att_w + pool_w + LANES]) + bfr_ref[...]
    lf_ref[...] = jax.nn.log_sigmoid(z)
    k = _dot(xb, wr_ref[:, att_w:2 * att_w])
    k_ref[...] = k
    kb_ref[...] = k.astype(BF)
    v = _dot_nt(xb, wt_ref[2 * att_w:3 * att_w, :])
    v_ref[...] = v
    vb_ref[...] = v.astype(BF)


def _proj_sample(x2, w_row, w_t, bf_row, att_w, pool_w):
    n, d = x2.shape
    kern = functools.partial(_proj_sample_kernel, att_w=att_w, pool_w=pool_w, scale=HEAD_DIM ** -0.5)
    return pl.pallas_call(
        kern,
        out_shape=[jax.ShapeDtypeStruct((n, att_w), BF),
                   jax.ShapeDtypeStruct((n, att_w), F32),
                   jax.ShapeDtypeStruct((n, att_w), F32),
                   jax.ShapeDtypeStruct((n, att_w), BF),
                   jax.ShapeDtypeStruct((n, att_w), BF),
                   jax.ShapeDtypeStruct((n, pool_w), F32),
                   jax.ShapeDtypeStruct((n, LANES), F32)],
        compiler_params=pltpu.CompilerParams(vmem_limit_bytes=VMEM_LIMIT),
        name="proj_sample",
    )(x2, w_row, w_t, bf_row)


KEY_CHUNK = 32


def _group_reduce(fn, x):
    parts = [x[i * SUBLANES:(i + 1) * SUBLANES] for i in range(x.shape[0] // SUBLANES)]
    while len(parts) > 1:
        parts = [fn(parts[i], parts[i + 1]) if i + 1 < len(parts) else parts[i]
                 for i in range(0, len(parts), 2)]
    return parts[0]


AUG_ROWS = 16


def _split3(x):
    hi = x.astype(BF)
    r1 = x - hi.astype(F32)
    mid = r1.astype(BF)
    lo = (r1 - mid.astype(F32)).astype(BF)
    return hi, mid, lo


def _attn_kernel(qt_ref, k_ref, vt_ref, lf_ref, lft_ref, sel_ref, one_ref, o_ref,
                 ct_ref, kx_ref, qx_ref, ml_ref, mb_ref, *per_head, tq, n_heads):
    qi = pl.program_id(1)
    tk = tq
    rc = KEY_CHUNK
    t = k_ref.shape[1]
    n_pairs = n_heads // 2
    s_refs, p_refs, acc_refs = (per_head[i * n_heads:(i + 1) * n_heads] for i in range(3))

    @pl.when(qi == 0)
    def _():
        ct_ref[...] = _scan_incl(lft_ref[0], 1) * LOG2E
        blk = 512 if t % 512 == 0 else t
        for r0 in range(0, t, blk):
            rows = slice(r0, r0 + blk)
            c = _scan_incl(lf_ref[0, rows, :], 0)
            if r0:
                c = c + tot
            tot = c[blk - 1:blk, :]
            hi, mid, lo = _split3(c * LOG2E)
            aug = _dot(jnp.concatenate([hi, mid, lo], axis=1), sel_ref[...]) + one_ref[...]
            for pr in range(n_pairs):
                kx_ref[pr, rows, 0:LANES] = k_ref[0, rows, pr * LANES:(pr + 1) * LANES]
                kx_ref[pr, rows, LANES:2 * LANES] = aug[:, pr * LANES:(pr + 1) * LANES].astype(BF)
        qx_ref[:, LANES + AUG_ROWS:, :] = jnp.zeros((n_heads, LANES - AUG_ROWS, tq), BF)

    q0 = pl.multiple_of(qi * tq, tq)
    first = lax.broadcasted_iota(jnp.int32, (LANES, 1), 0) < HEAD_DIM
    causal = (lax.broadcasted_iota(jnp.int32, (tk, tq), 0)
              <= lax.broadcasted_iota(jnp.int32, (tk, tq), 1))
    arow = lax.broadcasted_iota(jnp.int32, (AUG_ROWS, 1), 0)

    heads = range(n_heads)
    for pair in range(n_pairs):
        qtp = qt_ref[0, pair * LANES:(pair + 1) * LANES, :]
        zero = jnp.zeros_like(qtp)
        qx_ref[2 * pair, 0:LANES, :] = jnp.where(first, qtp, zero)
        qx_ref[2 * pair + 1, 0:LANES, :] = jnp.where(first, zero, qtp)
    for h in heads:
        hf = h % 2
        ci = ct_ref[h:h + 1, pl.ds(q0, tq)]
        hi = ci.astype(BF).astype(F32)
        r1 = ci - hi
        mid = r1.astype(BF).astype(F32)
        lo = r1 - mid
        pick = ((arow >= 3 * hf) & (arow < 3 * hf + 3)).astype(F32)
        blk = jnp.where(arow == 6, hi, jnp.where(arow == 7, mid, jnp.where(arow == 8, lo, pick)))
        qx_ref[h, LANES:LANES + AUG_ROWS, :] = blk.astype(BF)

    def scores(jb, masked):
        k0 = pl.multiple_of(jb * tk, tk)
        out = []
        for h in heads:
            s = _dot(kx_ref[h // 2, pl.ds(k0, tk), :], qx_ref[h])
            out.append(jnp.where(causal, s, NEG_INF) if masked else s)
        return out

    def keep(h, s):
        s_refs[h][...] = s
        mb_ref[h:h + 1, :] = jnp.max(_group_reduce(jnp.maximum, s), axis=0, keepdims=True)

    def consume(jb, nxt=None):
        k0 = pl.multiple_of(jb * tk, tk)
        m_new, alpha = [], []
        for h in heads:
            m_old = ml_ref[2 * h:2 * h + 1, :]
            m_new.append(jnp.maximum(m_old, mb_ref[h:h + 1, :]))
            alpha.append(jnp.exp2(m_old - m_new[h]))
            ml_ref[2 * h:2 * h + 1, :] = m_new[h]
        for h in heads:
            lb = jnp.zeros((SUBLANES, tq), F32)
            for r in range(tk // rc):
                p = jnp.exp2(s_refs[h][r * rc:(r + 1) * rc, :] - m_new[h])
                lb = lb + _group_reduce(jnp.add, p)
                p_refs[h][r * rc:(r + 1) * rc, :] = p.astype(BF)
            ml_ref[2 * h + 1:2 * h + 2, :] = (alpha[h] * ml_ref[2 * h + 1:2 * h + 2, :]
                                              + jnp.sum(lb, axis=0, keepdims=True))
            if nxt is not None:
                keep(h, nxt[h])
        for h in heads:
            vh = vt_ref[0, h * HEAD_DIM:(h + 1) * HEAD_DIM, pl.ds(k0, tk)]
            acc_refs[h][...] = alpha[h] * acc_refs[h][...] + _dot(vh, p_refs[h][...])

    is_max = lax.broadcasted_iota(jnp.int32, (2 * n_heads, 1), 0) % 2 == 0
    ml_ref[...] = jnp.where(is_max, jnp.full((2 * n_heads, tq), NEG_INF, F32), 0.0)
    for h in heads:
        acc_refs[h][...] = jnp.zeros((HEAD_DIM, tq), F32)

    @pl.when(qi == 0)
    def _():
        for h, s in enumerate(scores(0, True)):
            keep(h, s)

    @pl.when(qi > 0)
    def _():
        for h, s in enumerate(scores(0, False)):
            keep(h, s)

        def body(j, _):
            consume(j, scores(j + 1, False))
            return 0

        lax.fori_loop(0, qi - 1, body, 0)
        consume(qi - 1, scores(qi, True))

    consume(qi)
    for pair in range(n_heads // 2):
        h0, h1 = 2 * pair, 2 * pair + 1
        out_t = jnp.concatenate([acc_refs[h0][...] / ml_ref[2 * h0 + 1:2 * h0 + 2, :],
                                 acc_refs[h1][...] / ml_ref[2 * h1 + 1:2 * h1 + 2, :]], axis=0)
        o_ref[0, :, pair * LANES:(pair + 1) * LANES] = out_t.T.astype(o_ref.dtype)


def _attn(qt, kb, vtb, lf, lft, tq):
    b, w, t = qt.shape
    n_heads = w // HEAD_DIM
    n_pairs = n_heads // 2
    sel = np.zeros((3 * LANES, n_pairs * LANES), np.float32)
    one = np.zeros((1, n_pairs * LANES), np.float32)
    for pr in range(n_pairs):
        for hf in range(2):
            for piece in range(3):
                sel[piece * LANES + 2 * pr + hf, pr * LANES + 3 * hf + piece] = -1.0
        one[0, pr * LANES + 6:pr * LANES + 9] = 1.0
    kern = functools.partial(_attn_kernel, tq=tq, n_heads=n_heads)
    per_b = lambda s1, s2: pl.BlockSpec((1, s1, s2), lambda i, j: (i, 0, 0))
    return pl.pallas_call(
        kern,
        grid=(b, t // tq),
        in_specs=[pl.BlockSpec((1, w, tq), lambda i, j: (i, 0, j)),
                  per_b(t, w), per_b(w, t), per_b(t, LANES), per_b(n_heads, t),
                  _const_spec(sel.shape), _const_spec(one.shape)],
        out_specs=pl.BlockSpec((1, tq, w), lambda i, j: (i, j, 0)),
        out_shape=jax.ShapeDtypeStruct((b, t, w), BF),
        scratch_shapes=[pltpu.VMEM((n_heads, t), F32),
                        pltpu.VMEM((n_pairs, t, 2 * LANES), BF),
                        pltpu.VMEM((n_heads, 2 * LANES, tq), BF),
                        pltpu.VMEM((2 * n_heads, tq), F32), pltpu.VMEM((n_heads, tq), F32)]
                       + [pltpu.VMEM((tq, tq), F32)] * n_heads
                       + [pltpu.VMEM((tq, tq), BF)] * n_heads
                       + [pltpu.VMEM((HEAD_DIM, tq), F32)] * n_heads,
        compiler_params=_cparams(("parallel", "arbitrary")),
        name="attn_prompt",
    )(qt, kb, vtb, lf, lft, jnp.asarray(sel, BF), jnp.asarray(one, F32))


def _page_copies(step, slot, ng, pages, pt_ref, ck_hbm, cv_hbm, kbuf, vbuf, sem):
    bb = step // ng
    first = (ng - 1 - step % ng) * pages
    out = []
    for i in range(pages):
        pid = pt_ref[bb, first + i]
        out.append(pltpu.make_async_copy(ck_hbm.at[pid], kbuf.at[slot, i], sem.at[slot, 0]))
        out.append(pltpu.make_async_copy(cv_hbm.at[pid], vbuf.at[slot, i], sem.at[slot, 1]))
    return out


def _paged_step(step, n, ng, total, pt_ref, q_ref, kn_ref, vn_ref, lf_ref, lft_ref, tri_ref,
                clf_ref, ck_hbm, cv_hbm, o_ref, qbd_ref, ci_ref, m_ref, l_ref, acc_ref, carry_ref,
                kbuf, vbuf, sem, *, n_heads, pages, tq):
    b = step // ng
    g = step % ng
    mp = n * pages
    first_page = (ng - g - n) * pages
    rows = n_heads * tq
    w = q_ref.shape[2]
    dma = (pt_ref, ck_hbm, cv_hbm, kbuf, vbuf, sem)
    order = [(k, i) for k in reversed(range(n)) for i in range(pages)]

    for k in range(n):
        for cp in _page_copies(step + k, k, ng, pages, *dma):
            cp.wait()

    def expand(x):
        return jnp.concatenate(
            [jnp.broadcast_to(x[h:h + 1, :], (tq, x.shape[1])) for h in range(n_heads)], axis=0)

    def update(s, pv_fn):
        m = m_ref[...]
        m_new = jnp.maximum(m, jnp.max(s, axis=-1, keepdims=True))
        alpha = jnp.exp(m - m_new)
        p = jnp.exp(s - m_new)
        l_ref[...] = alpha * l_ref[...] + jnp.sum(p, axis=-1, keepdims=True)
        acc_ref[...] = alpha * acc_ref[...] + pv_fn(p.astype(BF))
        m_ref[...] = m_new

    @pl.when(g == 0)
    def _():
        q = q_ref[0]
        lane = lax.broadcasted_iota(jnp.int32, (1, w), 1)
        zero = jnp.zeros_like(q)
        qbd_ref[...] = jnp.concatenate(
            [jnp.where((lane >= h * HEAD_DIM) & (lane < (h + 1) * HEAD_DIM), q, zero)
             for h in range(n_heads)], axis=0)
        c = _scan_incl(lf_ref[0], 0)
        ci_ref[...] = jnp.concatenate([c[:, h:h + 1] for h in range(n_heads)], axis=0)
        m_ref[...] = jnp.full((rows, 1), NEG_INF, F32)
        l_ref[...] = jnp.zeros((rows, 1), F32)
        acc_ref[...] = jnp.zeros((rows, w), F32)
        carry_ref[...] = jnp.zeros((n_heads, PAGE_SIZE), F32)
        cj = _scan_incl(lft_ref[0], 1)
        s = _dot_nt(qbd_ref[...], kn_ref[0]) + ci_ref[...] - expand(cj)
        r_idx = lax.broadcasted_iota(jnp.int32, (rows, LANES), 0)
        c_idx = lax.broadcasted_iota(jnp.int32, (rows, LANES), 1)
        s = jnp.where(c_idx <= jnp.bitwise_and(r_idx, tq - 1), s, NEG_INF)
        update(s, lambda p: _dot(p, vn_ref[0]))

    ktc = jnp.concatenate([kbuf[k, i].reshape(w, PAGE_SIZE) for k, i in order], axis=1).astype(BF)
    vtc = jnp.concatenate([vbuf[k, i].reshape(w, PAGE_SIZE) for k, i in order], axis=1).astype(BF)
    lf = jnp.concatenate([clf_ref[pt_ref[b, first_page + i]] for i in range(mp)], axis=0)
    hi, mid, lo = _split3(lf)
    y = _dot(jnp.concatenate([hi, mid, lo], axis=0), tri_ref[...])
    ph = mp * n_heads
    y = y[0:ph] + y[ph:2 * ph] + y[2 * ph:3 * ph]
    run = carry_ref[...]
    cjs = [None] * mp
    for pg in reversed(range(mp)):
        blk = slice(pg * n_heads, (pg + 1) * n_heads)
        cjs[pg] = -(run + (y[blk, 0:PAGE_SIZE] - lf[blk]))
        run = run + y[blk, PAGE_SIZE:2 * PAGE_SIZE]
    carry_ref[...] = run
    cj = jnp.concatenate(cjs, axis=1)
    s = _dot(qbd_ref[...], ktc) + ci_ref[...] - expand(cj)
    update(s, lambda p: _dot_nt(p, vtc))

    @pl.when(g + n == ng)
    def _():
        lane = lax.broadcasted_iota(jnp.int32, (1, w), 1)
        out = jnp.zeros((tq, w), F32)
        for h in range(n_heads):
            sl = slice(h * tq, (h + 1) * tq)
            val = acc_ref[sl, :] / l_ref[sl, :]
            out = jnp.where((lane >= h * HEAD_DIM) & (lane < (h + 1) * HEAD_DIM), val, out)
        o_ref[0] = out.astype(o_ref.dtype)

    @pl.when(step + n < total)
    def _():
        for k in range(n):
            for cp in _page_copies(step + n + k, k, ng, pages, *dma):
                cp.start()


def _paged_operands(q, cache_kt, cache_lft, pages, slots, seq_of_step):
    db, tq, w = q.shape
    n_heads = w // HEAD_DIM
    rows = n_heads * tq
    assert tq & (tq - 1) == 0
    per_seq = lambda s1, s2: pl.BlockSpec((1, s1, s2), lambda i, j, pt: (seq_of_step(i, j), 0, 0))
    in_specs = [per_seq(tq, w), per_seq(PAGE_SIZE, w), per_seq(PAGE_SIZE, w), per_seq(tq, LANES),
                per_seq(n_heads, LANES),
                _const_spec((PAGE_SIZE, 2 * PAGE_SIZE)), _const_spec(cache_lft.shape),
                pl.BlockSpec(memory_space=pl.ANY), pl.BlockSpec(memory_space=pl.ANY)]
    page_buf = pltpu.VMEM((slots, pages, n_heads, HEAD_DIM, PAGE_SIZE), cache_kt.dtype)
    scratch = [pltpu.VMEM((rows, w), BF), pltpu.VMEM((rows, 1), F32),
               pltpu.VMEM((rows, 1), F32), pltpu.VMEM((rows, 1), F32),
               pltpu.VMEM((rows, w), F32), pltpu.VMEM((n_heads, PAGE_SIZE), F32),
               page_buf, page_buf, pltpu.SemaphoreType.DMA((slots, 2))]
    pos = lax.broadcasted_iota(jnp.int32, (PAGE_SIZE, PAGE_SIZE), 0)
    col = lax.broadcasted_iota(jnp.int32, (PAGE_SIZE, PAGE_SIZE), 1)
    tri = jnp.concatenate([(pos >= col).astype(BF), jnp.ones((PAGE_SIZE, PAGE_SIZE), BF)], axis=1)
    return in_specs, per_seq(tq, w), scratch, tri


def _mix_tail(pooled, att, x, wp_ref, ps_ref, wo_ref, g_ref, b_ref, alpha):
    gd = pooled[0].shape[1]
    att_w = att.shape[1]
    mixed = [_dot(pooled[g].astype(BF), wp_ref[g]) * ps_ref[:, g * gd:(g + 1) * gd]
             for g in range(len(pooled))]
    pool = jnp.concatenate(mixed, axis=1).astype(BF)
    mix = _dot(att, wo_ref[0:att_w, :]) + _dot(pool, wo_ref[att_w:, :])
    return _layer_norm(alpha * x + mix, g_ref[...], b_ref[...])


def _shift_down(x, s):
    n8, c = x.shape[0] // SUBLANES, x.shape[1]
    if s == SUBLANES:
        return jnp.concatenate([x[0:SUBLANES], x[:-SUBLANES]], axis=0)
    r = pltpu.roll(x.reshape(n8, SUBLANES, c), s, 1)
    prev = jnp.concatenate([r[0:1], r[:-1]], axis=0)
    sub = lax.broadcasted_iota(jnp.int32, (1, SUBLANES, c), 1)
    return jnp.where(sub < s, prev, r).reshape(n8 * SUBLANES, c)


def _mix_prompt_kernel(x_ref, att_ref, u_ref, halo_ref, wp_ref, ps_ref, wo_ref, g_ref, b_ref,
                       o_ref, *, tm, sub, alpha):
    t = pl.program_id(1)
    hl = MAX_WINDOW
    started = (t > 0).astype(F32)
    gd = u_ref.shape[2] // len(POOL_WINDOWS)
    att_w = att_ref.shape[2]
    blocks = [slice(r0, r0 + sub) for r0 in range(0, tm, sub)]
    pooled = []
    for rows in blocks:
        pos = t * tm + rows.start + lax.broadcasted_iota(jnp.int32, (sub, 1), 0)
        per_group = []
        for g, w in enumerate(POOL_WINDOWS):
            lanes = slice(g * gd, (g + 1) * gd)
            tok = u_ref[0, rows, lanes]
            halo = (halo_ref[0, :, lanes] * started if rows.start == 0
                    else u_ref[0, rows.start - hl:rows.start, lanes])
            win = jnp.concatenate([halo, tok], axis=0)
            k = 1
            while k < w:
                win = win + _shift_down(win, k)
                k *= 2
            cnt = jnp.minimum(pos + 1, w).astype(F32)
            per_group.append(win[hl:] / cnt - tok)
        pooled.append(per_group)
    pools = [jnp.concatenate([_dot(pg[g].astype(BF), wp_ref[g]) * ps_ref[:, g * gd:(g + 1) * gd]
                              for g in range(len(pg))], axis=1).astype(BF) for pg in pooled]
    mixes = [_dot(att_ref[0, rows, :], wo_ref[0:att_w, :]) + _dot(pool, wo_ref[att_w:, :])
             for rows, pool in zip(blocks, pools)]
    for rows, mix in zip(blocks, mixes):
        o_ref[0, rows, :] = _layer_norm(alpha * x_ref[0, rows, :] + mix, g_ref[...], b_ref[...])


def _mix_prompt(x, att, u, w_pool, pool_scale, w_out, g, b, alpha, tm):
    bsz, t, d = x.shape
    att_w, pool_w = att.shape[2], u.shape[2]
    hl = MAX_WINDOW
    ng, gd = w_pool.shape[0], w_pool.shape[1]
    kern = functools.partial(_mix_prompt_kernel, tm=tm, sub=min(tm, 256), alpha=alpha)
    row = lambda w: pl.BlockSpec((1, tm, w), lambda i, j: (i, j, 0))
    halo = pl.BlockSpec((1, hl, pool_w), lambda i, j: (i, jnp.maximum(j * (tm // hl) - 1, 0), 0))
    return pl.pallas_call(
        kern,
        grid=(bsz, t // tm),
        in_specs=[row(d), row(att_w), row(pool_w), halo,
                  _const_spec((ng, gd, gd)), _const_spec((1, pool_w)), _const_spec((att_w + pool_w, d)),
                  _const_spec((1, d)), _const_spec((1, d))],
        out_specs=row(d),
        out_shape=jax.ShapeDtypeStruct((bsz, t, d), F32),
        compiler_params=_cparams(("parallel", "parallel")),
        name="mix_prompt",
    )(x, att, u, u, w_pool, pool_scale, w_out, g, b)


def _mix_sample_kernel(x_ref, att_ref, full_ref, wp_ref, ps_ref, wo_ref, g_ref, b_ref, o_ref,
                       *, ts, n_past, alpha):
    nseq = full_ref.shape[0]
    hl = MAX_WINDOW
    gd = full_ref.shape[2] // len(POOL_WINDOWS)
    pos = n_past + lax.broadcasted_iota(jnp.int32, (1, ts, 1), 1)
    pooled = []
    for g, w in enumerate(POOL_WINDOWS):
        lanes = slice(g * gd, (g + 1) * gd)
        tok = full_ref[:, hl:hl + ts, lanes]
        acc = tok
        for k in range(1, w):
            acc = acc + full_ref[:, hl - k:hl - k + ts, lanes]
        cnt = jnp.minimum(pos + 1, w).astype(F32)
        pooled.append((acc / cnt - tok).reshape(nseq * ts, gd))
    o_ref[...] = _mix_tail(pooled, att_ref[...], x_ref[...], wp_ref, ps_ref, wo_ref, g_ref, b_ref, alpha)


def _mix_sample(x2, att2, full, w_pool, pool_scale, w_out, g, b, alpha, ts, n_past):
    n, d = x2.shape
    kern = functools.partial(_mix_sample_kernel, ts=ts, n_past=n_past, alpha=alpha)
    return pl.pallas_call(
        kern,
        out_shape=jax.ShapeDtypeStruct((n, d), F32),
        compiler_params=pltpu.CompilerParams(vmem_limit_bytes=VMEM_LIMIT),
        name="mix_sample",
    )(x2, att2, full, w_pool, pool_scale, w_out, g, b)


def _shift_rows(h, s):
    n8, c = h.shape[0] // SUBLANES, h.shape[1]
    r = pltpu.roll(h.reshape(n8, SUBLANES, c), s, 1)
    sub = lax.broadcasted_iota(jnp.int32, (1, SUBLANES, c), 1)
    return jnp.where(sub < s, r[:-1], r[1:]).reshape((n8 - 1) * SUBLANES, c)


def _ffn_chunk(hg3, hv3, wc_ref, bc_ref, gcols, vcols, wcv_ref=None, bcv_ref=None):
    def conv(h3, w_ref, b_ref, cols, scale):
        hc = b_ref[:, cols] * scale + h3[0] * (w_ref[0:1, cols] * scale)
        for j in range(1, CONV_WIDTH):
            hc = hc + h3[j] * (w_ref[j:j + 1, cols] * scale)
        return hc
    g = conv(hg3, wc_ref, bc_ref, gcols, 1.0)
    vh = conv(hv3, wc_ref if wcv_ref is None else wcv_ref, bc_ref if bcv_ref is None else bcv_ref,
              vcols, 0.5)
    t = jnp.tanh(g * (GELU_C1 + (GELU_C1 * GELU_C0) * (g * g)))
    return ((g * (1.0 + t)) * vh).astype(BF)


def _ffn_prompt_kernel(pt_ref, x_ref, halo_ref, wu_ref, bu_ref, wc_ref, bc_ref, wd_ref, g_ref, b_ref,
                       *rest, tm, d_ff, chunk, alpha, paged):
    sub, ng, total = paged["sub"], paged["ng"], paged["total"]
    pa_in, (o_ref, tail_ref, oa_ref), pa_scratch = rest[0:9], rest[9:12], rest[12:]
    dma = (pt_ref, pa_in[7], pa_in[8], pa_scratch[6], pa_scratch[7], pa_scratch[8])
    step0 = (pl.program_id(0) * pl.num_programs(1) + pl.program_id(1)) * sub

    @pl.when(step0 == 0)
    def _():
        for k in range(sub):
            for cp in _page_copies(k, k, ng, paged["pages"], *dma):
                cp.start()

    t = pl.program_id(1)
    hl = SUBLANES
    edges = list(range(0, d_ff, chunk)) + [d_ff]
    n_chunks = len(edges) - 1
    cols_of = lambda c: (slice(edges[c], edges[c + 1]), slice(d_ff + edges[c], d_ff + edges[c + 1]))
    st = {}

    def up(c):
        return [_dot(st["xin"], wu_ref[:, cols]) for cols in cols_of(c)]

    def shifted(hd, cols):
        body = hd[hl:] + bu_ref[:, cols]
        tail_ref[0, :, cols] = body[tm - hl:tm]
        h = jnp.concatenate([(hd[0:hl] + bu_ref[:, cols]) * st["started"], body], axis=0)
        return [_shift_rows(h, CONV_WIDTH - 1 - j) for j in range(CONV_WIDTH - 1)] + [body]

    def run_chunks(c_from, c_to):
        for c in range(c_from, c_to):
            cur = st["nxt"]
            if c + 1 < n_chunks:
                st["nxt"] = up(c + 1)
            gcols, vcols = cols_of(c)
            act = _ffn_chunk(shifted(cur[0], gcols), shifted(cur[1], vcols), wc_ref, bc_ref, gcols, vcols)
            st["acc"] = st["acc"] + _dot(act, wd_ref[gcols, :])

    _paged_step(step0, sub, ng, total, pt_ref, *pa_in, oa_ref, *pa_scratch,
                n_heads=paged["n_heads"], pages=paged["pages"], tq=paged["tq"])

    st["x"] = x_ref[0]
    st["xin"] = jnp.concatenate([halo_ref[0], st["x"]], axis=0).astype(BF)
    st["started"] = (t > 0).astype(F32)
    st["acc"] = jnp.zeros((tm, st["x"].shape[1]), F32)
    st["nxt"] = up(0)
    run_chunks(0, n_chunks)
    o_ref[0] = _layer_norm(alpha * st["x"] + st["acc"], g_ref[...], b_ref[...])


def _ffn_prompt_and_paged_attn(x1, w_up, b_up, w_conv, b_conv, w_down, g, b, alpha, tm, chunk,
                               page_table, q, kn, vn, lf, lft, cache_kt, cache_vt, cache_lft):
    bsz, t, d = x1.shape
    d_ff = w_down.shape[0]
    hl = SUBLANES
    nt = t // tm
    db, tq, w = q.shape
    n_pages = page_table.shape[1]
    fits = lambda p: (n_pages % p == 0 and (db * (n_pages // p)) % (bsz * nt) == 0
                      and (n_pages // p) % ((db * (n_pages // p)) // (bsz * nt)) == 0)
    pages = next(p for p in (16, 8, 4, 2, 1) if fits(p))
    ng = n_pages // pages
    total = db * ng
    sub = total // (bsz * nt)
    paged = dict(sub=sub, ng=ng, total=total, pages=pages, n_heads=w // HEAD_DIM, tq=tq)
    pa_specs, pa_out, pa_scratch, tri = _paged_operands(
        q, cache_kt, cache_lft, pages, sub, lambda i, j: ((i * nt + j) * sub) // ng)
    kern = functools.partial(_ffn_prompt_kernel, tm=tm, d_ff=d_ff, chunk=chunk, alpha=alpha, paged=paged)
    row = pl.BlockSpec((1, tm, d), lambda i, j, pt: (i, j, 0))
    halo = pl.BlockSpec((1, hl, d), lambda i, j, pt: (i, jnp.maximum(j * (tm // hl) - 1, 0), 0))
    grid_spec = pltpu.PrefetchScalarGridSpec(
        num_scalar_prefetch=1,
        grid=(bsz, nt),
        in_specs=[row, halo, _const_spec((d, 2 * d_ff)), _const_spec((1, 2 * d_ff)),
                  _const_spec((CONV_WIDTH, 2 * d_ff)), _const_spec((1, 2 * d_ff)),
                  _const_spec((d_ff, d)), _const_spec((1, d)), _const_spec((1, d))] + pa_specs,
        out_specs=[row, pl.BlockSpec((1, hl, 2 * d_ff), lambda i, j, pt: (i, 0, 0)), pa_out],
        scratch_shapes=pa_scratch)
    return pl.pallas_call(
        kern,
        grid_spec=grid_spec,
        out_shape=[jax.ShapeDtypeStruct((bsz, t, d), F32),
                   jax.ShapeDtypeStruct((bsz, hl, 2 * d_ff), F32),
                   jax.ShapeDtypeStruct((db, tq, w), BF)],
        compiler_params=_cparams(("arbitrary", "arbitrary")),
        name="ffn_prompt_attn_paged",
    )(page_table, x1, x1, w_up, b_up, w_conv, b_conv, w_down, g, b,
      q, kn, vn, lf, lft, tri, cache_lft, cache_kt, cache_vt)


def _ffn_sample_kernel(x_ref, stg_ref, stv_ref, wug_ref, wuv_ref, bug_ref, buv_ref, wcg_ref, wcv_ref,
                       bcg_ref, bcv_ref, wd_ref, g_ref, b_ref, o_ref, tg_ref, tv_ref, acc_ref, buf_ref,
                       *, ts, alpha):
    c = pl.program_id(0)
    nseq = stg_ref.shape[0]
    n = nseq * ts
    chunk = wd_ref.shape[0]
    hl = SUBLANES
    nst = CONV_WIDTH - 1
    cols = slice(0, chunk)
    xb = x_ref[...].astype(BF)

    @pl.when(c == 0)
    def _():
        acc_ref[...] = jnp.zeros(acc_ref.shape, F32)

    h3 = []
    for half, (wu, bu, st, tail) in enumerate(((wug_ref, bug_ref, stg_ref, tg_ref),
                                               (wuv_ref, buv_ref, stv_ref, tv_ref))):
        h = (_dot(xb, wu[...]) + bu[...]).reshape(nseq, ts, chunk)
        tail[...] = h[:, ts - nst:, :]
        buf_ref[half, :, hl - nst:hl, :] = st[...]
        buf_ref[half, :, hl:hl + ts, :] = h
        h3.append([buf_ref[half, :, hl - 2 + j:hl - 2 + j + ts, :].reshape(n, chunk)
                   for j in range(CONV_WIDTH)])
    act = _ffn_chunk(h3[0], h3[1], wcg_ref, bcg_ref, cols, cols, wcv_ref, bcv_ref)
    acc_ref[...] += _dot(act, wd_ref[...])

    @pl.when(c == pl.num_programs(0) - 1)
    def _():
        o_ref[...] = _layer_norm(alpha * x_ref[...] + acc_ref[...], g_ref[...], b_ref[...])


def _ffn_sample(x2, state, w_up, b_up, w_conv, b_conv, w_down, g, b, alpha, ts, chunk):
    n, d = x2.shape
    d_ff = w_down.shape[0]
    nseq = n // ts
    nc = d_ff // chunk
    nst = CONV_WIDTH - 1
    kern = functools.partial(_ffn_sample_kernel, ts=ts, alpha=alpha)
    halves = lambda shape, ax: [pl.BlockSpec(shape, lambda c, o=o: (0,) * ax + (c + o,) + (0,) * (len(shape) - ax - 1))
                                for o in (0, nc)]
    return pl.pallas_call(
        kern,
        grid=(nc,),
        in_specs=[_const_spec((n, d))] + halves((nseq, nst, chunk), 2) + halves((d, chunk), 1)
                 + halves((1, chunk), 1) + halves((CONV_WIDTH, chunk), 1) + halves((1, chunk), 1)
                 + [pl.BlockSpec((chunk, d), lambda c: (c, 0)), _const_spec((1, d)), _const_spec((1, d))],
        out_specs=[pl.BlockSpec((n, d), lambda c: (0, 0)),
                   pl.BlockSpec((nseq, nst, chunk), lambda c: (0, 0, c)),
                   pl.BlockSpec((nseq, nst, chunk), lambda c: (0, 0, c))],
        out_shape=[jax.ShapeDtypeStruct((n, d), F32),
                   jax.ShapeDtypeStruct((nseq, nst, d_ff), F32),
                   jax.ShapeDtypeStruct((nseq, nst, d_ff), F32)],
        scratch_shapes=[pltpu.VMEM((n, d), F32), pltpu.VMEM((2, nseq, SUBLANES + ts, chunk), F32)],
        compiler_params=_cparams(("arbitrary",)),
        name="ffn_sample",
    )(x2, state, state, w_up, w_up, b_up, b_up, w_conv, w_conv, b_conv, b_conv, w_down, g, b)


def _ffn_chunk_size(d_ff):
    for c in (512, 256, 128):
        if d_ff % c == 0:
            return c
    raise ValueError("d_ff must be a multiple of 128")


def kernel(x_prompt, x_sample, cache_k, cache_v, cache_logf, state_pool, state_conv, page_table,
           w_in, b_f, w_pool, pool_scale, w_out, ln1_g, ln1_b, w_up, b_up, w_conv, b_conv, w_down,
           ln2_g, ln2_b):
    bsz, t_p, d = x_prompt.shape
    db, t_s, _ = x_sample.shape
    depth = w_in.shape[0]
    n_heads = cache_k.shape[3]
    att_w = n_heads * HEAD_DIM
    pool_w = state_pool.shape[3]
    d_ff = w_down.shape[1]
    n_past = page_table.shape[1] * PAGE_SIZE
    alpha = (2 * depth) ** 0.25
    assert HEAD_DIM * 2 == LANES and n_heads % 2 == 0 and n_heads <= SUBLANES
    assert t_s == SUBLANES and cache_k.shape[2] == PAGE_SIZE and t_p >= MAX_WINDOW
    assert w_in.shape[2] == 3 * att_w + n_heads + pool_w

    tm_proj = min(512, t_p)
    tq = min(256, t_p)
    tm_mix = min(1024, t_p)
    tm_ffn = min(256, t_p)
    chunk = _ffn_chunk_size(d_ff)

    yp, ys = x_prompt, x_sample
    outs = [[] for _ in range(10)]
    for l in range(depth):
        wi = w_in[l]
        fcols = wi[:, 3 * att_w:3 * att_w + n_heads]
        w_row = jnp.concatenate(
            [wi[:, :2 * att_w], wi[:, 3 * att_w + n_heads:], jnp.pad(fcols, ((0, 0), (0, LANES - n_heads)))],
            axis=1).astype(BF)
        w_t = jnp.concatenate(
            [wi[:, :3 * att_w].T, jnp.pad(fcols.T, ((0, BF_ROWS - n_heads), (0, 0)))],
            axis=0).astype(BF)
        bf_row = jnp.pad(b_f[l], (0, LANES - n_heads)).reshape(1, LANES)
        bf_col = b_f[l].reshape(n_heads, 1)
        wp_b = w_pool[l].astype(BF)
        ps = pool_scale[l].reshape(1, pool_w)
        wo_b = w_out[l].astype(BF)
        g1, b1 = ln1_g[l].reshape(1, d), ln1_b[l].reshape(1, d)
        wu_b = w_up[l].astype(BF)
        bu = b_up[l].reshape(1, 2 * d_ff)
        wc = w_conv[l]
        bc = b_conv[l].reshape(1, 2 * d_ff)
        wd_b = w_down[l].astype(BF)
        g2, b2 = ln2_g[l].reshape(1, d), ln2_b[l].reshape(1, d)

        qt, kb, kt, vt, vtb, u, lf, lft = _proj_prompt(yp, w_row, w_t, bf_row, bf_col, att_w, pool_w,
                                                       n_heads, tm_proj)
        n_s = db * t_s
        qs, ks, vs, kbs, vbs, us, lfs = _proj_sample(ys.reshape(n_s, d), w_row, w_t, bf_row, att_w, pool_w)
        att = _attn(qt, kb, vtb, lf, lft, tq)
        x1 = _mix_prompt(yp, att, u, wp_b, ps, wo_b, g1, b1, alpha, tm_mix)

        lfs3 = lfs.reshape(db, t_s, LANES)
        lfts = jnp.pad(jnp.swapaxes(lfs3[:, :, :n_heads], 1, 2), ((0, 0), (0, 0), (0, LANES - t_s)))
        padk = lambda a: jnp.pad(a.reshape(db, t_s, att_w), ((0, 0), (0, PAGE_SIZE - t_s), (0, 0)))
        ckt = jnp.transpose(cache_k[l], (0, 2, 3, 1))
        cvt = jnp.transpose(cache_v[l], (0, 2, 3, 1))
        clft = jnp.swapaxes(cache_logf[l], 1, 2)
        yp, tail, att_s = _ffn_prompt_and_paged_attn(
            x1, wu_b, bu, wc, bc, wd_b, g2, b2, alpha, tm_ffn, chunk,
            page_table, qs.reshape(db, t_s, att_w), padk(kbs), padk(vbs), lfs3, lfts, ckt, cvt, clft)
        to_bthd = lambda a: jnp.transpose(a.reshape(bsz, n_heads, HEAD_DIM, t_p), (0, 3, 1, 2))
        outs[0].append(to_bthd(kt))
        outs[1].append(to_bthd(vt))
        outs[2].append(jnp.swapaxes(lft, 1, 2))
        outs[3].append(u[:, t_p - (MAX_WINDOW - 1):, :])
        outs[4].append(tail[:, SUBLANES - (CONV_WIDTH - 1):, :])

        us3 = us.reshape(db, t_s, pool_w)
        full = jnp.concatenate([jnp.zeros((db, 1, pool_w), F32), state_pool[l], us3], axis=1)
        x1s = _mix_sample(ys.reshape(n_s, d), att_s.reshape(n_s, att_w), full, wp_b, ps, wo_b, g1, b1,
                          alpha, t_s, n_past)
        ys2, tail_g, tail_v = _ffn_sample(x1s, state_conv[l], wu_b, bu, wc, bc, wd_b, g2, b2, alpha, t_s, chunk)
        ys = ys2.reshape(db, t_s, d)
        outs[5].append(ks.reshape(db, t_s, n_heads, HEAD_DIM))
        outs[6].append(vs.reshape(db, t_s, n_heads, HEAD_DIM))
        outs[7].append(lfs3[:, :, :n_heads])
        outs[8].append(full[:, t_s + 1:, :])
        outs[9].append(jnp.concatenate([tail_g, tail_v], axis=-1))

    st = [jnp.stack(o, 0) for o in outs]
    return (yp, ys, st[0], st[1], st[2], st[3], st[4], st[5], st[6], st[7], st[8], st[9])
```

```python
import functools

import jax
import jax.numpy as jnp
import numpy as np
from jax import lax
from jax.experimental import pallas as pl
from jax.experimental.pallas import tpu as pltpu

BF = jnp.bfloat16
F32 = jnp.float32

HEAD_DIM = 64
PAGE_SIZE = 128
POOL_WINDOWS = (2, 4, 8, 16)
MAX_WINDOW = max(POOL_WINDOWS)
CONV_WIDTH = 3
LN_EPS = 1e-5
NEG_INF = -1e30
GELU_C1 = float(np.float32(np.sqrt(2.0 / np.pi)))
GELU_C0 = float(np.float32(0.044715))
LOG2E = float(np.log2(np.e))
LANES = 128
SUBLANES = 8
BF_ROWS = 16
VMEM_LIMIT = 56 * 1024 * 1024


def _cparams(sem):
    return pltpu.CompilerParams(dimension_semantics=sem, vmem_limit_bytes=VMEM_LIMIT)


def _const_spec(shape):
    nd = len(shape)
    return pl.BlockSpec(shape, lambda *_: (0,) * nd, pipeline_mode=pl.Buffered(1))


def _dot(a, b):
    return jnp.dot(a, b, preferred_element_type=F32)


def _dot_nt(a, b):
    return lax.dot_general(a, b, (((1,), (1,)), ((), ())), preferred_element_type=F32)


def _layer_norm(r, g, b):
    mu = jnp.mean(r, axis=-1, keepdims=True)
    d = r - mu
    var = jnp.mean(d * d, axis=-1, keepdims=True)
    return d * lax.rsqrt(var + LN_EPS) * g + b


def _scan_incl(x, axis, reverse=False):
    n = x.shape[axis]
    idx = lax.broadcasted_iota(jnp.int32, x.shape, axis)
    s = 1
    while s < n:
        if reverse:
            x = x + jnp.where(idx + s < n, pltpu.roll(x, n - s, axis), 0.0)
        else:
            x = x + jnp.where(idx >= s, pltpu.roll(x, s, axis), 0.0)
        s *= 2
    return x


def _proj_prompt_kernel(x_ref, wr_ref, wt_ref, bfr_ref, bfc_ref,
                        qt_ref, kb_ref, kt_ref, vt_ref, vtb_ref, u_ref, lf_ref, lft_ref,
                        *, att_w, pool_w, n_heads, scale):
    xb = x_ref[0].astype(BF)
    u_ref[0] = _dot(xb, wr_ref[:, 2 * att_w:2 * att_w + pool_w])
    z = _dot(xb, wr_ref[:, 2 * att_w + pool_w:2 * att_w + pool_w + LANES]) + bfr_ref[...]
    lf_ref[0] = jax.nn.log_sigmoid(z)
    qt_ref[0] = (_dot_nt(wt_ref[0:att_w, :], xb) * scale).astype(BF)
    kt = _dot_nt(wt_ref[att_w:2 * att_w, :], xb)
    kt_ref[0] = kt
    kb_ref[0] = kt.T.astype(BF)
    vt = _dot_nt(wt_ref[2 * att_w:3 * att_w, :], xb)
    vt_ref[0] = vt
    vtb_ref[0] = vt.astype(BF)
    zt = _dot_nt(wt_ref[3 * att_w:3 * att_w + BF_ROWS, :], xb)[0:n_heads] + bfc_ref[...]
    lft_ref[0] = jax.nn.log_sigmoid(zt)


def _proj_prompt(x, w_row, w_t, bf_row, bf_col, att_w, pool_w, n_heads, tm):
    bsz, t, d = x.shape
    kern = functools.partial(_proj_prompt_kernel, att_w=att_w, pool_w=pool_w, n_heads=n_heads,
                             scale=HEAD_DIM ** -0.5 * LOG2E)
    row = lambda w: pl.BlockSpec((1, tm, w), lambda i, j: (i, j, 0))
    col = lambda h: pl.BlockSpec((1, h, tm), lambda i, j: (i, 0, j))
    return pl.pallas_call(
        kern,
        grid=(bsz, t // tm),
        in_specs=[row(d), _const_spec(w_row.shape), _const_spec(w_t.shape),
                  _const_spec((1, LANES)), _const_spec((n_heads, 1))],
        out_specs=[col(att_w), row(att_w), col(att_w), col(att_w), col(att_w), row(pool_w),
                   row(LANES), col(n_heads)],
        out_shape=[jax.ShapeDtypeStruct((bsz, att_w, t), BF),
                   jax.ShapeDtypeStruct((bsz, t, att_w), BF),
                   jax.ShapeDtypeStruct((bsz, att_w, t), F32),
                   jax.ShapeDtypeStruct((bsz, att_w, t), F32),
                   jax.ShapeDtypeStruct((bsz, att_w, t), BF),
                   jax.ShapeDtypeStruct((bsz, t, pool_w), F32),
                   jax.ShapeDtypeStruct((bsz, t, LANES), F32),
                   jax.ShapeDtypeStruct((bsz, n_heads, t), F32)],
        compiler_params=_cparams(("parallel", "parallel")),
        name="proj_prompt",
    )(x, w_row, w_t, bf_row, bf_col)


def _proj_sample_kernel(x_ref, wr_ref, wt_ref, bfr_ref, q_ref, k_ref, v_ref, kb_ref, vb_ref, u_ref, lf_ref,
                        *, att_w, pool_w, scale):
    xb = x_ref[...].astype(BF)
    q_ref[...] = (_dot(xb, wr_ref[:, 0:att_w]) * scale).astype(BF)
    u_ref[...] = _dot(xb, wr_ref[:, 2 * att_w:2 * att_w + pool_w])
    z = _dot(xb, wr_ref[:, 2 * att_w + pool_w:2 * att_w + pool_w + LANES]) + bfr_ref[...]
    lf_ref[...] = jax.nn.log_sigmoid(z)
    k = _dot(xb, wr_ref[:, att_w:2 * att_w])
    k_ref[...] = k
    kb_ref[...] = k.astype(BF)
    v = _dot_nt(xb, wt_ref[2 * att_w:3 * att_w, :])
    v_ref[...] = v
    vb_ref[...] = v.astype(BF)


def _proj_sample(x2, w_row, w_t, bf_row, att_w, pool_w):
    n, d = x2.shape
    kern = functools.partial(_proj_sample_kernel, att_w=att_w, pool_w=pool_w, scale=HEAD_DIM ** -0.5)
    return pl.pallas_call(
        kern,
        out_shape=[jax.ShapeDtypeStruct((n, att_w), BF),
                   jax.ShapeDtypeStruct((n, att_w), F32),
                   jax.ShapeDtypeStruct((n, att_w), F32),
                   jax.ShapeDtypeStruct((n, att_w), BF),
                   jax.ShapeDtypeStruct((n, att_w), BF),
                   jax.ShapeDtypeStruct((n, pool_w), F32),
                   jax.ShapeDtypeStruct((n, LANES), F32)],
        compiler_params=pltpu.CompilerParams(vmem_limit_bytes=VMEM_LIMIT),
        name="proj_sample",
    )(x2, w_row, w_t, bf_row)


KEY_CHUNK = 32


def _group_reduce(fn, x):
    parts = [x[i * SUBLANES:(i + 1) * SUBLANES] for i in range(x.shape[0] // SUBLANES)]
    while len(parts) > 1:
        parts = [fn(parts[i], parts[i + 1]) if i + 1 < len(parts) else parts[i]
                 for i in range(0, len(parts), 2)]
    return parts[0]


AUG_ROWS = 16


def _split3(x):
    hi = x.astype(BF)
    r1 = x - hi.astype(F32)
    mid = r1.astype(BF)
    lo = (r1 - mid.astype(F32)).astype(BF)
    return hi, mid, lo


def _attn_kernel(qt_ref, qn_ref, k_ref, vt_ref, lf_ref, lft_ref, sel_ref, one_ref, o_ref,
                 ct_ref, kx_ref, qx_ref, ml_ref, mb_ref, *per_head, tq, n_heads):
    qi = pl.program_id(1)
    tk = tq
    rc = KEY_CHUNK
    t = k_ref.shape[1]
    n_pairs = n_heads // 2
    s_refs, p_refs, acc_refs = (per_head[i * n_heads:(i + 1) * n_heads] for i in range(3))

    @pl.when(qi == 0)
    def _():
        ct_ref[...] = _scan_incl(lft_ref[0], 1) * LOG2E
        blk = 512 if t % 512 == 0 else t
        for r0 in range(0, t, blk):
            rows = slice(r0, r0 + blk)
            c = _scan_incl(lf_ref[0, rows, :], 0)
            if r0:
                c = c + tot
            tot = c[blk - 1:blk, :]
            hi, mid, lo = _split3(c * LOG2E)
            aug = _dot(jnp.concatenate([hi, mid, lo], axis=1), sel_ref[...]) + one_ref[...]
            for pr in range(n_pairs):
                kx_ref[pr, rows, 0:LANES] = k_ref[0, rows, pr * LANES:(pr + 1) * LANES]
                kx_ref[pr, rows, LANES:2 * LANES] = aug[:, pr * LANES:(pr + 1) * LANES].astype(BF)
        qx_ref[:, LANES + AUG_ROWS:, :] = jnp.zeros((2 * n_heads, LANES - AUG_ROWS, tq), BF)

    q0 = pl.multiple_of(qi * tq, tq)
    par = lax.rem(qi, 2)
    first = lax.broadcasted_iota(jnp.int32, (LANES, 1), 0) < HEAD_DIM
    causal = (lax.broadcasted_iota(jnp.int32, (tk, tq), 0)
              <= lax.broadcasted_iota(jnp.int32, (tk, tq), 1))
    arow = lax.broadcasted_iota(jnp.int32, (AUG_ROWS, 1), 0)
    heads = range(n_heads)

    def build_queries(src_ref, half, start):
        for pair in range(n_pairs):
            qtp = src_ref[0, pair * LANES:(pair + 1) * LANES, :]
            zero = jnp.zeros_like(qtp)
            qx_ref[half * n_heads + 2 * pair, 0:LANES, :] = jnp.where(first, qtp, zero)
            qx_ref[half * n_heads + 2 * pair + 1, 0:LANES, :] = jnp.where(first, zero, qtp)
        for h in heads:
            hf = h % 2
            ci = ct_ref[h:h + 1, pl.ds(start, tq)]
            hi = ci.astype(BF).astype(F32)
            r1 = ci - hi
            mid = r1.astype(BF).astype(F32)
            lo = r1 - mid
            pick = ((arow >= 3 * hf) & (arow < 3 * hf + 3)).astype(F32)
            blk = jnp.where(arow == 6, hi, jnp.where(arow == 7, mid, jnp.where(arow == 8, lo, pick)))
            qx_ref[half * n_heads + h, LANES:LANES + AUG_ROWS, :] = blk.astype(BF)

    def scores(jb, masked, half):
        k0 = pl.multiple_of(jb * tk, tk)
        out = []
        for h in heads:
            s = _dot(kx_ref[h // 2, pl.ds(k0, tk), :], qx_ref[half * n_heads + h])
            out.append(jnp.where(causal, s, NEG_INF) if masked else s)
        return out

    def keep(h, s):
        s_refs[h][...] = s
        mb_ref[h:h + 1, :] = jnp.max(_group_reduce(jnp.maximum, s), axis=0, keepdims=True)

    def consume(jb, nxt=None):
        k0 = pl.multiple_of(jb * tk, tk)
        m_new, alpha = [], []
        for h in heads:
            m_old = ml_ref[2 * h:2 * h + 1, :]
            m_new.append(jnp.maximum(m_old, mb_ref[h:h + 1, :]))
            alpha.append(jnp.exp2(m_old - m_new[h]))
            ml_ref[2 * h:2 * h + 1, :] = m_new[h]
        for h in heads:
            lb = jnp.zeros((SUBLANES, tq), F32)
            for r in range(tk // rc):
                p = jnp.exp2(s_refs[h][r * rc:(r + 1) * rc, :] - m_new[h])
                lb = lb + _group_reduce(jnp.add, p)
                p_refs[h][r * rc:(r + 1) * rc, :] = p.astype(BF)
            ml_ref[2 * h + 1:2 * h + 2, :] = (alpha[h] * ml_ref[2 * h + 1:2 * h + 2, :]
                                              + jnp.sum(lb, axis=0, keepdims=True))
            if nxt is not None:
                keep(h, nxt[h])
        for h in heads:
            vh = vt_ref[0, h * HEAD_DIM:(h + 1) * HEAD_DIM, pl.ds(k0, tk)]
            acc_refs[h][...] = alpha[h] * acc_refs[h][...] + _dot(vh, p_refs[h][...])

    is_max = lax.broadcasted_iota(jnp.int32, (2 * n_heads, 1), 0) % 2 == 0
    ml_ref[...] = jnp.where(is_max, jnp.full((2 * n_heads, tq), NEG_INF, F32), 0.0)
    for h in heads:
        acc_refs[h][...] = jnp.zeros((HEAD_DIM, tq), F32)

    @pl.when(qi == 0)
    def _():
        build_queries(qt_ref, 0, 0)
        for h, s in enumerate(scores(0, True, 0)):
            keep(h, s)

    @pl.when(qi > 0)
    def _():
        def body(j, _):
            consume(j, scores(j + 1, False, par))
            return 0

        lax.fori_loop(0, qi - 1, body, 0)
        consume(qi - 1, scores(qi, True, par))

    has_next = qi + 1 < pl.num_programs(1)

    @pl.when(has_next)
    def _():
        build_queries(qn_ref, 1 - par, pl.multiple_of(q0 + tq, tq))
        consume(qi, scores(0, False, 1 - par))

    @pl.when(jnp.logical_not(has_next))
    def _():
        consume(qi)

    for pair in range(n_heads // 2):
        h0, h1 = 2 * pair, 2 * pair + 1
        out_t = jnp.concatenate([acc_refs[h0][...] / ml_ref[2 * h0 + 1:2 * h0 + 2, :],
                                 acc_refs[h1][...] / ml_ref[2 * h1 + 1:2 * h1 + 2, :]], axis=0)
        o_ref[0, :, pair * LANES:(pair + 1) * LANES] = out_t.T.astype(o_ref.dtype)


def _attn(qt, kb, vtb, lf, lft, tq):
    b, w, t = qt.shape
    n_heads = w // HEAD_DIM
    n_pairs = n_heads // 2
    sel = np.zeros((3 * LANES, n_pairs * LANES), np.float32)
    one = np.zeros((1, n_pairs * LANES), np.float32)
    for pr in range(n_pairs):
        for hf in range(2):
            for piece in range(3):
                sel[piece * LANES + 2 * pr + hf, pr * LANES + 3 * hf + piece] = -1.0
        one[0, pr * LANES + 6:pr * LANES + 9] = 1.0
    kern = functools.partial(_attn_kernel, tq=tq, n_heads=n_heads)
    per_b = lambda s1, s2: pl.BlockSpec((1, s1, s2), lambda i, j: (i, 0, 0))
    return pl.pallas_call(
        kern,
        grid=(b, t // tq),
        in_specs=[pl.BlockSpec((1, w, tq), lambda i, j: (i, 0, j)),
                  pl.BlockSpec((1, w, tq), lambda i, j: (i, 0, jnp.minimum(j + 1, t // tq - 1))),
                  per_b(t, w), per_b(w, t), per_b(t, LANES), per_b(n_heads, t),
                  _const_spec(sel.shape), _const_spec(one.shape)],
        out_specs=pl.BlockSpec((1, tq, w), lambda i, j: (i, j, 0)),
        out_shape=jax.ShapeDtypeStruct((b, t, w), BF),
        scratch_shapes=[pltpu.VMEM((n_heads, t), F32),
                        pltpu.VMEM((n_pairs, t, 2 * LANES), BF),
                        pltpu.VMEM((2 * n_heads, 2 * LANES, tq), BF),
                        pltpu.VMEM((2 * n_heads, tq), F32), pltpu.VMEM((n_heads, tq), F32)]
                       + [pltpu.VMEM((tq, tq), F32)] * n_heads
                       + [pltpu.VMEM((tq, tq), BF)] * n_heads
                       + [pltpu.VMEM((HEAD_DIM, tq), F32)] * n_heads,
        compiler_params=_cparams(("parallel", "arbitrary")),
        name="attn_prompt",
    )(qt, qt, kb, vtb, lf, lft, jnp.asarray(sel, BF), jnp.asarray(one, F32))


def _page_copies(step, slot, ng, pages, pt_ref, ck_hbm, cv_hbm, kbuf, vbuf, sem):
    bb = step // ng
    first = (ng - 1 - step % ng) * pages
    out = []
    for i in range(pages):
        pid = pt_ref[bb, first + i]
        out.append(pltpu.make_async_copy(ck_hbm.at[pid], kbuf.at[slot, i], sem.at[slot, 0]))
        out.append(pltpu.make_async_copy(cv_hbm.at[pid], vbuf.at[slot, i], sem.at[slot, 1]))
    return out


def _paged_step(step, n, ng, total, pt_ref, q_ref, kn_ref, vn_ref, lf_ref, lft_ref, tri_ref,
                clf_ref, ck_hbm, cv_hbm, o_ref, qbd_ref, ci_ref, m_ref, l_ref, acc_ref, carry_ref,
                kbuf, vbuf, sem, *, n_heads, pages, tq):
    b = step // ng
    g = step % ng
    mp = n * pages
    first_page = (ng - g - n) * pages
    rows = n_heads * tq
    w = q_ref.shape[2]
    dma = (pt_ref, ck_hbm, cv_hbm, kbuf, vbuf, sem)
    order = [(k, i) for k in reversed(range(n)) for i in range(pages)]

    for k in range(n):
        for cp in _page_copies(step + k, k, ng, pages, *dma):
            cp.wait()

    def expand(x):
        return jnp.concatenate(
            [jnp.broadcast_to(x[h:h + 1, :], (tq, x.shape[1])) for h in range(n_heads)], axis=0)

    def update(s, pv_fn):
        m = m_ref[...]
        m_new = jnp.maximum(m, jnp.max(s, axis=-1, keepdims=True))
        alpha = jnp.exp(m - m_new)
        p = jnp.exp(s - m_new)
        l_ref[...] = alpha * l_ref[...] + jnp.sum(p, axis=-1, keepdims=True)
        acc_ref[...] = alpha * acc_ref[...] + pv_fn(p.astype(BF))
        m_ref[...] = m_new

    @pl.when(g == 0)
    def _():
        q = q_ref[0]
        lane = lax.broadcasted_iota(jnp.int32, (1, w), 1)
        zero = jnp.zeros_like(q)
        qbd_ref[...] = jnp.concatenate(
            [jnp.where((lane >= h * HEAD_DIM) & (lane < (h + 1) * HEAD_DIM), q, zero)
             for h in range(n_heads)], axis=0)
        c = _scan_incl(lf_ref[0], 0)
        ci_ref[...] = jnp.concatenate([c[:, h:h + 1] for h in range(n_heads)], axis=0)
        m_ref[...] = jnp.full((rows, 1), NEG_INF, F32)
        l_ref[...] = jnp.zeros((rows, 1), F32)
        acc_ref[...] = jnp.zeros((rows, w), F32)
        carry_ref[...] = jnp.zeros((n_heads, PAGE_SIZE), F32)
        lfn = lft_ref[0]
        pieces = [p.astype(F32) for p in _split3(lfn)] + [jnp.zeros_like(lfn)]
        yn = _dot(jnp.concatenate(pieces, axis=0).astype(BF), tri_ref[...])
        yn = yn[0:n_heads] + yn[n_heads:2 * n_heads] + yn[2 * n_heads:3 * n_heads]
        cj = yn[:, PAGE_SIZE:2 * PAGE_SIZE] - yn[:, 0:PAGE_SIZE] + lfn
        s = _dot_nt(qbd_ref[...], kn_ref[0]) + ci_ref[...] - expand(cj)
        r_idx = lax.broadcasted_iota(jnp.int32, (rows, LANES), 0)
        c_idx = lax.broadcasted_iota(jnp.int32, (rows, LANES), 1)
        s = jnp.where(c_idx <= jnp.bitwise_and(r_idx, tq - 1), s, NEG_INF)
        update(s, lambda p: _dot(p, vn_ref[0]))

    ktc = jnp.concatenate([kbuf[k, i].reshape(w, PAGE_SIZE) for k, i in order], axis=1).astype(BF)
    vtc = jnp.concatenate([vbuf[k, i].reshape(w, PAGE_SIZE) for k, i in order], axis=1).astype(BF)
    lf = jnp.concatenate([clf_ref[pt_ref[b, first_page + i]] for i in range(mp)], axis=0)
    hi, mid, lo = _split3(lf)
    y = _dot(jnp.concatenate([hi, mid, lo], axis=0), tri_ref[...])
    ph = mp * n_heads
    y = y[0:ph] + y[ph:2 * ph] + y[2 * ph:3 * ph]
    run = carry_ref[...]
    cjs = [None] * mp
    for pg in reversed(range(mp)):
        blk = slice(pg * n_heads, (pg + 1) * n_heads)
        cjs[pg] = -(run + (y[blk, 0:PAGE_SIZE] - lf[blk]))
        run = run + y[blk, PAGE_SIZE:2 * PAGE_SIZE]
    carry_ref[...] = run
    cj = jnp.concatenate(cjs, axis=1)
    s = _dot(qbd_ref[...], ktc) + ci_ref[...] - expand(cj)
    update(s, lambda p: _dot_nt(p, vtc))

    @pl.when(g + n == ng)
    def _():
        lane = lax.broadcasted_iota(jnp.int32, (1, w), 1)
        out = jnp.zeros((tq, w), F32)
        for h in range(n_heads):
            sl = slice(h * tq, (h + 1) * tq)
            val = acc_ref[sl, :] / l_ref[sl, :]
            out = jnp.where((lane >= h * HEAD_DIM) & (lane < (h + 1) * HEAD_DIM), val, out)
        o_ref[0] = out.astype(o_ref.dtype)

    @pl.when(step + n < total)
    def _():
        for k in range(n):
            for cp in _page_copies(step + n + k, k, ng, pages, *dma):
                cp.start()


def _paged_operands(q, cache_kt, cache_lft, pages, slots, seq_of_step):
    db, tq, w = q.shape
    n_heads = w // HEAD_DIM
    rows = n_heads * tq
    assert tq & (tq - 1) == 0
    per_seq = lambda s1, s2: pl.BlockSpec((1, s1, s2), lambda i, j, pt: (seq_of_step(i, j), 0, 0))
    in_specs = [per_seq(tq, w), per_seq(PAGE_SIZE, w), per_seq(PAGE_SIZE, w), per_seq(tq, LANES),
                per_seq(n_heads, LANES),
                _const_spec((PAGE_SIZE, 2 * PAGE_SIZE)), _const_spec(cache_lft.shape),
                pl.BlockSpec(memory_space=pl.ANY), pl.BlockSpec(memory_space=pl.ANY)]
    page_buf = pltpu.VMEM((slots, pages, n_heads, HEAD_DIM, PAGE_SIZE), cache_kt.dtype)
    scratch = [pltpu.VMEM((rows, w), BF), pltpu.VMEM((rows, 1), F32),
               pltpu.VMEM((rows, 1), F32), pltpu.VMEM((rows, 1), F32),
               pltpu.VMEM((rows, w), F32), pltpu.VMEM((n_heads, PAGE_SIZE), F32),
               page_buf, page_buf, pltpu.SemaphoreType.DMA((slots, 2))]
    pos = lax.broadcasted_iota(jnp.int32, (PAGE_SIZE, PAGE_SIZE), 0)
    col = lax.broadcasted_iota(jnp.int32, (PAGE_SIZE, PAGE_SIZE), 1)
    tri = jnp.concatenate([(pos >= col).astype(BF), jnp.ones((PAGE_SIZE, PAGE_SIZE), BF)], axis=1)
    return in_specs, per_seq(tq, w), scratch, tri


def _mix_tail(pooled, att, x, wp_ref, ps_ref, wo_ref, g_ref, b_ref, alpha):
    gd = pooled[0].shape[1]
    att_w = att.shape[1]
    mixed = [_dot(pooled[g].astype(BF), wp_ref[g]) * ps_ref[:, g * gd:(g + 1) * gd]
             for g in range(len(pooled))]
    pool = jnp.concatenate(mixed, axis=1).astype(BF)
    mix = _dot(att, wo_ref[0:att_w, :]) + _dot(pool, wo_ref[att_w:, :])
    return _layer_norm(alpha * x + mix, g_ref[...], b_ref[...])


def _shift_down(x, s):
    n8, c = x.shape[0] // SUBLANES, x.shape[1]
    if s == SUBLANES:
        return jnp.concatenate([x[0:SUBLANES], x[:-SUBLANES]], axis=0)
    r = pltpu.roll(x.reshape(n8, SUBLANES, c), s, 1)
    prev = jnp.concatenate([r[0:1], r[:-1]], axis=0)
    sub = lax.broadcasted_iota(jnp.int32, (1, SUBLANES, c), 1)
    return jnp.where(sub < s, prev, r).reshape(n8 * SUBLANES, c)


def _mix_prompt_kernel(x_ref, att_ref, u_ref, halo_ref, wp_ref, ps_ref, wo_ref, g_ref, b_ref,
                       o_ref, *, tm, sub, alpha):
    t = pl.program_id(1)
    hl = MAX_WINDOW
    started = (t > 0).astype(F32)
    gd = u_ref.shape[2] // len(POOL_WINDOWS)
    att_w = att_ref.shape[2]
    blocks = [slice(r0, r0 + sub) for r0 in range(0, tm, sub)]
    pooled = []
    for rows in blocks:
        pos = t * tm + rows.start + lax.broadcasted_iota(jnp.int32, (sub, 1), 0)
        per_group = []
        for g, w in enumerate(POOL_WINDOWS):
            lanes = slice(g * gd, (g + 1) * gd)
            tok = u_ref[0, rows, lanes]
            halo = (halo_ref[0, :, lanes] * started if rows.start == 0
                    else u_ref[0, rows.start - hl:rows.start, lanes])
            win = jnp.concatenate([halo, tok], axis=0)
            k = 1
            while k < w:
                win = win + _shift_down(win, k)
                k *= 2
            cnt = jnp.minimum(pos + 1, w).astype(F32)
            per_group.append(win[hl:] / cnt - tok)
        pooled.append(per_group)
    pools = [jnp.concatenate([_dot(pg[g].astype(BF), wp_ref[g]) * ps_ref[:, g * gd:(g + 1) * gd]
                              for g in range(len(pg))], axis=1).astype(BF) for pg in pooled]
    mixes = [_dot(att_ref[0, rows, :], wo_ref[0:att_w, :]) + _dot(pool, wo_ref[att_w:, :])
             for rows, pool in zip(blocks, pools)]
    for rows, mix in zip(blocks, mixes):
        o_ref[0, rows, :] = _layer_norm(alpha * x_ref[0, rows, :] + mix, g_ref[...], b_ref[...])


def _mix_prompt(x, att, u, w_pool, pool_scale, w_out, g, b, alpha, tm):
    bsz, t, d = x.shape
    att_w, pool_w = att.shape[2], u.shape[2]
    hl = MAX_WINDOW
    ng, gd = w_pool.shape[0], w_pool.shape[1]
    kern = functools.partial(_mix_prompt_kernel, tm=tm, sub=min(tm, 256), alpha=alpha)
    row = lambda w: pl.BlockSpec((1, tm, w), lambda i, j: (i, j, 0))
    halo = pl.BlockSpec((1, hl, pool_w), lambda i, j: (i, jnp.maximum(j * (tm // hl) - 1, 0), 0))
    return pl.pallas_call(
        kern,
        grid=(bsz, t // tm),
        in_specs=[row(d), row(att_w), row(pool_w), halo,
                  _const_spec((ng, gd, gd)), _const_spec((1, pool_w)), _const_spec((att_w + pool_w, d)),
                  _const_spec((1, d)), _const_spec((1, d))],
        out_specs=row(d),
        out_shape=jax.ShapeDtypeStruct((bsz, t, d), F32),
        compiler_params=_cparams(("parallel", "parallel")),
        name="mix_prompt",
    )(x, att, u, u, w_pool, pool_scale, w_out, g, b)


def _mix_sample_kernel(x_ref, att_ref, full_ref, wp_ref, ps_ref, wo_ref, g_ref, b_ref, o_ref,
                       *, ts, n_past, alpha):
    nseq = full_ref.shape[0]
    hl = MAX_WINDOW
    gd = full_ref.shape[2] // len(POOL_WINDOWS)
    pos = n_past + lax.broadcasted_iota(jnp.int32, (1, ts, 1), 1)
    pooled = []
    for g, w in enumerate(POOL_WINDOWS):
        lanes = slice(g * gd, (g + 1) * gd)
        tok = full_ref[:, hl:hl + ts, lanes]
        acc = tok
        for k in range(1, w):
            acc = acc + full_ref[:, hl - k:hl - k + ts, lanes]
        cnt = jnp.minimum(pos + 1, w).astype(F32)
        pooled.append((acc / cnt - tok).reshape(nseq * ts, gd))
    o_ref[...] = _mix_tail(pooled, att_ref[...], x_ref[...], wp_ref, ps_ref, wo_ref, g_ref, b_ref, alpha)


def _mix_sample(x2, att2, full, w_pool, pool_scale, w_out, g, b, alpha, ts, n_past):
    n, d = x2.shape
    kern = functools.partial(_mix_sample_kernel, ts=ts, n_past=n_past, alpha=alpha)
    return pl.pallas_call(
        kern,
        out_shape=jax.ShapeDtypeStruct((n, d), F32),
        compiler_params=pltpu.CompilerParams(vmem_limit_bytes=VMEM_LIMIT),
        name="mix_sample",
    )(x2, att2, full, w_pool, pool_scale, w_out, g, b)


def _shift_rows(h, s):
    n8, c = h.shape[0] // SUBLANES, h.shape[1]
    r = pltpu.roll(h.reshape(n8, SUBLANES, c), s, 1)
    sub = lax.broadcasted_iota(jnp.int32, (1, SUBLANES, c), 1)
    return jnp.where(sub < s, r[:-1], r[1:]).reshape((n8 - 1) * SUBLANES, c)


def _ffn_chunk(hg3, hv3, wc_ref, bc_ref, gcols, vcols, wcv_ref=None, bcv_ref=None):
    def conv(h3, w_ref, b_ref, cols, scale):
        hc = b_ref[:, cols] * scale + h3[0] * (w_ref[0:1, cols] * scale)
        for j in range(1, CONV_WIDTH):
            hc = hc + h3[j] * (w_ref[j:j + 1, cols] * scale)
        return hc
    g = conv(hg3, wc_ref, bc_ref, gcols, 1.0)
    vh = conv(hv3, wc_ref if wcv_ref is None else wcv_ref, bc_ref if bcv_ref is None else bcv_ref,
              vcols, 0.5)
    t = jnp.tanh(g * (GELU_C1 + (GELU_C1 * GELU_C0) * (g * g)))
    return ((g * (1.0 + t)) * vh).astype(BF)


def _ffn_prompt_kernel(pt_ref, x_ref, halo_ref, wu_ref, bu_ref, wc_ref, bc_ref, wd_ref, g_ref, b_ref,
                       *rest, tm, d_ff, chunk, alpha, paged):
    sub, ng, total = paged["sub"], paged["ng"], paged["total"]
    pa_in, (o_ref, tail_ref, oa_ref), pa_scratch = rest[0:9], rest[9:12], rest[12:]
    dma = (pt_ref, pa_in[7], pa_in[8], pa_scratch[6], pa_scratch[7], pa_scratch[8])
    step0 = (pl.program_id(0) * pl.num_programs(1) + pl.program_id(1)) * sub

    @pl.when(step0 == 0)
    def _():
        for k in range(sub):
            for cp in _page_copies(k, k, ng, paged["pages"], *dma):
                cp.start()

    t = pl.program_id(1)
    hl = SUBLANES
    edges = list(range(0, d_ff, chunk)) + [d_ff]
    n_chunks = len(edges) - 1
    cols_of = lambda c: (slice(edges[c], edges[c + 1]), slice(d_ff + edges[c], d_ff + edges[c + 1]))
    st = {}

    def up(c):
        return [_dot(st["xin"], wu_ref[:, cols]) for cols in cols_of(c)]

    def shifted(hd, cols):
        body = hd[hl:] + bu_ref[:, cols]
        tail_ref[0, :, cols] = body[tm - hl:tm]
        h = jnp.concatenate([(hd[0:hl] + bu_ref[:, cols]) * st["started"], body], axis=0)
        return [_shift_rows(h, CONV_WIDTH - 1 - j) for j in range(CONV_WIDTH - 1)] + [body]

    def run_chunks(c_from, c_to):
        for c in range(c_from, c_to):
            cur = st["nxt"]
            if c + 1 < n_chunks:
                st["nxt"] = up(c + 1)
            gcols, vcols = cols_of(c)
            act = _ffn_chunk(shifted(cur[0], gcols), shifted(cur[1], vcols), wc_ref, bc_ref, gcols, vcols)
            st["acc"] = st["acc"] + _dot(act, wd_ref[gcols, :])

    _paged_step(step0, sub, ng, total, pt_ref, *pa_in, oa_ref, *pa_scratch,
                n_heads=paged["n_heads"], pages=paged["pages"], tq=paged["tq"])

    st["x"] = x_ref[0]
    st["xin"] = jnp.concatenate([halo_ref[0], st["x"]], axis=0).astype(BF)
    st["started"] = (t > 0).astype(F32)
    st["acc"] = jnp.zeros((tm, st["x"].shape[1]), F32)
    st["nxt"] = up(0)
    run_chunks(0, n_chunks)
    o_ref[0] = _layer_norm(alpha * st["x"] + st["acc"], g_ref[...], b_ref[...])


def _ffn_prompt_and_paged_attn(x1, w_up, b_up, w_conv, b_conv, w_down, g, b, alpha, tm, chunk,
                               page_table, q, kn, vn, lf, lft, cache_kt, cache_vt, cache_lft):
    bsz, t, d = x1.shape
    d_ff = w_down.shape[0]
    hl = SUBLANES
    nt = t // tm
    db, tq, w = q.shape
    n_pages = page_table.shape[1]
    fits = lambda p: (n_pages % p == 0 and (db * (n_pages // p)) % (bsz * nt) == 0
                      and (n_pages // p) % ((db * (n_pages // p)) // (bsz * nt)) == 0)
    pages = next(p for p in (16, 8, 4, 2, 1) if fits(p))
    ng = n_pages // pages
    total = db * ng
    sub = total // (bsz * nt)
    paged = dict(sub=sub, ng=ng, total=total, pages=pages, n_heads=w // HEAD_DIM, tq=tq)
    pa_specs, pa_out, pa_scratch, tri = _paged_operands(
        q, cache_kt, cache_lft, pages, sub, lambda i, j: ((i * nt + j) * sub) // ng)
    kern = functools.partial(_ffn_prompt_kernel, tm=tm, d_ff=d_ff, chunk=chunk, alpha=alpha, paged=paged)
    row = pl.BlockSpec((1, tm, d), lambda i, j, pt: (i, j, 0))
    halo = pl.BlockSpec((1, hl, d), lambda i, j, pt: (i, jnp.maximum(j * (tm // hl) - 1, 0), 0))
    grid_spec = pltpu.PrefetchScalarGridSpec(
        num_scalar_prefetch=1,
        grid=(bsz, nt),
        in_specs=[row, halo, _const_spec((d, 2 * d_ff)), _const_spec((1, 2 * d_ff)),
                  _const_spec((CONV_WIDTH, 2 * d_ff)), _const_spec((1, 2 * d_ff)),
                  _const_spec((d_ff, d)), _const_spec((1, d)), _const_spec((1, d))] + pa_specs,
        out_specs=[row, pl.BlockSpec((1, hl, 2 * d_ff), lambda i, j, pt: (i, 0, 0)), pa_out],
        scratch_shapes=pa_scratch)
    return pl.pallas_call(
        kern,
        grid_spec=grid_spec,
        out_shape=[jax.ShapeDtypeStruct((bsz, t, d), F32),
                   jax.ShapeDtypeStruct((bsz, hl, 2 * d_ff), F32),
                   jax.ShapeDtypeStruct((db, tq, w), BF)],
        compiler_params=_cparams(("arbitrary", "arbitrary")),
        name="ffn_prompt_attn_paged",
    )(page_table, x1, x1, w_up, b_up, w_conv, b_conv, w_down, g, b,
      q, kn, vn, lf, lft, tri, cache_lft, cache_kt, cache_vt)


def _ffn_sample_kernel(x_ref, stg_ref, stv_ref, wug_ref, wuv_ref, bug_ref, buv_ref, wcg_ref, wcv_ref,
                       bcg_ref, bcv_ref, wd_ref, g_ref, b_ref, o_ref, tg_ref, tv_ref, acc_ref, buf_ref,
                       *, ts, alpha):
    c = pl.program_id(0)
    nseq = stg_ref.shape[0]
    n = nseq * ts
    chunk = wd_ref.shape[0]
    hl = SUBLANES
    nst = CONV_WIDTH - 1
    cols = slice(0, chunk)
    xb = x_ref[...].astype(BF)

    @pl.when(c == 0)
    def _():
        acc_ref[...] = jnp.zeros(acc_ref.shape, F32)

    h3 = []
    for half, (wu, bu, st, tail) in enumerate(((wug_ref, bug_ref, stg_ref, tg_ref),
                                               (wuv_ref, buv_ref, stv_ref, tv_ref))):
        h = (_dot(xb, wu[...]) + bu[...]).reshape(nseq, ts, chunk)
        tail[...] = h[:, ts - nst:, :]
        buf_ref[half, :, hl - nst:hl, :] = st[...]
        buf_ref[half, :, hl:hl + ts, :] = h
        h3.append([buf_ref[half, :, hl - 2 + j:hl - 2 + j + ts, :].reshape(n, chunk)
                   for j in range(CONV_WIDTH)])
    act = _ffn_chunk(h3[0], h3[1], wcg_ref, bcg_ref, cols, cols, wcv_ref, bcv_ref)
    acc_ref[...] += _dot(act, wd_ref[...])

    @pl.when(c == pl.num_programs(0) - 1)
    def _():
        o_ref[...] = _layer_norm(alpha * x_ref[...] + acc_ref[...], g_ref[...], b_ref[...])


def _ffn_sample(x2, state, w_up, b_up, w_conv, b_conv, w_down, g, b, alpha, ts, chunk):
    n, d = x2.shape
    d_ff = w_down.shape[0]
    nseq = n // ts
    nc = d_ff // chunk
    nst = CONV_WIDTH - 1
    kern = functools.partial(_ffn_sample_kernel, ts=ts, alpha=alpha)
    halves = lambda shape, ax: [pl.BlockSpec(shape, lambda c, o=o: (0,) * ax + (c + o,) + (0,) * (len(shape) - ax - 1))
                                for o in (0, nc)]
    return pl.pallas_call(
        kern,
        grid=(nc,),
        in_specs=[_const_spec((n, d))] + halves((nseq, nst, chunk), 2) + halves((d, chunk), 1)
                 + halves((1, chunk), 1) + halves((CONV_WIDTH, chunk), 1) + halves((1, chunk), 1)
                 + [pl.BlockSpec((chunk, d), lambda c: (c, 0)), _const_spec((1, d)), _const_spec((1, d))],
        out_specs=[pl.BlockSpec((n, d), lambda c: (0, 0)),
                   pl.BlockSpec((nseq, nst, chunk), lambda c: (0, 0, c)),
                   pl.BlockSpec((nseq, nst, chunk), lambda c: (0, 0, c))],
        out_shape=[jax.ShapeDtypeStruct((n, d), F32),
                   jax.ShapeDtypeStruct((nseq, nst, d_ff), F32),
                   jax.ShapeDtypeStruct((nseq, nst, d_ff), F32)],
        scratch_shapes=[pltpu.VMEM((n, d), F32), pltpu.VMEM((2, nseq, SUBLANES + ts, chunk), F32)],
        compiler_params=_cparams(("arbitrary",)),
        name="ffn_sample",
    )(x2, state, state, w_up, w_up, b_up, b_up, w_conv, w_conv, b_conv, b_conv, w_down, g, b)


def _ffn_chunk_size(d_ff):
    for c in (512, 256, 128):
        if d_ff % c == 0:
            return c
    raise ValueError("d_ff must be a multiple of 128")


def kernel(x_prompt, x_sample, cache_k, cache_v, cache_logf, state_pool, state_conv, page_table,
           w_in, b_f, w_pool, pool_scale, w_out, ln1_g, ln1_b, w_up, b_up, w_conv, b_conv, w_down,
           ln2_g, ln2_b):
    bsz, t_p, d = x_prompt.shape
    db, t_s, _ = x_sample.shape
    depth = w_in.shape[0]
    n_heads = cache_k.shape[3]
    att_w = n_heads * HEAD_DIM
    pool_w = state_pool.shape[3]
    d_ff = w_down.shape[1]
    n_past = page_table.shape[1] * PAGE_SIZE
    alpha = (2 * depth) ** 0.25
    assert HEAD_DIM * 2 == LANES and n_heads % 2 == 0 and n_heads <= SUBLANES
    assert t_s == SUBLANES and cache_k.shape[2] == PAGE_SIZE and t_p >= MAX_WINDOW
    assert w_in.shape[2] == 3 * att_w + n_heads + pool_w

    tm_proj = min(512, t_p)
    tq = min(256, t_p)
    tm_mix = min(1024, t_p)
    tm_ffn = min(256, t_p)
    chunk = _ffn_chunk_size(d_ff)

    yp, ys = x_prompt, x_sample
    outs = [[] for _ in range(10)]
    for l in range(depth):
        wi = w_in[l]
        fcols = wi[:, 3 * att_w:3 * att_w + n_heads]
        w_row = jnp.concatenate(
            [wi[:, :2 * att_w], wi[:, 3 * att_w + n_heads:], jnp.pad(fcols, ((0, 0), (0, LANES - n_heads)))],
            axis=1).astype(BF)
        w_t = jnp.concatenate(
            [wi[:, :3 * att_w].T, jnp.pad(fcols.T, ((0, BF_ROWS - n_heads), (0, 0)))],
            axis=0).astype(BF)
        bf_row = jnp.pad(b_f[l], (0, LANES - n_heads)).reshape(1, LANES)
        bf_col = b_f[l].reshape(n_heads, 1)
        wp_b = w_pool[l].astype(BF)
        ps = pool_scale[l].reshape(1, pool_w)
        wo_b = w_out[l].astype(BF)
        g1, b1 = ln1_g[l].reshape(1, d), ln1_b[l].reshape(1, d)
        wu_b = w_up[l].astype(BF)
        bu = b_up[l].reshape(1, 2 * d_ff)
        wc = w_conv[l]
        bc = b_conv[l].reshape(1, 2 * d_ff)
        wd_b = w_down[l].astype(BF)
        g2, b2 = ln2_g[l].reshape(1, d), ln2_b[l].reshape(1, d)

        qt, kb, kt, vt, vtb, u, lf, lft = _proj_prompt(yp, w_row, w_t, bf_row, bf_col, att_w, pool_w,
                                                       n_heads, tm_proj)
        n_s = db * t_s
        qs, ks, vs, kbs, vbs, us, lfs = _proj_sample(ys.reshape(n_s, d), w_row, w_t, bf_row, att_w, pool_w)
        att = _attn(qt, kb, vtb, lf, lft, tq)
        x1 = _mix_prompt(yp, att, u, wp_b, ps, wo_b, g1, b1, alpha, tm_mix)

        lfs3 = lfs.reshape(db, t_s, LANES)
        lfts = jnp.pad(jnp.swapaxes(lfs3[:, :, :n_heads], 1, 2), ((0, 0), (0, 0), (0, LANES - t_s)))
        padk = lambda a: jnp.pad(a.reshape(db, t_s, att_w), ((0, 0), (0, PAGE_SIZE - t_s), (0, 0)))
        ckt = jnp.transpose(cache_k[l], (0, 2, 3, 1))
        cvt = jnp.transpose(cache_v[l], (0, 2, 3, 1))
        clft = jnp.swapaxes(cache_logf[l], 1, 2)
        yp, tail, att_s = _ffn_prompt_and_paged_attn(
            x1, wu_b, bu, wc, bc, wd_b, g2, b2, alpha, tm_ffn, chunk,
            page_table, qs.reshape(db, t_s, att_w), padk(kbs), padk(vbs), lfs3, lfts, ckt, cvt, clft)
        to_bthd = lambda a: jnp.transpose(a.reshape(bsz, n_heads, HEAD_DIM, t_p), (0, 3, 1, 2))
        outs[0].append(to_bthd(kt))
        outs[1].append(to_bthd(vt))
        outs[2].append(jnp.swapaxes(lft, 1, 2))
        outs[3].append(u[:, t_p - (MAX_WINDOW - 1):, :])
        outs[4].append(tail[:, SUBLANES - (CONV_WIDTH - 1):, :])

        us3 = us.reshape(db, t_s, pool_w)
        full = jnp.concatenate([jnp.zeros((db, 1, pool_w), F32), state_pool[l], us3], axis=1)
        x1s = _mix_sample(ys.reshape(n_s, d), att_s.reshape(n_s, att_w), full, wp_b, ps, wo_b, g1, b1,
                          alpha, t_s, n_past)
        ys2, tail_g, tail_v = _ffn_sample(x1s, state_conv[l], wu_b, bu, wc, bc, wd_b, g2, b2, alpha, t_s, chunk)
        ys = ys2.reshape(db, t_s, d)
        outs[5].append(ks.reshape(db, t_s, n_heads, HEAD_DIM))
        outs[6].append(vs.reshape(db, t_s, n_heads, HEAD_DIM))
        outs[7].append(lfs3[:, :, :n_heads])
        outs[8].append(full[:, t_s + 1:, :])
        outs[9].append(jnp.concatenate([tail_g, tail_v], axis=-1))

    st = [jnp.stack(o, 0) for o in outs]
    return (yp, ys, st[0], st[1], st[2], st[3], st[4], st[5], st[6], st[7], st[8], st[9])
```

```python
import functools

import jax
import jax.numpy as jnp
import numpy as np
from jax import lax
from jax.experimental import pallas as pl
from jax.experimental.pallas import tpu as pltpu

BF = jnp.bfloat16
F32 = jnp.float32

HEAD_DIM = 64
PAGE_SIZE = 128
POOL_WINDOWS = (2, 4, 8, 16)
MAX_WINDOW = max(POOL_WINDOWS)
CONV_WIDTH = 3
LN_EPS = 1e-5
NEG_INF = -1e30
GELU_C1 = float(np.float32(np.sqrt(2.0 / np.pi)))
GELU_C0 = float(np.float32(0.044715))
LOG2E = float(np.log2(np.e))
LANES = 128
SUBLANES = 8
BF_ROWS = 16
VMEM_LIMIT = 56 * 1024 * 1024


def _cparams(sem):
    return pltpu.CompilerParams(dimension_semantics=sem, vmem_limit_bytes=VMEM_LIMIT)


def _const_spec(shape):
    nd = len(shape)
    return pl.BlockSpec(shape, lambda *_: (0,) * nd, pipeline_mode=pl.Buffered(1))


def _dot(a, b):
    return jnp.dot(a, b, preferred_element_type=F32)


def _dot_nt(a, b):
    return lax.dot_general(a, b, (((1,), (1,)), ((), ())), preferred_element_type=F32)


def _layer_norm(r, g, b):
    mu = jnp.mean(r, axis=-1, keepdims=True)
    d = r - mu
    var = jnp.mean(d * d, axis=-1, keepdims=True)
    return d * lax.rsqrt(var + LN_EPS) * g + b


def _scan_incl(x, axis):
    n = x.shape[axis]
    idx = lax.broadcasted_iota(jnp.int32, x.shape, axis)
    s = 1
    while s < n:
        x = x + jnp.where(idx >= s, pltpu.roll(x, s, axis), 0.0)
        s *= 2
    return x


def _proj_prompt_kernel(x_ref, wr_ref, wt_ref, bfr_ref, bfc_ref,
                        qt_ref, kb_ref, kt_ref, vt_ref, vtb_ref, u_ref, lf_ref, lft_ref,
                        *, att_w, pool_w, n_heads, scale):
    xb = x_ref[0].astype(BF)
    u_ref[0] = _dot(xb, wr_ref[:, 2 * att_w:2 * att_w + pool_w])
    z = _dot(xb, wr_ref[:, 2 * att_w + pool_w:2 * att_w + pool_w + LANES]) + bfr_ref[...]
    lf_ref[0] = jax.nn.log_sigmoid(z)
    qt_ref[0] = (_dot_nt(wt_ref[0:att_w, :], xb) * scale).astype(BF)
    kt = _dot_nt(wt_ref[att_w:2 * att_w, :], xb)
    kt_ref[0] = kt
    kb_ref[0] = kt.T.astype(BF)
    vt = _dot_nt(wt_ref[2 * att_w:3 * att_w, :], xb)
    vt_ref[0] = vt
    vtb_ref[0] = vt.astype(BF)
    zt = _dot_nt(wt_ref[3 * att_w:3 * att_w + BF_ROWS, :], xb)[0:n_heads] + bfc_ref[...]
    lft_ref[0] = jax.nn.log_sigmoid(zt)


def _proj_prompt(x, w_row, w_t, bf_row, bf_col, att_w, pool_w, n_heads, tm):
    bsz, t, d = x.shape
    kern = functools.partial(_proj_prompt_kernel, att_w=att_w, pool_w=pool_w, n_heads=n_heads,
                             scale=HEAD_DIM ** -0.5 * LOG2E)
    row = lambda w: pl.BlockSpec((1, tm, w), lambda i, j: (i, j, 0))
    col = lambda h: pl.BlockSpec((1, h, tm), lambda i, j: (i, 0, j))
    return pl.pallas_call(
        kern,
        grid=(bsz, t // tm),
        in_specs=[row(d), _const_spec(w_row.shape), _const_spec(w_t.shape),
                  _const_spec((1, LANES)), _const_spec((n_heads, 1))],
        out_specs=[col(att_w), row(att_w), col(att_w), col(att_w), col(att_w), row(pool_w),
                   row(LANES), col(n_heads)],
        out_shape=[jax.ShapeDtypeStruct((bsz, att_w, t), BF),
                   jax.ShapeDtypeStruct((bsz, t, att_w), BF),
                   jax.ShapeDtypeStruct((bsz, att_w, t), F32),
                   jax.ShapeDtypeStruct((bsz, att_w, t), F32),
                   jax.ShapeDtypeStruct((bsz, att_w, t), BF),
                   jax.ShapeDtypeStruct((bsz, t, pool_w), F32),
                   jax.ShapeDtypeStruct((bsz, t, LANES), F32),
                   jax.ShapeDtypeStruct((bsz, n_heads, t), F32)],
        compiler_params=_cparams(("parallel", "parallel")),
        name="proj_prompt",
    )(x, w_row, w_t, bf_row, bf_col)


def _proj_sample_kernel(x_ref, wr_ref, wt_ref, bfr_ref, q_ref, k_ref, v_ref, kb_ref, vb_ref, u_ref, lf_ref,
                        *, att_w, pool_w, scale):
    xb = x_ref[...].astype(BF)
    q_ref[...] = (_dot(xb, wr_ref[:, 0:att_w]) * scale).astype(BF)
    u_ref[...] = _dot(xb, wr_ref[:, 2 * att_w:2 * att_w + pool_w])
    z = _dot(xb, wr_ref[:, 2 * att_w + pool_w:2 * att_w + pool_w + LANES]) + bfr_ref[...]
    lf_ref[...] = jax.nn.log_sigmoid(z)
    k = _dot(xb, wr_ref[:, att_w:2 * att_w])
    k_ref[...] = k
    kb_ref[...] = k.astype(BF)
    v = _dot_nt(xb, wt_ref[2 * att_w:3 * att_w, :])
    v_ref[...] = v
    vb_ref[...] = v.astype(BF)


def _proj_sample(x2, w_row, w_t, bf_row, att_w, pool_w):
    n, d = x2.shape
    kern = functools.partial(_proj_sample_kernel, att_w=att_w, pool_w=pool_w, scale=HEAD_DIM ** -0.5)
    return pl.pallas_call(
        kern,
        out_shape=[jax.ShapeDtypeStruct((n, att_w), BF),
                   jax.ShapeDtypeStruct((n, att_w), F32),
                   jax.ShapeDtypeStruct((n, att_w), F32),
                   jax.ShapeDtypeStruct((n, att_w), BF),
                   jax.ShapeDtypeStruct((n, att_w), BF),
                   jax.ShapeDtypeStruct((n, pool_w), F32),
                   jax.ShapeDtypeStruct((n, LANES), F32)],
        compiler_params=pltpu.CompilerParams(vmem_limit_bytes=VMEM_LIMIT),
        name="proj_sample",
    )(x2, w_row, w_t, bf_row)


KEY_CHUNK = 32


def _group_reduce(fn, x):
    parts = [x[i * SUBLANES:(i + 1) * SUBLANES] for i in range(x.shape[0] // SUBLANES)]
    while len(parts) > 1:
        parts = [fn(parts[i], parts[i + 1]) if i + 1 < len(parts) else parts[i]
                 for i in range(0, len(parts), 2)]
    return parts[0]


AUG_ROWS = 16


def _split3(x):
    hi = x.astype(BF)
    r1 = x - hi.astype(F32)
    mid = r1.astype(BF)
    lo = (r1 - mid.astype(F32)).astype(BF)
    return hi, mid, lo


def _attn_kernel(qt_ref, qn_ref, k_ref, vt_ref, lf_ref, lft_ref, sel_ref, one_ref, o_ref,
                 ct_ref, kx_ref, qx_ref, ml_ref, mb_ref, *per_head, tq, n_heads):
    qi = pl.program_id(1)
    tk = tq
    rc = KEY_CHUNK
    t = k_ref.shape[1]
    n_pairs = n_heads // 2
    s_refs, p_refs, acc_refs = (per_head[i * n_heads:(i + 1) * n_heads] for i in range(3))

    @pl.when(qi == 0)
    def _():
        ct_ref[...] = _scan_incl(lft_ref[0], 1) * LOG2E
        blk = 512 if t % 512 == 0 else t
        for r0 in range(0, t, blk):
            rows = slice(r0, r0 + blk)
            c = _scan_incl(lf_ref[0, rows, :], 0)
            if r0:
                c = c + tot
            tot = c[blk - 1:blk, :]
            hi, mid, lo = _split3(c * LOG2E)
            aug = _dot(jnp.concatenate([hi, mid, lo], axis=1), sel_ref[...]) + one_ref[...]
            for pr in range(n_pairs):
                kx_ref[pr, rows, 0:LANES] = k_ref[0, rows, pr * LANES:(pr + 1) * LANES]
                kx_ref[pr, rows, LANES:2 * LANES] = aug[:, pr * LANES:(pr + 1) * LANES].astype(BF)
        qx_ref[:, LANES + AUG_ROWS:, :] = jnp.zeros((2 * n_heads, LANES - AUG_ROWS, tq), BF)

    q0 = pl.multiple_of(qi * tq, tq)
    par = lax.rem(qi, 2)
    first = lax.broadcasted_iota(jnp.int32, (LANES, 1), 0) < HEAD_DIM
    causal = (lax.broadcasted_iota(jnp.int32, (tk, tq), 0)
              <= lax.broadcasted_iota(jnp.int32, (tk, tq), 1))
    arow = lax.broadcasted_iota(jnp.int32, (AUG_ROWS, 1), 0)
    heads = range(n_heads)

    def build_queries(src_ref, half, start):
        for pair in range(n_pairs):
            qtp = src_ref[0, pair * LANES:(pair + 1) * LANES, :]
            zero = jnp.zeros_like(qtp)
            qx_ref[half * n_heads + 2 * pair, 0:LANES, :] = jnp.where(first, qtp, zero)
            qx_ref[half * n_heads + 2 * pair + 1, 0:LANES, :] = jnp.where(first, zero, qtp)
        for h in heads:
            hf = h % 2
            ci = ct_ref[h:h + 1, pl.ds(start, tq)]
            hi = ci.astype(BF).astype(F32)
            r1 = ci - hi
            mid = r1.astype(BF).astype(F32)
            lo = r1 - mid
            pick = ((arow >= 3 * hf) & (arow < 3 * hf + 3)).astype(F32)
            blk = jnp.where(arow == 6, hi, jnp.where(arow == 7, mid, jnp.where(arow == 8, lo, pick)))
            qx_ref[half * n_heads + h, LANES:LANES + AUG_ROWS, :] = blk.astype(BF)

    def scores(jb, masked, half):
        k0 = pl.multiple_of(jb * tk, tk)
        out = []
        for h in heads:
            s = _dot(kx_ref[h // 2, pl.ds(k0, tk), :], qx_ref[half * n_heads + h])
            out.append(jnp.where(causal, s, NEG_INF) if masked else s)
        return out

    def keep(h, s):
        s_refs[h][...] = s
        mb_ref[h:h + 1, :] = jnp.max(_group_reduce(jnp.maximum, s), axis=0, keepdims=True)

    def consume(jb, nxt=None):
        k0 = pl.multiple_of(jb * tk, tk)
        m_new, alpha = [], []
        for h in heads:
            m_old = ml_ref[2 * h:2 * h + 1, :]
            m_new.append(jnp.maximum(m_old, mb_ref[h:h + 1, :]))
            alpha.append(jnp.exp2(m_old - m_new[h]))
            ml_ref[2 * h:2 * h + 1, :] = m_new[h]
        for h in heads:
            lb = jnp.zeros((SUBLANES, tq), F32)
            for r in range(tk // rc):
                p = jnp.exp2(s_refs[h][r * rc:(r + 1) * rc, :] - m_new[h])
                lb = lb + _group_reduce(jnp.add, p)
                p_refs[h][r * rc:(r + 1) * rc, :] = p.astype(BF)
            ml_ref[2 * h + 1:2 * h + 2, :] = (alpha[h] * ml_ref[2 * h + 1:2 * h + 2, :]
                                              + jnp.sum(lb, axis=0, keepdims=True))
            if nxt is not None:
                keep(h, nxt[h])
        for h in heads:
            vh = vt_ref[0, h * HEAD_DIM:(h + 1) * HEAD_DIM, pl.ds(k0, tk)]
            acc_refs[h][...] = alpha[h] * acc_refs[h][...] + _dot(vh, p_refs[h][...])

    is_max = lax.broadcasted_iota(jnp.int32, (2 * n_heads, 1), 0) % 2 == 0
    ml_ref[...] = jnp.where(is_max, jnp.full((2 * n_heads, tq), NEG_INF, F32), 0.0)
    for h in heads:
        acc_refs[h][...] = jnp.zeros((HEAD_DIM, tq), F32)

    @pl.when(qi == 0)
    def _():
        build_queries(qt_ref, 0, 0)
        for h, s in enumerate(scores(0, True, 0)):
            keep(h, s)

    @pl.when(qi > 0)
    def _():
        def body(j, _):
            consume(j, scores(j + 1, False, par))
            return 0

        lax.fori_loop(0, qi - 1, body, 0)
        consume(qi - 1, scores(qi, True, par))

    has_next = qi + 1 < pl.num_programs(1)

    @pl.when(has_next)
    def _():
        build_queries(qn_ref, 1 - par, pl.multiple_of(q0 + tq, tq))
        consume(qi, scores(0, False, 1 - par))

    @pl.when(jnp.logical_not(has_next))
    def _():
        consume(qi)

    for pair in range(n_heads // 2):
        h0, h1 = 2 * pair, 2 * pair + 1
        out_t = jnp.concatenate([acc_refs[h0][...] / ml_ref[2 * h0 + 1:2 * h0 + 2, :],
                                 acc_refs[h1][...] / ml_ref[2 * h1 + 1:2 * h1 + 2, :]], axis=0)
        o_ref[0, :, pair * LANES:(pair + 1) * LANES] = out_t.T.astype(o_ref.dtype)


def _attn(qt, kb, vtb, lf, lft, tq):
    b, w, t = qt.shape
    n_heads = w // HEAD_DIM
    n_pairs = n_heads // 2
    sel = np.zeros((3 * LANES, n_pairs * LANES), np.float32)
    one = np.zeros((1, n_pairs * LANES), np.float32)
    for pr in range(n_pairs):
        for hf in range(2):
            for piece in range(3):
                sel[piece * LANES + 2 * pr + hf, pr * LANES + 3 * hf + piece] = -1.0
        one[0, pr * LANES + 6:pr * LANES + 9] = 1.0
    kern = functools.partial(_attn_kernel, tq=tq, n_heads=n_heads)
    per_b = lambda s1, s2: pl.BlockSpec((1, s1, s2), lambda i, j: (i, 0, 0))
    return pl.pallas_call(
        kern,
        grid=(b, t // tq),
        in_specs=[pl.BlockSpec((1, w, tq), lambda i, j: (i, 0, j)),
                  pl.BlockSpec((1, w, tq), lambda i, j: (i, 0, jnp.minimum(j + 1, t // tq - 1))),
                  per_b(t, w), per_b(w, t), per_b(t, LANES), per_b(n_heads, t),
                  _const_spec(sel.shape), _const_spec(one.shape)],
        out_specs=pl.BlockSpec((1, tq, w), lambda i, j: (i, j, 0)),
        out_shape=jax.ShapeDtypeStruct((b, t, w), BF),
        scratch_shapes=[pltpu.VMEM((n_heads, t), F32),
                        pltpu.VMEM((n_pairs, t, 2 * LANES), BF),
                        pltpu.VMEM((2 * n_heads, 2 * LANES, tq), BF),
                        pltpu.VMEM((2 * n_heads, tq), F32), pltpu.VMEM((n_heads, tq), F32)]
                       + [pltpu.VMEM((tq, tq), F32)] * n_heads
                       + [pltpu.VMEM((tq, tq), BF)] * n_heads
                       + [pltpu.VMEM((HEAD_DIM, tq), F32)] * n_heads,
        compiler_params=_cparams(("parallel", "arbitrary")),
        name="attn_prompt",
    )(qt, qt, kb, vtb, lf, lft, jnp.asarray(sel, BF), jnp.asarray(one, F32))


def _page_copies(step, slot, ng, pages, pt_ref, ck_hbm, cv_hbm, kbuf, vbuf, sem):
    bb = step // ng
    first = (ng - 1 - step % ng) * pages
    out = []
    for i in range(pages):
        pid = pt_ref[bb, first + i]
        out.append(pltpu.make_async_copy(ck_hbm.at[pid], kbuf.at[slot, i], sem.at[slot, 0]))
        out.append(pltpu.make_async_copy(cv_hbm.at[pid], vbuf.at[slot, i], sem.at[slot, 1]))
    return out


def _paged_step(step, n, ng, total, pt_ref, q_ref, kn_ref, vn_ref, lf_ref, lft_ref, tri_ref,
                clf_ref, ck_hbm, cv_hbm, o_ref, qbd_ref, ci_ref, m_ref, l_ref, acc_ref, carry_ref,
                kbuf, vbuf, sem, *, n_heads, pages, tq):
    b = step // ng
    g = step % ng
    mp = n * pages
    first_page = (ng - g - n) * pages
    rows = n_heads * tq
    w = q_ref.shape[2]
    dma = (pt_ref, ck_hbm, cv_hbm, kbuf, vbuf, sem)
    order = [(k, i) for k in reversed(range(n)) for i in range(pages)]

    for k in range(n):
        for cp in _page_copies(step + k, k, ng, pages, *dma):
            cp.wait()

    def expand(x):
        return jnp.concatenate(
            [jnp.broadcast_to(x[h:h + 1, :], (tq, x.shape[1])) for h in range(n_heads)], axis=0)

    def update(s, pv_fn):
        m = m_ref[...]
        m_new = jnp.maximum(m, jnp.max(s, axis=-1, keepdims=True))
        alpha = jnp.exp(m - m_new)
        p = jnp.exp(s - m_new)
        l_ref[...] = alpha * l_ref[...] + jnp.sum(p, axis=-1, keepdims=True)
        acc_ref[...] = alpha * acc_ref[...] + pv_fn(p.astype(BF))
        m_ref[...] = m_new

    @pl.when(g == 0)
    def _():
        q = q_ref[0]
        lane = lax.broadcasted_iota(jnp.int32, (1, w), 1)
        zero = jnp.zeros_like(q)
        qbd_ref[...] = jnp.concatenate(
            [jnp.where((lane >= h * HEAD_DIM) & (lane < (h + 1) * HEAD_DIM), q, zero)
             for h in range(n_heads)], axis=0)
        c = _scan_incl(lf_ref[0], 0)
        ci_ref[...] = jnp.concatenate([c[:, h:h + 1] for h in range(n_heads)], axis=0)
        m_ref[...] = jnp.full((rows, 1), NEG_INF, F32)
        l_ref[...] = jnp.zeros((rows, 1), F32)
        acc_ref[...] = jnp.zeros((rows, w), F32)
        carry_ref[...] = jnp.zeros((n_heads, PAGE_SIZE), F32)
        lfn = lft_ref[0]
        pieces = [p.astype(F32) for p in _split3(lfn)] + [jnp.zeros_like(lfn)]
        yn = _dot(jnp.concatenate(pieces, axis=0).astype(BF), tri_ref[...])
        yn = yn[0:n_heads] + yn[n_heads:2 * n_heads] + yn[2 * n_heads:3 * n_heads]
        cj = yn[:, PAGE_SIZE:2 * PAGE_SIZE] - yn[:, 0:PAGE_SIZE] + lfn
        s = _dot_nt(qbd_ref[...], kn_ref[0]) + ci_ref[...] - expand(cj)
        r_idx = lax.broadcasted_iota(jnp.int32, (rows, LANES), 0)
        c_idx = lax.broadcasted_iota(jnp.int32, (rows, LANES), 1)
        s = jnp.where(c_idx <= jnp.bitwise_and(r_idx, tq - 1), s, NEG_INF)
        update(s, lambda p: _dot(p, vn_ref[0]))

    ktc = jnp.concatenate([kbuf[k, i].reshape(w, PAGE_SIZE) for k, i in order], axis=1).astype(BF)
    vtc = jnp.concatenate([vbuf[k, i].reshape(w, PAGE_SIZE) for k, i in order], axis=1).astype(BF)
    lf = jnp.concatenate([clf_ref[pt_ref[b, first_page + i]] for i in range(mp)], axis=0)
    hi, mid, lo = _split3(lf)
    y = _dot(jnp.concatenate([hi, mid, lo], axis=0), tri_ref[...])
    ph = mp * n_heads
    y = y[0:ph] + y[ph:2 * ph] + y[2 * ph:3 * ph]
    run = carry_ref[...]
    cjs = [None] * mp
    for pg in reversed(range(mp)):
        blk = slice(pg * n_heads, (pg + 1) * n_heads)
        cjs[pg] = -(run + (y[blk, 0:PAGE_SIZE] - lf[blk]))
        run = run + y[blk, PAGE_SIZE:2 * PAGE_SIZE]
    carry_ref[...] = run
    cj = jnp.concatenate(cjs, axis=1)
    s = _dot(qbd_ref[...], ktc) + ci_ref[...] - expand(cj)
    update(s, lambda p: _dot_nt(p, vtc))

    @pl.when(g + n == ng)
    def _():
        lane = lax.broadcasted_iota(jnp.int32, (1, w), 1)
        out = jnp.zeros((tq, w), F32)
        for h in range(n_heads):
            sl = slice(h * tq, (h + 1) * tq)
            val = acc_ref[sl, :] / l_ref[sl, :]
            out = jnp.where((lane >= h * HEAD_DIM) & (lane < (h + 1) * HEAD_DIM), val, out)
        o_ref[0] = out.astype(o_ref.dtype)

    @pl.when(step + n < total)
    def _():
        for k in range(n):
            for cp in _page_copies(step + n + k, k, ng, pages, *dma):
                cp.start()


def _paged_operands(q, cache_kt, cache_lft, pages, slots, seq_of_step):
    db, tq, w = q.shape
    n_heads = w // HEAD_DIM
    rows = n_heads * tq
    assert tq & (tq - 1) == 0
    per_seq = lambda s1, s2: pl.BlockSpec((1, s1, s2), lambda i, j, pt: (seq_of_step(i, j), 0, 0))
    in_specs = [per_seq(tq, w), per_seq(PAGE_SIZE, w), per_seq(PAGE_SIZE, w), per_seq(tq, LANES),
                per_seq(n_heads, LANES),
                _const_spec((PAGE_SIZE, 2 * PAGE_SIZE)), _const_spec(cache_lft.shape),
                pl.BlockSpec(memory_space=pl.ANY), pl.BlockSpec(memory_space=pl.ANY)]
    page_buf = pltpu.VMEM((slots, pages, n_heads, HEAD_DIM, PAGE_SIZE), cache_kt.dtype)
    scratch = [pltpu.VMEM((rows, w), BF), pltpu.VMEM((rows, 1), F32),
               pltpu.VMEM((rows, 1), F32), pltpu.VMEM((rows, 1), F32),
               pltpu.VMEM((rows, w), F32), pltpu.VMEM((n_heads, PAGE_SIZE), F32),
               page_buf, page_buf, pltpu.SemaphoreType.DMA((slots, 2))]
    pos = lax.broadcasted_iota(jnp.int32, (PAGE_SIZE, PAGE_SIZE), 0)
    col = lax.broadcasted_iota(jnp.int32, (PAGE_SIZE, PAGE_SIZE), 1)
    tri = jnp.concatenate([(pos >= col).astype(BF), jnp.ones((PAGE_SIZE, PAGE_SIZE), BF)], axis=1)
    return in_specs, per_seq(tq, w), scratch, tri


def _mix_tail(pooled, att, x, wp_ref, ps_ref, wo_ref, g_ref, b_ref, alpha):
    gd = pooled[0].shape[1]
    att_w = att.shape[1]
    mixed = [_dot(pooled[g].astype(BF), wp_ref[g]) * ps_ref[:, g * gd:(g + 1) * gd]
             for g in range(len(pooled))]
    pool = jnp.concatenate(mixed, axis=1).astype(BF)
    mix = _dot(att, wo_ref[0:att_w, :]) + _dot(pool, wo_ref[att_w:, :])
    return _layer_norm(alpha * x + mix, g_ref[...], b_ref[...])


def _shift_down(x, s):
    n8, c = x.shape[0] // SUBLANES, x.shape[1]
    if s == SUBLANES:
        return jnp.concatenate([x[0:SUBLANES], x[:-SUBLANES]], axis=0)
    r = pltpu.roll(x.reshape(n8, SUBLANES, c), s, 1)
    prev = jnp.concatenate([r[0:1], r[:-1]], axis=0)
    sub = lax.broadcasted_iota(jnp.int32, (1, SUBLANES, c), 1)
    return jnp.where(sub < s, prev, r).reshape(n8 * SUBLANES, c)


def _mix_prompt_kernel(x_ref, att_ref, u_ref, halo_ref, wp_ref, ps_ref, wo_ref, g_ref, b_ref,
                       o_ref, *, tm, sub, alpha):
    t = pl.program_id(1)
    hl = MAX_WINDOW
    started = (t > 0).astype(F32)
    gd = u_ref.shape[2] // len(POOL_WINDOWS)
    att_w = att_ref.shape[2]
    blocks = [slice(r0, r0 + sub) for r0 in range(0, tm, sub)]
    pooled = []
    for rows in blocks:
        pos = t * tm + rows.start + lax.broadcasted_iota(jnp.int32, (sub, 1), 0)
        per_group = []
        for g, w in enumerate(POOL_WINDOWS):
            lanes = slice(g * gd, (g + 1) * gd)
            tok = u_ref[0, rows, lanes]
            halo = (halo_ref[0, :, lanes] * started if rows.start == 0
                    else u_ref[0, rows.start - hl:rows.start, lanes])
            win = jnp.concatenate([halo, tok], axis=0)
            k = 1
            while k < w:
                win = win + _shift_down(win, k)
                k *= 2
            cnt = jnp.minimum(pos + 1, w).astype(F32)
            per_group.append(win[hl:] / cnt - tok)
        pooled.append(per_group)
    pools = [jnp.concatenate([_dot(pg[g].astype(BF), wp_ref[g]) * ps_ref[:, g * gd:(g + 1) * gd]
                              for g in range(len(pg))], axis=1).astype(BF) for pg in pooled]
    mixes = [_dot(att_ref[0, rows, :], wo_ref[0:att_w, :]) + _dot(pool, wo_ref[att_w:, :])
             for rows, pool in zip(blocks, pools)]
    for rows, mix in zip(blocks, mixes):
        o_ref[0, rows, :] = _layer_norm(alpha * x_ref[0, rows, :] + mix, g_ref[...], b_ref[...])


def _mix_prompt(x, att, u, w_pool, pool_scale, w_out, g, b, alpha, tm):
    bsz, t, d = x.shape
    att_w, pool_w = att.shape[2], u.shape[2]
    hl = MAX_WINDOW
    ng, gd = w_pool.shape[0], w_pool.shape[1]
    kern = functools.partial(_mix_prompt_kernel, tm=tm, sub=min(tm, 256), alpha=alpha)
    row = lambda w: pl.BlockSpec((1, tm, w), lambda i, j: (i, j, 0))
    halo = pl.BlockSpec((1, hl, pool_w), lambda i, j: (i, jnp.maximum(j * (tm // hl) - 1, 0), 0))
    return pl.pallas_call(
        kern,
        grid=(bsz, t // tm),
        in_specs=[row(d), row(att_w), row(pool_w), halo,
                  _const_spec((ng, gd, gd)), _const_spec((1, pool_w)), _const_spec((att_w + pool_w, d)),
                  _const_spec((1, d)), _const_spec((1, d))],
        out_specs=row(d),
        out_shape=jax.ShapeDtypeStruct((bsz, t, d), F32),
        compiler_params=_cparams(("parallel", "parallel")),
        name="mix_prompt",
    )(x, att, u, u, w_pool, pool_scale, w_out, g, b)


def _mix_sample_kernel(x_ref, att_ref, full_ref, wp_ref, ps_ref, wo_ref, g_ref, b_ref, o_ref,
                       *, ts, n_past, alpha):
    nseq = full_ref.shape[0]
    hl = MAX_WINDOW
    gd = full_ref.shape[2] // len(POOL_WINDOWS)
    pos = n_past + lax.broadcasted_iota(jnp.int32, (1, ts, 1), 1)
    pooled = []
    for g, w in enumerate(POOL_WINDOWS):
        lanes = slice(g * gd, (g + 1) * gd)
        tok = full_ref[:, hl:hl + ts, lanes]
        acc = tok
        for k in range(1, w):
            acc = acc + full_ref[:, hl - k:hl - k + ts, lanes]
        cnt = jnp.minimum(pos + 1, w).astype(F32)
        pooled.append((acc / cnt - tok).reshape(nseq * ts, gd))
    o_ref[...] = _mix_tail(pooled, att_ref[...], x_ref[...], wp_ref, ps_ref, wo_ref, g_ref, b_ref, alpha)


def _mix_sample(x2, att2, full, w_pool, pool_scale, w_out, g, b, alpha, ts, n_past):
    n, d = x2.shape
    kern = functools.partial(_mix_sample_kernel, ts=ts, n_past=n_past, alpha=alpha)
    return pl.pallas_call(
        kern,
        out_shape=jax.ShapeDtypeStruct((n, d), F32),
        compiler_params=pltpu.CompilerParams(vmem_limit_bytes=VMEM_LIMIT),
        name="mix_sample",
    )(x2, att2, full, w_pool, pool_scale, w_out, g, b)


def _shift_rows(h, s):
    n8, c = h.shape[0] // SUBLANES, h.shape[1]
    r = pltpu.roll(h.reshape(n8, SUBLANES, c), s, 1)
    sub = lax.broadcasted_iota(jnp.int32, (1, SUBLANES, c), 1)
    return jnp.where(sub < s, r[:-1], r[1:]).reshape((n8 - 1) * SUBLANES, c)


def _ffn_chunk(hg3, hv3, wc_ref, bc_ref, gcols, vcols, wcv_ref=None, bcv_ref=None):
    def conv(h3, w_ref, b_ref, cols, scale):
        hc = b_ref[:, cols] * scale + h3[0] * (w_ref[0:1, cols] * scale)
        for j in range(1, CONV_WIDTH):
            hc = hc + h3[j] * (w_ref[j:j + 1, cols] * scale)
        return hc
    g = conv(hg3, wc_ref, bc_ref, gcols, 1.0)
    vh = conv(hv3, wc_ref if wcv_ref is None else wcv_ref, bc_ref if bcv_ref is None else bcv_ref,
              vcols, 0.5)
    t = jnp.tanh(g * (GELU_C1 + (GELU_C1 * GELU_C0) * (g * g)))
    return ((g * (1.0 + t)) * vh).astype(BF)


def _ffn_prompt_kernel(pt_ref, x_ref, halo_ref, wu_ref, bu_ref, wc_ref, bc_ref, wd_ref, g_ref, b_ref,
                       *rest, tm, d_ff, chunk, alpha, paged):
    sub, ng, total = paged["sub"], paged["ng"], paged["total"]
    pa_in, (o_ref, tail_ref, oa_ref), pa_scratch = rest[0:9], rest[9:12], rest[12:]
    dma = (pt_ref, pa_in[7], pa_in[8], pa_scratch[6], pa_scratch[7], pa_scratch[8])
    step0 = (pl.program_id(0) * pl.num_programs(1) + pl.program_id(1)) * sub

    @pl.when(step0 == 0)
    def _():
        for k in range(sub):
            for cp in _page_copies(k, k, ng, paged["pages"], *dma):
                cp.start()

    t = pl.program_id(1)
    hl = SUBLANES
    edges = list(range(0, d_ff, chunk)) + [d_ff]
    n_chunks = len(edges) - 1
    cols_of = lambda c: (slice(edges[c], edges[c + 1]), slice(d_ff + edges[c], d_ff + edges[c + 1]))
    st = {}

    def up(c):
        return [_dot(st["xin"], wu_ref[:, cols]) for cols in cols_of(c)]

    def shifted(hd, cols):
        body = hd[hl:] + bu_ref[:, cols]
        tail_ref[0, :, cols] = body[tm - hl:tm]
        h = jnp.concatenate([(hd[0:hl] + bu_ref[:, cols]) * st["started"], body], axis=0)
        return [_shift_rows(h, CONV_WIDTH - 1 - j) for j in range(CONV_WIDTH - 1)] + [body]

    def run_chunks(c_from, c_to):
        for c in range(c_from, c_to):
            cur = st["nxt"]
            if c + 1 < n_chunks:
                st["nxt"] = up(c + 1)
            gcols, vcols = cols_of(c)
            act = _ffn_chunk(shifted(cur[0], gcols), shifted(cur[1], vcols), wc_ref, bc_ref, gcols, vcols)
            st["acc"] = st["acc"] + _dot(act, wd_ref[gcols, :])

    _paged_step(step0, sub, ng, total, pt_ref, *pa_in, oa_ref, *pa_scratch,
                n_heads=paged["n_heads"], pages=paged["pages"], tq=paged["tq"])

    st["x"] = x_ref[0]
    st["xin"] = jnp.concatenate([halo_ref[0], st["x"]], axis=0).astype(BF)
    st["started"] = (t > 0).astype(F32)
    st["acc"] = jnp.zeros((tm, st["x"].shape[1]), F32)
    st["nxt"] = up(0)
    run_chunks(0, n_chunks)
    o_ref[0] = _layer_norm(alpha * st["x"] + st["acc"], g_ref[...], b_ref[...])


def _ffn_prompt_and_paged_attn(x1, w_up, b_up, w_conv, b_conv, w_down, g, b, alpha, tm, chunk,
                               page_table, q, kn, vn, lf, lft, cache_kt, cache_vt, cache_lft):
    bsz, t, d = x1.shape
    d_ff = w_down.shape[0]
    hl = SUBLANES
    nt = t // tm
    db, tq, w = q.shape
    n_pages = page_table.shape[1]
    fits = lambda p: (n_pages % p == 0 and (db * (n_pages // p)) % (bsz * nt) == 0
                      and (n_pages // p) % ((db * (n_pages // p)) // (bsz * nt)) == 0)
    pages = next(p for p in (16, 8, 4, 2, 1) if fits(p))
    ng = n_pages // pages
    total = db * ng
    sub = total // (bsz * nt)
    paged = dict(sub=sub, ng=ng, total=total, pages=pages, n_heads=w // HEAD_DIM, tq=tq)
    pa_specs, pa_out, pa_scratch, tri = _paged_operands(
        q, cache_kt, cache_lft, pages, sub, lambda i, j: ((i * nt + j) * sub) // ng)
    kern = functools.partial(_ffn_prompt_kernel, tm=tm, d_ff=d_ff, chunk=chunk, alpha=alpha, paged=paged)
    row = pl.BlockSpec((1, tm, d), lambda i, j, pt: (i, j, 0))
    halo = pl.BlockSpec((1, hl, d), lambda i, j, pt: (i, jnp.maximum(j * (tm // hl) - 1, 0), 0))
    grid_spec = pltpu.PrefetchScalarGridSpec(
        num_scalar_prefetch=1,
        grid=(bsz, nt),
        in_specs=[row, halo, _const_spec((d, 2 * d_ff)), _const_spec((1, 2 * d_ff)),
                  _const_spec((CONV_WIDTH, 2 * d_ff)), _const_spec((1, 2 * d_ff)),
                  _const_spec((d_ff, d)), _const_spec((1, d)), _const_spec((1, d))] + pa_specs,
        out_specs=[row, pl.BlockSpec((1, hl, 2 * d_ff), lambda i, j, pt: (i, 0, 0)), pa_out],
        scratch_shapes=pa_scratch)
    return pl.pallas_call(
        kern,
        grid_spec=grid_spec,
        out_shape=[jax.ShapeDtypeStruct((bsz, t, d), F32),
                   jax.ShapeDtypeStruct((bsz, hl, 2 * d_ff), F32),
                   jax.ShapeDtypeStruct((db, tq, w), BF)],
        compiler_params=_cparams(("arbitrary", "arbitrary")),
        name="ffn_prompt_attn_paged",
    )(page_table, x1, x1, w_up, b_up, w_conv, b_conv, w_down, g, b,
      q, kn, vn, lf, lft, tri, cache_lft, cache_kt, cache_vt)


def _ffn_sample_kernel(x_ref, stg_ref, stv_ref, wug_ref, wuv_ref, bug_ref, buv_ref, wcg_ref, wcv_ref,
                       bcg_ref, bcv_ref, wd_ref, g_ref, b_ref, o_ref, tg_ref, tv_ref, acc_ref, buf_ref,
                       *, ts, alpha):
    c = pl.program_id(0)
    nseq = stg_ref.shape[0]
    n = nseq * ts
    chunk = wd_ref.shape[0]
    hl = SUBLANES
    nst = CONV_WIDTH - 1
    cols = slice(0, chunk)
    xb = x_ref[...].astype(BF)

    @pl.when(c == 0)
    def _():
        acc_ref[...] = jnp.zeros(acc_ref.shape, F32)

    h3 = []
    for half, (wu, bu, st, tail) in enumerate(((wug_ref, bug_ref, stg_ref, tg_ref),
                                               (wuv_ref, buv_ref, stv_ref, tv_ref))):
        h = (_dot(xb, wu[...]) + bu[...]).reshape(nseq, ts, chunk)
        tail[...] = h[:, ts - nst:, :]
        buf_ref[half, :, hl - nst:hl, :] = st[...]
        buf_ref[half, :, hl:hl + ts, :] = h
        h3.append([buf_ref[half, :, hl - 2 + j:hl - 2 + j + ts, :].reshape(n, chunk)
                   for j in range(CONV_WIDTH)])
    act = _ffn_chunk(h3[0], h3[1], wcg_ref, bcg_ref, cols, cols, wcv_ref, bcv_ref)
    acc_ref[...] += _dot(act, wd_ref[...])

    @pl.when(c == pl.num_programs(0) - 1)
    def _():
        o_ref[...] = _layer_norm(alpha * x_ref[...] + acc_ref[...], g_ref[...], b_ref[...])


def _ffn_sample(x2, state, w_up, b_up, w_conv, b_conv, w_down, g, b, alpha, ts, chunk):
    n, d = x2.shape
    d_ff = w_down.shape[0]
    nseq = n // ts
    nc = d_ff // chunk
    nst = CONV_WIDTH - 1
    kern = functools.partial(_ffn_sample_kernel, ts=ts, alpha=alpha)
    halves = lambda shape, ax: [pl.BlockSpec(shape, lambda c, o=o: (0,) * ax + (c + o,) + (0,) * (len(shape) - ax - 1))
                                for o in (0, nc)]
    return pl.pallas_call(
        kern,
        grid=(nc,),
        in_specs=[_const_spec((n, d))] + halves((nseq, nst, chunk), 2) + halves((d, chunk), 1)
                 + halves((1, chunk), 1) + halves((CONV_WIDTH, chunk), 1) + halves((1, chunk), 1)
                 + [pl.BlockSpec((chunk, d), lambda c: (c, 0)), _const_spec((1, d)), _const_spec((1, d))],
        out_specs=[pl.BlockSpec((n, d), lambda c: (0, 0)),
                   pl.BlockSpec((nseq, nst, chunk), lambda c: (0, 0, c)),
                   pl.BlockSpec((nseq, nst, chunk), lambda c: (0, 0, c))],
        out_shape=[jax.ShapeDtypeStruct((n, d), F32),
                   jax.ShapeDtypeStruct((nseq, nst, d_ff), F32),
                   jax.ShapeDtypeStruct((nseq, nst, d_ff), F32)],
        scratch_shapes=[pltpu.VMEM((n, d), F32), pltpu.VMEM((2, nseq, SUBLANES + ts, chunk), F32)],
        compiler_params=_cparams(("arbitrary",)),
        name="ffn_sample",
    )(x2, state, state, w_up, w_up, b_up, b_up, w_conv, w_conv, b_conv, b_conv, w_down, g, b)


def _ffn_chunk_size(d_ff):
    for c in (512, 256, 128):
        if d_ff % c == 0:
            return c
    raise ValueError("d_ff must be a multiple of 128")


def kernel(x_prompt, x_sample, cache_k, cache_v, cache_logf, state_pool, state_conv, page_table,
           w_in, b_f, w_pool, pool_scale, w_out, ln1_g, ln1_b, w_up, b_up, w_conv, b_conv, w_down,
           ln2_g, ln2_b):
    bsz, t_p, d = x_prompt.shape
    db, t_s, _ = x_sample.shape
    depth = w_in.shape[0]
    n_heads = cache_k.shape[3]
    att_w = n_heads * HEAD_DIM
    pool_w = state_pool.shape[3]
    d_ff = w_down.shape[1]
    n_past = page_table.shape[1] * PAGE_SIZE
    alpha = (2 * depth) ** 0.25
    assert HEAD_DIM * 2 == LANES and n_heads % 2 == 0 and n_heads <= SUBLANES
    assert t_s == SUBLANES and cache_k.shape[2] == PAGE_SIZE and t_p >= MAX_WINDOW
    assert w_in.shape[2] == 3 * att_w + n_heads + pool_w

    tm_proj = min(1024, t_p)
    tq = min(256, t_p)
    tm_mix = min(1024, t_p)
    tm_ffn = min(256, t_p)
    chunk = _ffn_chunk_size(d_ff)

    yp, ys = x_prompt, x_sample
    outs = [[] for _ in range(10)]
    for l in range(depth):
        wi = w_in[l]
        fcols = wi[:, 3 * att_w:3 * att_w + n_heads]
        w_row = jnp.concatenate(
            [wi[:, :2 * att_w], wi[:, 3 * att_w + n_heads:], jnp.pad(fcols, ((0, 0), (0, LANES - n_heads)))],
            axis=1).astype(BF)
        w_t = jnp.concatenate(
            [wi[:, :3 * att_w].T, jnp.pad(fcols.T, ((0, BF_ROWS - n_heads), (0, 0)))],
            axis=0).astype(BF)
        bf_row = jnp.pad(b_f[l], (0, LANES - n_heads)).reshape(1, LANES)
        bf_col = b_f[l].reshape(n_heads, 1)
        wp_b = w_pool[l].astype(BF)
        ps = pool_scale[l].reshape(1, pool_w)
        wo_b = w_out[l].astype(BF)
        g1, b1 = ln1_g[l].reshape(1, d), ln1_b[l].reshape(1, d)
        wu_b = w_up[l].astype(BF)
        bu = b_up[l].reshape(1, 2 * d_ff)
        wc = w_conv[l]
        bc = b_conv[l].reshape(1, 2 * d_ff)
        wd_b = w_down[l].astype(BF)
        g2, b2 = ln2_g[l].reshape(1, d), ln2_b[l].reshape(1, d)

        qt, kb, kt, vt, vtb, u, lf, lft = _proj_prompt(yp, w_row, w_t, bf_row, bf_col, att_w, pool_w,
                                                       n_heads, tm_proj)
        n_s = db * t_s
        qs, ks, vs, kbs, vbs, us, lfs = _proj_sample(ys.reshape(n_s, d), w_row, w_t, bf_row, att_w, pool_w)
        att = _attn(qt, kb, vtb, lf, lft, tq)
        x1 = _mix_prompt(yp, att, u, wp_b, ps, wo_b, g1, b1, alpha, tm_mix)

        lfs3 = lfs.reshape(db, t_s, LANES)
        lfts = jnp.pad(jnp.swapaxes(lfs3[:, :, :n_heads], 1, 2), ((0, 0), (0, 0), (0, LANES - t_s)))
        padk = lambda a: jnp.pad(a.reshape(db, t_s, att_w), ((0, 0), (0, PAGE_SIZE - t_s), (0, 0)))
        ckt = jnp.transpose(cache_k[l], (0, 2, 3, 1))
        cvt = jnp.transpose(cache_v[l], (0, 2, 3, 1))
        clft = jnp.swapaxes(cache_logf[l], 1, 2)
        yp, tail, att_s = _ffn_prompt_and_paged_attn(
            x1, wu_b, bu, wc, bc, wd_b, g2, b2, alpha, tm_ffn, chunk,
            page_table, qs.reshape(db, t_s, att_w), padk(kbs), padk(vbs), lfs3, lfts, ckt, cvt, clft)
        to_bthd = lambda a: jnp.transpose(a.reshape(bsz, n_heads, HEAD_DIM, t_p), (0, 3, 1, 2))
        outs[0].append(to_bthd(kt))
        outs[1].append(to_bthd(vt))
        outs[2].append(jnp.swapaxes(lft, 1, 2))
        outs[3].append(u[:, t_p - (MAX_WINDOW - 1):, :])
        outs[4].append(tail[:, SUBLANES - (CONV_WIDTH - 1):, :])

        us3 = us.reshape(db, t_s, pool_w)
        full = jnp.concatenate([jnp.zeros((db, 1, pool_w), F32), state_pool[l], us3], axis=1)
        x1s = _mix_sample(ys.reshape(n_s, d), att_s.reshape(n_s, att_w), full, wp_b, ps, wo_b, g1, b1,
                          alpha, t_s, n_past)
        ys2, tail_g, tail_v = _ffn_sample(x1s, state_conv[l], wu_b, bu, wc, bc, wd_b, g2, b2, alpha, t_s, chunk)
        ys = ys2.reshape(db, t_s, d)
        outs[5].append(ks.reshape(db, t_s, n_heads, HEAD_DIM))
        outs[6].append(vs.reshape(db, t_s, n_heads, HEAD_DIM))
        outs[7].append(lfs3[:, :, :n_heads])
        outs[8].append(full[:, t_s + 1:, :])
        outs[9].append(jnp.concatenate([tail_g, tail_v], axis=-1))

    st = [jnp.stack(o, 0) for o in outs]
    return (yp, ys, st[0], st[1], st[2], st[3], st[4], st[5], st[6], st[7], st[8], st[9])
```

```python
import functools

import jax
import jax.numpy as jnp
import numpy as np
from jax import lax
from jax.experimental import pallas as pl
from jax.experimental.pallas import tpu as pltpu

BF = jnp.bfloat16
F32 = jnp.float32

HEAD_DIM = 64
PAGE_SIZE = 128
POOL_WINDOWS = (2, 4, 8, 16)
MAX_WINDOW = max(POOL_WINDOWS)
CONV_WIDTH = 3
LN_EPS = 1e-5
NEG_INF = -1e30
GELU_C1 = float(np.float32(np.sqrt(2.0 / np.pi)))
GELU_C0 = float(np.float32(0.044715))
LOG2E = float(np.log2(np.e))
LANES = 128
SUBLANES = 8
BF_ROWS = 16
VMEM_LIMIT = 56 * 1024 * 1024


def _cparams(sem):
    return pltpu.CompilerParams(dimension_semantics=sem, vmem_limit_bytes=VMEM_LIMIT)


def _const_spec(shape):
    nd = len(shape)
    return pl.BlockSpec(shape, lambda *_: (0,) * nd, pipeline_mode=pl.Buffered(1))


def _dot(a, b):
    return jnp.dot(a, b, preferred_element_type=F32)


def _dot_nt(a, b):
    return lax.dot_general(a, b, (((1,), (1,)), ((), ())), preferred_element_type=F32)


def _layer_norm(r, g, b):
    mu = jnp.mean(r, axis=-1, keepdims=True)
    d = r - mu
    var = jnp.mean(d * d, axis=-1, keepdims=True)
    return d * lax.rsqrt(var + LN_EPS) * g + b


def _scan_incl(x, axis):
    n = x.shape[axis]
    idx = lax.broadcasted_iota(jnp.int32, x.shape, axis)
    s = 1
    while s < n:
        x = x + jnp.where(idx >= s, pltpu.roll(x, s, axis), 0.0)
        s *= 2
    return x


def _proj_prompt_kernel(x_ref, wr_ref, wt_ref, bfr_ref, bfc_ref,
                        qt_ref, kb_ref, kt_ref, vt_ref, vtb_ref, u_ref, lf_ref, lft_ref,
                        *, att_w, pool_w, n_heads, scale):
    xb = x_ref[0].astype(BF)
    u_ref[0] = _dot(xb, wr_ref[:, 2 * att_w:2 * att_w + pool_w])
    z = _dot(xb, wr_ref[:, 2 * att_w + pool_w:2 * att_w + pool_w + LANES]) + bfr_ref[...]
    lf_ref[0] = jax.nn.log_sigmoid(z)
    qt_ref[0] = (_dot_nt(wt_ref[0:att_w, :], xb) * scale).astype(BF)
    kt = _dot_nt(wt_ref[att_w:2 * att_w, :], xb)
    kt_ref[0] = kt
    kb_ref[0] = kt.T.astype(BF)
    vt = _dot_nt(wt_ref[2 * att_w:3 * att_w, :], xb)
    vt_ref[0] = vt
    vtb_ref[0] = vt.astype(BF)
    zt = _dot_nt(wt_ref[3 * att_w:3 * att_w + BF_ROWS, :], xb)[0:n_heads] + bfc_ref[...]
    lft_ref[0] = jax.nn.log_sigmoid(zt)


def _proj_prompt(x, w_row, w_t, bf_row, bf_col, att_w, pool_w, n_heads, tm):
    bsz, t, d = x.shape
    kern = functools.partial(_proj_prompt_kernel, att_w=att_w, pool_w=pool_w, n_heads=n_heads,
                             scale=HEAD_DIM ** -0.5 * LOG2E)
    row = lambda w: pl.BlockSpec((1, tm, w), lambda i, j: (i, j, 0))
    col = lambda h: pl.BlockSpec((1, h, tm), lambda i, j: (i, 0, j))
    return pl.pallas_call(
        kern,
        grid=(bsz, t // tm),
        in_specs=[row(d), _const_spec(w_row.shape), _const_spec(w_t.shape),
                  _const_spec((1, LANES)), _const_spec((n_heads, 1))],
        out_specs=[col(att_w), row(att_w), col(att_w), col(att_w), col(att_w), row(pool_w),
                   row(LANES), col(n_heads)],
        out_shape=[jax.ShapeDtypeStruct((bsz, att_w, t), BF),
                   jax.ShapeDtypeStruct((bsz, t, att_w), BF),
                   jax.ShapeDtypeStruct((bsz, att_w, t), F32),
                   jax.ShapeDtypeStruct((bsz, att_w, t), F32),
                   jax.ShapeDtypeStruct((bsz, att_w, t), BF),
                   jax.ShapeDtypeStruct((bsz, t, pool_w), F32),
                   jax.ShapeDtypeStruct((bsz, t, LANES), F32),
                   jax.ShapeDtypeStruct((bsz, n_heads, t), F32)],
        compiler_params=_cparams(("parallel", "parallel")),
        name="proj_prompt",
    )(x, w_row, w_t, bf_row, bf_col)


def _proj_sample_kernel(x_ref, wr_ref, wt_ref, bfr_ref, q_ref, k_ref, v_ref, kb_ref, vb_ref, u_ref, lf_ref,
                        *, att_w, pool_w, scale):
    xb = x_ref[...].astype(BF)
    q_ref[...] = (_dot(xb, wr_ref[:, 0:att_w]) * scale).astype(BF)
    u_ref[...] = _dot(xb, wr_ref[:, 2 * att_w:2 * att_w + pool_w])
    z = _dot(xb, wr_ref[:, 2 * att_w + pool_w:2 * att_w + pool_w + LANES]) + bfr_ref[...]
    lf_ref[...] = jax.nn.log_sigmoid(z)
    k = _dot(xb, wr_ref[:, att_w:2 * att_w])
    k_ref[...] = k
    kb_ref[...] = k.astype(BF)
    v = _dot_nt(xb, wt_ref[2 * att_w:3 * att_w, :])
    v_ref[...] = v
    vb_ref[...] = v.astype(BF)


def _proj_sample(x2, w_row, w_t, bf_row, att_w, pool_w):
    n, d = x2.shape
    kern = functools.partial(_proj_sample_kernel, att_w=att_w, pool_w=pool_w, scale=HEAD_DIM ** -0.5)
    return pl.pallas_call(
        kern,
        out_shape=[jax.ShapeDtypeStruct((n, att_w), BF),
                   jax.ShapeDtypeStruct((n, att_w), F32),
                   jax.ShapeDtypeStruct((n, att_w), F32),
                   jax.ShapeDtypeStruct((n, att_w), BF),
                   jax.ShapeDtypeStruct((n, att_w), BF),
                   jax.ShapeDtypeStruct((n, pool_w), F32),
                   jax.ShapeDtypeStruct((n, LANES), F32)],
        compiler_params=pltpu.CompilerParams(vmem_limit_bytes=VMEM_LIMIT),
        name="proj_sample",
    )(x2, w_row, w_t, bf_row)


KEY_CHUNK = 32


def _group_reduce(fn, x):
    parts = [x[i * SUBLANES:(i + 1) * SUBLANES] for i in range(x.shape[0] // SUBLANES)]
    while len(parts) > 1:
        parts = [fn(parts[i], parts[i + 1]) if i + 1 < len(parts) else parts[i]
                 for i in range(0, len(parts), 2)]
    return parts[0]


AUG_ROWS = 16


def _split3(x):
    hi = x.astype(BF)
    r1 = x - hi.astype(F32)
    mid = r1.astype(BF)
    lo = (r1 - mid.astype(F32)).astype(BF)
    return hi, mid, lo


def _attn_kernel(qt_ref, qn_ref, k_ref, vt_ref, lf_ref, lft_ref, sel_ref, one_ref, o_ref,
                 ct_ref, kx_ref, qx_ref, ml_ref, mb_ref, *per_head, tq, n_heads):
    qi = pl.program_id(1)
    tk = tq
    rc = KEY_CHUNK
    t = k_ref.shape[1]
    n_pairs = n_heads // 2
    s_refs, p_refs, acc_refs = (per_head[i * n_heads:(i + 1) * n_heads] for i in range(3))

    @pl.when(qi == 0)
    def _():
        ct_ref[...] = _scan_incl(lft_ref[0], 1) * LOG2E
        blk = 512 if t % 512 == 0 else t
        for r0 in range(0, t, blk):
            rows = slice(r0, r0 + blk)
            c = _scan_incl(lf_ref[0, rows, :], 0)
            if r0:
                c = c + tot
            tot = c[blk - 1:blk, :]
            hi, mid, lo = _split3(c * LOG2E)
            aug = _dot(jnp.concatenate([hi, mid, lo], axis=1), sel_ref[...]) + one_ref[...]
            for pr in range(n_pairs):
                kx_ref[pr, rows, 0:LANES] = k_ref[0, rows, pr * LANES:(pr + 1) * LANES]
                kx_ref[pr, rows, LANES:2 * LANES] = aug[:, pr * LANES:(pr + 1) * LANES].astype(BF)
        qx_ref[:, LANES + AUG_ROWS:, :] = jnp.zeros((2 * n_heads, LANES - AUG_ROWS, tq), BF)

    q0 = pl.multiple_of(qi * tq, tq)
    par = lax.rem(qi, 2)
    first = lax.broadcasted_iota(jnp.int32, (LANES, 1), 0) < HEAD_DIM
    causal = (lax.broadcasted_iota(jnp.int32, (tk, tq), 0)
              <= lax.broadcasted_iota(jnp.int32, (tk, tq), 1))
    arow = lax.broadcasted_iota(jnp.int32, (AUG_ROWS, 1), 0)
    heads = range(n_heads)

    def build_queries(src_ref, half, start):
        for pair in range(n_pairs):
            qtp = src_ref[0, pair * LANES:(pair + 1) * LANES, :]
            zero = jnp.zeros_like(qtp)
            qx_ref[half * n_heads + 2 * pair, 0:LANES, :] = jnp.where(first, qtp, zero)
            qx_ref[half * n_heads + 2 * pair + 1, 0:LANES, :] = jnp.where(first, zero, qtp)
        for h in heads:
            hf = h % 2
            ci = ct_ref[h:h + 1, pl.ds(start, tq)]
            hi = ci.astype(BF).astype(F32)
            r1 = ci - hi
            mid = r1.astype(BF).astype(F32)
            lo = r1 - mid
            pick = ((arow >= 3 * hf) & (arow < 3 * hf + 3)).astype(F32)
            blk = jnp.where(arow == 6, hi, jnp.where(arow == 7, mid, jnp.where(arow == 8, lo, pick)))
            qx_ref[half * n_heads + h, LANES:LANES + AUG_ROWS, :] = blk.astype(BF)

    def scores(jb, masked, half):
        k0 = pl.multiple_of(jb * tk, tk)
        out = []
        for h in heads:
            s = _dot(kx_ref[h // 2, pl.ds(k0, tk), :], qx_ref[half * n_heads + h])
            out.append(jnp.where(causal, s, NEG_INF) if masked else s)
        return out

    def keep(h, s):
        s_refs[h][...] = s
        mb_ref[h:h + 1, :] = jnp.max(_group_reduce(jnp.maximum, s), axis=0, keepdims=True)

    def consume(jb, nxt=None):
        k0 = pl.multiple_of(jb * tk, tk)
        m_new, alpha = [], []
        for h in heads:
            m_old = ml_ref[2 * h:2 * h + 1, :]
            m_new.append(jnp.maximum(m_old, mb_ref[h:h + 1, :]))
            alpha.append(jnp.exp2(m_old - m_new[h]))
            ml_ref[2 * h:2 * h + 1, :] = m_new[h]
        for h in heads:
            for r in range(tk // rc):
                p = jnp.exp2(s_refs[h][r * rc:(r + 1) * rc, :] - m_new[h])
                p_refs[h][r * rc:(r + 1) * rc, :] = p.astype(BF)
            if nxt is not None:
                keep(h, nxt[h])
        ones = jnp.ones((BF_ROWS, tk), BF)
        for h in heads:
            vh = jnp.concatenate([vt_ref[0, h * HEAD_DIM:(h + 1) * HEAD_DIM, pl.ds(k0, tk)], ones], axis=0)
            acc_refs[h][...] = alpha[h] * acc_refs[h][...] + _dot(vh, p_refs[h][...])

    ml_ref[...] = jnp.full((2 * n_heads, tq), NEG_INF, F32)
    for h in heads:
        acc_refs[h][...] = jnp.zeros((HEAD_DIM + BF_ROWS, tq), F32)

    @pl.when(qi == 0)
    def _():
        build_queries(qt_ref, 0, 0)
        for h, s in enumerate(scores(0, True, 0)):
            keep(h, s)

    @pl.when(qi > 0)
    def _():
        def body(j, _):
            consume(j, scores(j + 1, False, par))
            return 0

        lax.fori_loop(0, qi - 1, body, 0)
        consume(qi - 1, scores(qi, True, par))

    has_next = qi + 1 < pl.num_programs(1)

    @pl.when(has_next)
    def _():
        build_queries(qn_ref, 1 - par, pl.multiple_of(q0 + tq, tq))
        consume(qi, scores(0, False, 1 - par))

    @pl.when(jnp.logical_not(has_next))
    def _():
        consume(qi)

    for pair in range(n_heads // 2):
        h0, h1 = 2 * pair, 2 * pair + 1
        out_t = jnp.concatenate(
            [acc_refs[h][0:HEAD_DIM, :] / acc_refs[h][HEAD_DIM:HEAD_DIM + 1, :] for h in (h0, h1)],
            axis=0)
        o_ref[0, :, pair * LANES:(pair + 1) * LANES] = out_t.T.astype(o_ref.dtype)


def _attn(qt, kb, vtb, lf, lft, tq):
    b, w, t = qt.shape
    n_heads = w // HEAD_DIM
    n_pairs = n_heads // 2
    sel = np.zeros((3 * LANES, n_pairs * LANES), np.float32)
    one = np.zeros((1, n_pairs * LANES), np.float32)
    for pr in range(n_pairs):
        for hf in range(2):
            for piece in range(3):
                sel[piece * LANES + 2 * pr + hf, pr * LANES + 3 * hf + piece] = -1.0
        one[0, pr * LANES + 6:pr * LANES + 9] = 1.0
    kern = functools.partial(_attn_kernel, tq=tq, n_heads=n_heads)
    per_b = lambda s1, s2: pl.BlockSpec((1, s1, s2), lambda i, j: (i, 0, 0))
    return pl.pallas_call(
        kern,
        grid=(b, t // tq),
        in_specs=[pl.BlockSpec((1, w, tq), lambda i, j: (i, 0, j)),
                  pl.BlockSpec((1, w, tq), lambda i, j: (i, 0, jnp.minimum(j + 1, t // tq - 1))),
                  per_b(t, w), per_b(w, t), per_b(t, LANES), per_b(n_heads, t),
                  _const_spec(sel.shape), _const_spec(one.shape)],
        out_specs=pl.BlockSpec((1, tq, w), lambda i, j: (i, j, 0)),
        out_shape=jax.ShapeDtypeStruct((b, t, w), BF),
        scratch_shapes=[pltpu.VMEM((n_heads, t), F32),
                        pltpu.VMEM((n_pairs, t, 2 * LANES), BF),
                        pltpu.VMEM((2 * n_heads, 2 * LANES, tq), BF),
                        pltpu.VMEM((2 * n_heads, tq), F32), pltpu.VMEM((n_heads, tq), F32)]
                       + [pltpu.VMEM((tq, tq), F32)] * n_heads
                       + [pltpu.VMEM((tq, tq), BF)] * n_heads
                       + [pltpu.VMEM((HEAD_DIM + BF_ROWS, tq), F32)] * n_heads,
        compiler_params=_cparams(("parallel", "arbitrary")),
        name="attn_prompt",
    )(qt, qt, kb, vtb, lf, lft, jnp.asarray(sel, BF), jnp.asarray(one, F32))


def _page_copies(step, slot, ng, pages, pt_ref, ck_hbm, cv_hbm, kbuf, vbuf, sem):
    bb = step // ng
    first = (ng - 1 - step % ng) * pages
    out = []
    for i in range(pages):
        pid = pt_ref[bb, first + i]
        out.append(pltpu.make_async_copy(ck_hbm.at[pid], kbuf.at[slot, i], sem.at[slot, 0]))
        out.append(pltpu.make_async_copy(cv_hbm.at[pid], vbuf.at[slot, i], sem.at[slot, 1]))
    return out


def _paged_step(step, n, ng, total, pt_ref, q_ref, kn_ref, vn_ref, lf_ref, lft_ref, tri_ref,
                clf_ref, ck_hbm, cv_hbm, o_ref, qbd_ref, ci_ref, m_ref, l_ref, acc_ref, carry_ref,
                kbuf, vbuf, sem, *, n_heads, pages, tq):
    b = step // ng
    g = step % ng
    mp = n * pages
    first_page = (ng - g - n) * pages
    rows = n_heads * tq
    w = q_ref.shape[2]
    dma = (pt_ref, ck_hbm, cv_hbm, kbuf, vbuf, sem)
    order = [(k, i) for k in reversed(range(n)) for i in range(pages)]

    for k in range(n):
        for cp in _page_copies(step + k, k, ng, pages, *dma):
            cp.wait()

    def expand(x):
        return jnp.concatenate(
            [jnp.broadcast_to(x[h:h + 1, :], (tq, x.shape[1])) for h in range(n_heads)], axis=0)

    def update(s, pv_fn):
        m = m_ref[...]
        m_new = jnp.maximum(m, jnp.max(s, axis=-1, keepdims=True))
        alpha = jnp.exp(m - m_new)
        p = jnp.exp(s - m_new)
        l_ref[...] = alpha * l_ref[...] + jnp.sum(p, axis=-1, keepdims=True)
        acc_ref[...] = alpha * acc_ref[...] + pv_fn(p.astype(BF))
        m_ref[...] = m_new

    @pl.when(g == 0)
    def _():
        q = q_ref[0]
        lane = lax.broadcasted_iota(jnp.int32, (1, w), 1)
        zero = jnp.zeros_like(q)
        qbd_ref[...] = jnp.concatenate(
            [jnp.where((lane >= h * HEAD_DIM) & (lane < (h + 1) * HEAD_DIM), q, zero)
             for h in range(n_heads)], axis=0)
        c = _scan_incl(lf_ref[0], 0)
        ci_ref[...] = jnp.concatenate([c[:, h:h + 1] for h in range(n_heads)], axis=0)
        m_ref[...] = jnp.full((rows, 1), NEG_INF, F32)
        l_ref[...] = jnp.zeros((rows, 1), F32)
        acc_ref[...] = jnp.zeros((rows, w), F32)
        carry_ref[...] = jnp.zeros((n_heads, PAGE_SIZE), F32)
        lfn = lft_ref[0]
        pieces = [p.astype(F32) for p in _split3(lfn)] + [jnp.zeros_like(lfn)]
        yn = _dot(jnp.concatenate(pieces, axis=0).astype(BF), tri_ref[...])
        yn = yn[0:n_heads] + yn[n_heads:2 * n_heads] + yn[2 * n_heads:3 * n_heads]
        cj = yn[:, PAGE_SIZE:2 * PAGE_SIZE] - yn[:, 0:PAGE_SIZE] + lfn
        s = _dot_nt(qbd_ref[...], kn_ref[0]) + ci_ref[...] - expand(cj)
        r_idx = lax.broadcasted_iota(jnp.int32, (rows, LANES), 0)
        c_idx = lax.broadcasted_iota(jnp.int32, (rows, LANES), 1)
        s = jnp.where(c_idx <= jnp.bitwise_and(r_idx, tq - 1), s, NEG_INF)
        update(s, lambda p: _dot(p, vn_ref[0]))

    ktc = jnp.concatenate([kbuf[k, i].reshape(w, PAGE_SIZE) for k, i in order], axis=1).astype(BF)
    vtc = jnp.concatenate([vbuf[k, i].reshape(w, PAGE_SIZE) for k, i in order], axis=1).astype(BF)
    lf = jnp.concatenate([clf_ref[pt_ref[b, first_page + i]] for i in range(mp)], axis=0)
    hi, mid, lo = _split3(lf)
    y = _dot(jnp.concatenate([hi, mid, lo], axis=0), tri_ref[...])
    ph = mp * n_heads
    y = y[0:ph] + y[ph:2 * ph] + y[2 * ph:3 * ph]
    run = carry_ref[...]
    cjs = [None] * mp
    for pg in reversed(range(mp)):
        blk = slice(pg * n_heads, (pg + 1) * n_heads)
        cjs[pg] = -(run + (y[blk, 0:PAGE_SIZE] - lf[blk]))
        run = run + y[blk, PAGE_SIZE:2 * PAGE_SIZE]
    carry_ref[...] = run
    cj = jnp.concatenate(cjs, axis=1)
    s = _dot(qbd_ref[...], ktc) + ci_ref[...] - expand(cj)
    update(s, lambda p: _dot_nt(p, vtc))

    @pl.when(g + n == ng)
    def _():
        lane = lax.broadcasted_iota(jnp.int32, (1, w), 1)
        out = jnp.zeros((tq, w), F32)
        for h in range(n_heads):
            sl = slice(h * tq, (h + 1) * tq)
            val = acc_ref[sl, :] / l_ref[sl, :]
            out = jnp.where((lane >= h * HEAD_DIM) & (lane < (h + 1) * HEAD_DIM), val, out)
        o_ref[0] = out.astype(o_ref.dtype)

    @pl.when(step + n < total)
    def _():
        for k in range(n):
            for cp in _page_copies(step + n + k, k, ng, pages, *dma):
                cp.start()


def _paged_operands(q, cache_kt, cache_lft, pages, slots, seq_of_step):
    db, tq, w = q.shape
    n_heads = w // HEAD_DIM
    rows = n_heads * tq
    assert tq & (tq - 1) == 0
    per_seq = lambda s1, s2: pl.BlockSpec((1, s1, s2), lambda i, j, pt: (seq_of_step(i, j), 0, 0))
    in_specs = [per_seq(tq, w), per_seq(PAGE_SIZE, w), per_seq(PAGE_SIZE, w), per_seq(tq, LANES),
                per_seq(n_heads, LANES),
                _const_spec((PAGE_SIZE, 2 * PAGE_SIZE)), _const_spec(cache_lft.shape),
                pl.BlockSpec(memory_space=pl.ANY), pl.BlockSpec(memory_space=pl.ANY)]
    page_buf = pltpu.VMEM((slots, pages, n_heads, HEAD_DIM, PAGE_SIZE), cache_kt.dtype)
    scratch = [pltpu.VMEM((rows, w), BF), pltpu.VMEM((rows, 1), F32),
               pltpu.VMEM((rows, 1), F32), pltpu.VMEM((rows, 1), F32),
               pltpu.VMEM((rows, w), F32), pltpu.VMEM((n_heads, PAGE_SIZE), F32),
               page_buf, page_buf, pltpu.SemaphoreType.DMA((slots, 2))]
    pos = lax.broadcasted_iota(jnp.int32, (PAGE_SIZE, PAGE_SIZE), 0)
    col = lax.broadcasted_iota(jnp.int32, (PAGE_SIZE, PAGE_SIZE), 1)
    tri = jnp.concatenate([(pos >= col).astype(BF), jnp.ones((PAGE_SIZE, PAGE_SIZE), BF)], axis=1)
    return in_specs, per_seq(tq, w), scratch, tri


def _mix_tail(pooled, att, x, wp_ref, ps_ref, wo_ref, g_ref, b_ref, alpha):
    gd = pooled[0].shape[1]
    att_w = att.shape[1]
    mixed = [_dot(pooled[g].astype(BF), wp_ref[g]) * ps_ref[:, g * gd:(g + 1) * gd]
             for g in range(len(pooled))]
    pool = jnp.concatenate(mixed, axis=1).astype(BF)
    mix = _dot(att, wo_ref[0:att_w, :]) + _dot(pool, wo_ref[att_w:, :])
    return _layer_norm(alpha * x + mix, g_ref[...], b_ref[...])


def _shift_down(x, s):
    n8, c = x.shape[0] // SUBLANES, x.shape[1]
    if s == SUBLANES:
        return jnp.concatenate([x[0:SUBLANES], x[:-SUBLANES]], axis=0)
    r = pltpu.roll(x.reshape(n8, SUBLANES, c), s, 1)
    prev = jnp.concatenate([r[0:1], r[:-1]], axis=0)
    sub = lax.broadcasted_iota(jnp.int32, (1, SUBLANES, c), 1)
    return jnp.where(sub < s, prev, r).reshape(n8 * SUBLANES, c)


def _mix_prompt_kernel(x_ref, att_ref, u_ref, halo_ref, wp_ref, ps_ref, wo_ref, g_ref, b_ref,
                       o_ref, *, tm, sub, alpha):
    t = pl.program_id(1)
    hl = MAX_WINDOW
    started = (t > 0).astype(F32)
    gd = u_ref.shape[2] // len(POOL_WINDOWS)
    att_w = att_ref.shape[2]
    blocks = [slice(r0, r0 + sub) for r0 in range(0, tm, sub)]
    pooled = []
    for rows in blocks:
        pos = t * tm + rows.start + lax.broadcasted_iota(jnp.int32, (sub, 1), 0)
        per_group = []
        for g, w in enumerate(POOL_WINDOWS):
            lanes = slice(g * gd, (g + 1) * gd)
            tok = u_ref[0, rows, lanes]
            halo = (halo_ref[0, :, lanes] * started if rows.start == 0
                    else u_ref[0, rows.start - hl:rows.start, lanes])
            win = jnp.concatenate([halo, tok], axis=0)
            k = 1
            while k < w:
                win = win + _shift_down(win, k)
                k *= 2
            cnt = jnp.minimum(pos + 1, w).astype(F32)
            per_group.append(win[hl:] / cnt - tok)
        pooled.append(per_group)
    pools = [jnp.concatenate([_dot(pg[g].astype(BF), wp_ref[g]) * ps_ref[:, g * gd:(g + 1) * gd]
                              for g in range(len(pg))], axis=1).astype(BF) for pg in pooled]
    mixes = [_dot(att_ref[0, rows, :], wo_ref[0:att_w, :]) + _dot(pool, wo_ref[att_w:, :])
             for rows, pool in zip(blocks, pools)]
    for rows, mix in zip(blocks, mixes):
        o_ref[0, rows, :] = _layer_norm(alpha * x_ref[0, rows, :] + mix, g_ref[...], b_ref[...])


def _mix_prompt(x, att, u, w_pool, pool_scale, w_out, g, b, alpha, tm):
    bsz, t, d = x.shape
    att_w, pool_w = att.shape[2], u.shape[2]
    hl = MAX_WINDOW
    ng, gd = w_pool.shape[0], w_pool.shape[1]
    kern = functools.partial(_mix_prompt_kernel, tm=tm, sub=min(tm, 256), alpha=alpha)
    row = lambda w: pl.BlockSpec((1, tm, w), lambda i, j: (i, j, 0))
    halo = pl.BlockSpec((1, hl, pool_w), lambda i, j: (i, jnp.maximum(j * (tm // hl) - 1, 0), 0))
    return pl.pallas_call(
        kern,
        grid=(bsz, t // tm),
        in_specs=[row(d), row(att_w), row(pool_w), halo,
                  _const_spec((ng, gd, gd)), _const_spec((1, pool_w)), _const_spec((att_w + pool_w, d)),
                  _const_spec((1, d)), _const_spec((1, d))],
        out_specs=row(d),
        out_shape=jax.ShapeDtypeStruct((bsz, t, d), F32),
        compiler_params=_cparams(("parallel", "parallel")),
        name="mix_prompt",
    )(x, att, u, u, w_pool, pool_scale, w_out, g, b)


def _mix_sample_kernel(x_ref, att_ref, full_ref, wp_ref, ps_ref, wo_ref, g_ref, b_ref, o_ref,
                       *, ts, n_past, alpha):
    nseq = full_ref.shape[0]
    hl = MAX_WINDOW
    gd = full_ref.shape[2] // len(POOL_WINDOWS)
    pos = n_past + lax.broadcasted_iota(jnp.int32, (1, ts, 1), 1)
    pooled = []
    for g, w in enumerate(POOL_WINDOWS):
        lanes = slice(g * gd, (g + 1) * gd)
        tok = full_ref[:, hl:hl + ts, lanes]
        acc = tok
        for k in range(1, w):
            acc = acc + full_ref[:, hl - k:hl - k + ts, lanes]
        cnt = jnp.minimum(pos + 1, w).astype(F32)
        pooled.append((acc / cnt - tok).reshape(nseq * ts, gd))
    o_ref[...] = _mix_tail(pooled, att_ref[...], x_ref[...], wp_ref, ps_ref, wo_ref, g_ref, b_ref, alpha)


def _mix_sample(x2, att2, full, w_pool, pool_scale, w_out, g, b, alpha, ts, n_past):
    n, d = x2.shape
    kern = functools.partial(_mix_sample_kernel, ts=ts, n_past=n_past, alpha=alpha)
    return pl.pallas_call(
        kern,
        out_shape=jax.ShapeDtypeStruct((n, d), F32),
        compiler_params=pltpu.CompilerParams(vmem_limit_bytes=VMEM_LIMIT),
        name="mix_sample",
    )(x2, att2, full, w_pool, pool_scale, w_out, g, b)


def _shift_rows(h, s):
    n8, c = h.shape[0] // SUBLANES, h.shape[1]
    r = pltpu.roll(h.reshape(n8, SUBLANES, c), s, 1)
    sub = lax.broadcasted_iota(jnp.int32, (1, SUBLANES, c), 1)
    return jnp.where(sub < s, r[:-1], r[1:]).reshape((n8 - 1) * SUBLANES, c)


def _ffn_chunk(hg3, hv3, wc_ref, bc_ref, gcols, vcols, wcv_ref=None, bcv_ref=None):
    def conv(h3, w_ref, b_ref, cols, scale):
        hc = b_ref[:, cols] * scale + h3[0] * (w_ref[0:1, cols] * scale)
        for j in range(1, CONV_WIDTH):
            hc = hc + h3[j] * (w_ref[j:j + 1, cols] * scale)
        return hc
    g = conv(hg3, wc_ref, bc_ref, gcols, 1.0)
    vh = conv(hv3, wc_ref if wcv_ref is None else wcv_ref, bc_ref if bcv_ref is None else bcv_ref,
              vcols, 0.5)
    t = jnp.tanh(g * (GELU_C1 + (GELU_C1 * GELU_C0) * (g * g)))
    return ((g * (1.0 + t)) * vh).astype(BF)


def _ffn_prompt_kernel(pt_ref, x_ref, halo_ref, wu_ref, bu_ref, wc_ref, bc_ref, wd_ref, g_ref, b_ref,
                       *rest, tm, d_ff, chunk, alpha, paged):
    sub, ng, total = paged["sub"], paged["ng"], paged["total"]
    pa_in, (o_ref, tail_ref, oa_ref), pa_scratch = rest[0:9], rest[9:12], rest[12:]
    dma = (pt_ref, pa_in[7], pa_in[8], pa_scratch[6], pa_scratch[7], pa_scratch[8])
    step0 = (pl.program_id(0) * pl.num_programs(1) + pl.program_id(1)) * sub

    @pl.when(step0 == 0)
    def _():
        for k in range(sub):
            for cp in _page_copies(k, k, ng, paged["pages"], *dma):
                cp.start()

    t = pl.program_id(1)
    hl = SUBLANES
    edges = list(range(0, d_ff, chunk)) + [d_ff]
    n_chunks = len(edges) - 1
    cols_of = lambda c: (slice(edges[c], edges[c + 1]), slice(d_ff + edges[c], d_ff + edges[c + 1]))
    st = {}

    def up(c):
        return [_dot(st["xin"], wu_ref[:, cols]) for cols in cols_of(c)]

    def shifted(hd, cols):
        body = hd[hl:] + bu_ref[:, cols]
        tail_ref[0, :, cols] = body[tm - hl:tm]
        h = jnp.concatenate([(hd[0:hl] + bu_ref[:, cols]) * st["started"], body], axis=0)
        return [_shift_rows(h, CONV_WIDTH - 1 - j) for j in range(CONV_WIDTH - 1)] + [body]

    def run_chunks(c_from, c_to):
        for c in range(c_from, c_to):
            cur = st["nxt"]
            if c + 1 < n_chunks:
                st["nxt"] = up(c + 1)
            gcols, vcols = cols_of(c)
            act = _ffn_chunk(shifted(cur[0], gcols), shifted(cur[1], vcols), wc_ref, bc_ref, gcols, vcols)
            st["acc"] = st["acc"] + _dot(act, wd_ref[gcols, :])

    _paged_step(step0, sub, ng, total, pt_ref, *pa_in, oa_ref, *pa_scratch,
                n_heads=paged["n_heads"], pages=paged["pages"], tq=paged["tq"])

    st["x"] = x_ref[0]
    st["xin"] = jnp.concatenate([halo_ref[0], st["x"]], axis=0).astype(BF)
    st["started"] = (t > 0).astype(F32)
    st["acc"] = jnp.zeros((tm, st["x"].shape[1]), F32)
    st["nxt"] = up(0)
    run_chunks(0, n_chunks)
    o_ref[0] = _layer_norm(alpha * st["x"] + st["acc"], g_ref[...], b_ref[...])


def _ffn_prompt_and_paged_attn(x1, w_up, b_up, w_conv, b_conv, w_down, g, b, alpha, tm, chunk,
                               page_table, q, kn, vn, lf, lft, cache_kt, cache_vt, cache_lft):
    bsz, t, d = x1.shape
    d_ff = w_down.shape[0]
    hl = SUBLANES
    nt = t // tm
    db, tq, w = q.shape
    n_pages = page_table.shape[1]
    fits = lambda p: (n_pages % p == 0 and (db * (n_pages // p)) % (bsz * nt) == 0
                      and (n_pages // p) % ((db * (n_pages // p)) // (bsz * nt)) == 0)
    pages = next(p for p in (16, 8, 4, 2, 1) if fits(p))
    ng = n_pages // pages
    total = db * ng
    sub = total // (bsz * nt)
    paged = dict(sub=sub, ng=ng, total=total, pages=pages, n_heads=w // HEAD_DIM, tq=tq)
    pa_specs, pa_out, pa_scratch, tri = _paged_operands(
        q, cache_kt, cache_lft, pages, sub, lambda i, j: ((i * nt + j) * sub) // ng)
    kern = functools.partial(_ffn_prompt_kernel, tm=tm, d_ff=d_ff, chunk=chunk, alpha=alpha, paged=paged)
    row = pl.BlockSpec((1, tm, d), lambda i, j, pt: (i, j, 0))
    halo = pl.BlockSpec((1, hl, d), lambda i, j, pt: (i, jnp.maximum(j * (tm // hl) - 1, 0), 0))
    grid_spec = pltpu.PrefetchScalarGridSpec(
        num_scalar_prefetch=1,
        grid=(bsz, nt),
        in_specs=[row, halo, _const_spec((d, 2 * d_ff)), _const_spec((1, 2 * d_ff)),
                  _const_spec((CONV_WIDTH, 2 * d_ff)), _const_spec((1, 2 * d_ff)),
                  _const_spec((d_ff, d)), _const_spec((1, d)), _const_spec((1, d))] + pa_specs,
        out_specs=[row, pl.BlockSpec((1, hl, 2 * d_ff), lambda i, j, pt: (i, 0, 0)), pa_out],
        scratch_shapes=pa_scratch)
    return pl.pallas_call(
        kern,
        grid_spec=grid_spec,
        out_shape=[jax.ShapeDtypeStruct((bsz, t, d), F32),
                   jax.ShapeDtypeStruct((bsz, hl, 2 * d_ff), F32),
                   jax.ShapeDtypeStruct((db, tq, w), BF)],
        compiler_params=_cparams(("arbitrary", "arbitrary")),
        name="ffn_prompt_attn_paged",
    )(page_table, x1, x1, w_up, b_up, w_conv, b_conv, w_down, g, b,
      q, kn, vn, lf, lft, tri, cache_lft, cache_kt, cache_vt)


def _ffn_sample_kernel(x_ref, stg_ref, stv_ref, wug_ref, wuv_ref, bug_ref, buv_ref, wcg_ref, wcv_ref,
                       bcg_ref, bcv_ref, wd_ref, g_ref, b_ref, o_ref, tg_ref, tv_ref, acc_ref, buf_ref,
                       *, ts, alpha):
    c = pl.program_id(0)
    nseq = stg_ref.shape[0]
    n = nseq * ts
    chunk = wd_ref.shape[0]
    hl = SUBLANES
    nst = CONV_WIDTH - 1
    cols = slice(0, chunk)
    xb = x_ref[...].astype(BF)

    @pl.when(c == 0)
    def _():
        acc_ref[...] = jnp.zeros(acc_ref.shape, F32)

    h3 = []
    for half, (wu, bu, st, tail) in enumerate(((wug_ref, bug_ref, stg_ref, tg_ref),
                                               (wuv_ref, buv_ref, stv_ref, tv_ref))):
        h = (_dot(xb, wu[...]) + bu[...]).reshape(nseq, ts, chunk)
        tail[...] = h[:, ts - nst:, :]
        buf_ref[half, :, hl - nst:hl, :] = st[...]
        buf_ref[half, :, hl:hl + ts, :] = h
        h3.append([buf_ref[half, :, hl - 2 + j:hl - 2 + j + ts, :].reshape(n, chunk)
                   for j in range(CONV_WIDTH)])
    act = _ffn_chunk(h3[0], h3[1], wcg_ref, bcg_ref, cols, cols, wcv_ref, bcv_ref)
    acc_ref[...] += _dot(act, wd_ref[...])

    @pl.when(c == pl.num_programs(0) - 1)
    def _():
        o_ref[...] = _layer_norm(alpha * x_ref[...] + acc_ref[...], g_ref[...], b_ref[...])


def _ffn_sample(x2, state, w_up, b_up, w_conv, b_conv, w_down, g, b, alpha, ts, chunk):
    n, d = x2.shape
    d_ff = w_down.shape[0]
    nseq = n // ts
    nc = d_ff // chunk
    nst = CONV_WIDTH - 1
    kern = functools.partial(_ffn_sample_kernel, ts=ts, alpha=alpha)
    halves = lambda shape, ax: [pl.BlockSpec(shape, lambda c, o=o: (0,) * ax + (c + o,) + (0,) * (len(shape) - ax - 1))
                                for o in (0, nc)]
    return pl.pallas_call(
        kern,
        grid=(nc,),
        in_specs=[_const_spec((n, d))] + halves((nseq, nst, chunk), 2) + halves((d, chunk), 1)
                 + halves((1, chunk), 1) + halves((CONV_WIDTH, chunk), 1) + halves((1, chunk), 1)
                 + [pl.BlockSpec((chunk, d), lambda c: (c, 0)), _const_spec((1, d)), _const_spec((1, d))],
        out_specs=[pl.BlockSpec((n, d), lambda c: (0, 0)),
                   pl.BlockSpec((nseq, nst, chunk), lambda c: (0, 0, c)),
                   pl.BlockSpec((nseq, nst, chunk), lambda c: (0, 0, c))],
        out_shape=[jax.ShapeDtypeStruct((n, d), F32),
                   jax.ShapeDtypeStruct((nseq, nst, d_ff), F32),
                   jax.ShapeDtypeStruct((nseq, nst, d_ff), F32)],
        scratch_shapes=[pltpu.VMEM((n, d), F32), pltpu.VMEM((2, nseq, SUBLANES + ts, chunk), F32)],
        compiler_params=_cparams(("arbitrary",)),
        name="ffn_sample",
    )(x2, state, state, w_up, w_up, b_up, b_up, w_conv, w_conv, b_conv, b_conv, w_down, g, b)


def _ffn_chunk_size(d_ff):
    for c in (512, 256, 128):
        if d_ff % c == 0:
            return c
    raise ValueError("d_ff must be a multiple of 128")


def kernel(x_prompt, x_sample, cache_k, cache_v, cache_logf, state_pool, state_conv, page_table,
           w_in, b_f, w_pool, pool_scale, w_out, ln1_g, ln1_b, w_up, b_up, w_conv, b_conv, w_down,
           ln2_g, ln2_b):
    bsz, t_p, d = x_prompt.shape
    db, t_s, _ = x_sample.shape
    depth = w_in.shape[0]
    n_heads = cache_k.shape[3]
    att_w = n_heads * HEAD_DIM
    pool_w = state_pool.shape[3]
    d_ff = w_down.shape[1]
    n_past = page_table.shape[1] * PAGE_SIZE
    alpha = (2 * depth) ** 0.25
    assert HEAD_DIM * 2 == LANES and n_heads % 2 == 0 and n_heads <= SUBLANES
    assert t_s == SUBLANES and cache_k.shape[2] == PAGE_SIZE and t_p >= MAX_WINDOW
    assert w_in.shape[2] == 3 * att_w + n_heads + pool_w

    tm_proj = min(1024, t_p)
    tq = min(256, t_p)
    tm_mix = min(1024, t_p)
    tm_ffn = min(256, t_p)
    chunk = _ffn_chunk_size(d_ff)

    yp, ys = x_prompt, x_sample
    outs = [[] for _ in range(10)]
    for l in range(depth):
        wi = w_in[l]
        fcols = wi[:, 3 * att_w:3 * att_w + n_heads]
        w_row = jnp.concatenate(
            [wi[:, :2 * att_w], wi[:, 3 * att_w + n_heads:], jnp.pad(fcols, ((0, 0), (0, LANES - n_heads)))],
            axis=1).astype(BF)
        w_t = jnp.concatenate(
            [wi[:, :3 * att_w].T, jnp.pad(fcols.T, ((0, BF_ROWS - n_heads), (0, 0)))],
            axis=0).astype(BF)
        bf_row = jnp.pad(b_f[l], (0, LANES - n_heads)).reshape(1, LANES)
        bf_col = b_f[l].reshape(n_heads, 1)
        wp_b = w_pool[l].astype(BF)
        ps = pool_scale[l].reshape(1, pool_w)
        wo_b = w_out[l].astype(BF)
        g1, b1 = ln1_g[l].reshape(1, d), ln1_b[l].reshape(1, d)
        wu_b = w_up[l].astype(BF)
        bu = b_up[l].reshape(1, 2 * d_ff)
        wc = w_conv[l]
        bc = b_conv[l].reshape(1, 2 * d_ff)
        wd_b = w_down[l].astype(BF)
        g2, b2 = ln2_g[l].reshape(1, d), ln2_b[l].reshape(1, d)

        qt, kb, kt, vt, vtb, u, lf, lft = _proj_prompt(yp, w_row, w_t, bf_row, bf_col, att_w, pool_w,
                                                       n_heads, tm_proj)
        n_s = db * t_s
        qs, ks, vs, kbs, vbs, us, lfs = _proj_sample(ys.reshape(n_s, d), w_row, w_t, bf_row, att_w, pool_w)
        att = _attn(qt, kb, vtb, lf, lft, tq)
        x1 = _mix_prompt(yp, att, u, wp_b, ps, wo_b, g1, b1, alpha, tm_mix)

        lfs3 = lfs.reshape(db, t_s, LANES)
        lfts = jnp.pad(jnp.swapaxes(lfs3[:, :, :n_heads], 1, 2), ((0, 0), (0, 0), (0, LANES - t_s)))
        padk = lambda a: jnp.pad(a.reshape(db, t_s, att_w), ((0, 0), (0, PAGE_SIZE - t_s), (0, 0)))
        ckt = jnp.transpose(cache_k[l], (0, 2, 3, 1))
        cvt = jnp.transpose(cache_v[l], (0, 2, 3, 1))
        clft = jnp.swapaxes(cache_logf[l], 1, 2)
        yp, tail, att_s = _ffn_prompt_and_paged_attn(
            x1, wu_b, bu, wc, bc, wd_b, g2, b2, alpha, tm_ffn, chunk,
            page_table, qs.reshape(db, t_s, att_w), padk(kbs), padk(vbs), lfs3, lfts, ckt, cvt, clft)
        to_bthd = lambda a: jnp.transpose(a.reshape(bsz, n_heads, HEAD_DIM, t_p), (0, 3, 1, 2))
        outs[0].append(to_bthd(kt))
        outs[1].append(to_bthd(vt))
        outs[2].append(jnp.swapaxes(lft, 1, 2))
        outs[3].append(u[:, t_p - (MAX_WINDOW - 1):, :])
        outs[4].append(tail[:, SUBLANES - (CONV_WIDTH - 1):, :])

        us3 = us.reshape(db, t_s, pool_w)
        full = jnp.concatenate([jnp.zeros((db, 1, pool_w), F32), state_pool[l], us3], axis=1)
        x1s = _mix_sample(ys.reshape(n_s, d), att_s.reshape(n_s, att_w), full, wp_b, ps, wo_b, g1, b1,
                          alpha, t_s, n_past)
        ys2, tail_g, tail_v = _ffn_sample(x1s, state_conv[l], wu_b, bu, wc, bc, wd_b, g2, b2, alpha, t_s, chunk)
        ys = ys2.reshape(db, t_s, d)
        outs[5].append(ks.reshape(db, t_s, n_heads, HEAD_DIM))
        outs[6].append(vs.reshape(db, t_s, n_heads, HEAD_DIM))
        outs[7].append(lfs3[:, :, :n_heads])
        outs[8].append(full[:, t_s + 1:, :])
        outs[9].append(jnp.concatenate([tail_g, tail_v], axis=-1))

    st = [jnp.stack(o, 0) for o in outs]
    return (yp, ys, st[0], st[1], st[2], st[3], st[4], st[5], st[6], st[7], st[8], st[9])
```
